```python
import jax, jax.numpy as jnp
from jax import lax
import numpy as np

D_MODEL = 2048
BATCH = 8
SEQ = 2048
DEPTH = 1

GRID_W = 64
CTX_LEN = 256
D_MIX = D_MODEL
HG_W = D_MIX // 2
HG_DK = 128
HG_H = HG_W // HG_DK
HG_DV = HG_W // HG_H
GD_W = D_MIX - HG_W
GD_DK = 128
GD_H = GD_W // GD_DK
GD_DV = GD_W // GD_H
CONV_W = 3
CHUNK = 64
N_EXPERTS = 32
TOP_K = 4
D_FF = D_MODEL
SWIGLU_LIMIT = 7.0
SWIGLU_ALPHA = 1.702
MOE_BLOCK = 128
EPS = 1e-6
IN_SIZES = (HG_W, HG_W, HG_W, HG_W, HG_W, GD_W, GD_W, GD_W, GD_W, GD_H, GD_H, GD_H, GD_H)
IN_DIM = sum(IN_SIZES)

kernel_name = "hybrid_hgrn2_gdn_moe_flow_block"


def rmsnorm(x, w):
    xf = x.astype(jnp.float32)
    y = xf * lax.rsqrt(jnp.mean(xf * xf, axis=-1, keepdims=True) + EPS)
    return (y * w.astype(jnp.float32)).astype(x.dtype)


def l2norm(x):
    return x * lax.rsqrt(jnp.sum(x * x, axis=-1, keepdims=True) + EPS)


def split_last(t, sizes):
    return jnp.split(t, np.cumsum(sizes)[:-1].tolist(), axis=-1)


def to_heads(t, h):
    b, s, w = t.shape
    return t.reshape(b, s, h, w // h).transpose(0, 2, 1, 3)


def short_conv(x, w):
    pad = CONV_W // 2
    n = x.shape[1]
    xp = jnp.pad(x, ((0, 0), (pad, pad), (0, 0)))
    return sum(xp[:, j:j + n, :] * w[j] for j in range(CONV_W))


def gla_chunked(q, k, v, logf, s0, with_out):
    b, h, t, dk = k.shape
    dv = v.shape[-1]
    n = t // CHUNK
    blk = lambda a: a.reshape(b, h, n, CHUNK, a.shape[-1])
    k, v, logf = blk(k), blk(v), blk(logf)
    cum = jnp.cumsum(logf, axis=3)
    cum_last = cum[:, :, :, -1:, :]
    u = jnp.einsum('bhncd,bhnce->bhnde', k * jnp.exp(cum_last - cum), v)
    decay = jnp.exp(cum_last[:, :, :, 0, :])

    def step(s, inp):
        dec, un = inp
        return dec[..., None] * s + un, s

    s_fin, s_start = lax.scan(step, s0, (jnp.moveaxis(decay, 2, 0), jnp.moveaxis(u, 2, 0)))
    if not with_out:
        return None, s_fin
    q = blk(q)
    s_start = jnp.moveaxis(s_start, 0, 2)
    ref = cum[:, :, :, CHUNK // 2:CHUNK // 2 + 1, :]
    scores = jnp.einsum('bhncd,bhnsd->bhncs', q * jnp.exp(cum - ref), k * jnp.exp(ref - cum))
    lower = np.tril(np.ones((CHUNK, CHUNK), bool))
    scores = jnp.where(lower, scores, 0.0)
    o = (jnp.einsum('bhncs,bhnse->bhnce', scores, v)
         + jnp.einsum('bhncd,bhnde->bhnce', q * jnp.exp(cum), s_start))
    return o.reshape(b, h, t, dv), s_fin


def gdn_chunked(q, k, v, g, beta, s0, with_out):
    b, h, t, dk = k.shape
    dv = v.shape[-1]
    n = t // CHUNK
    k = k.reshape(b, h, n, CHUNK, dk)
    v = v.reshape(b, h, n, CHUNK, dv)
    g = g.reshape(b, h, n, CHUNK)
    beta = beta.reshape(b, h, n, CHUNK)
    cum = jnp.cumsum(g, axis=-1)
    lower = np.tril(np.ones((CHUNK, CHUNK), bool))
    strict = np.tril(np.ones((CHUNK, CHUNK), bool), -1)
    dmask = jnp.exp(jnp.where(lower, cum[..., :, None] - cum[..., None, :], -jnp.inf))
    kb = k * beta[..., None]
    tri = jnp.where(strict, jnp.einsum('bhnid,bhnjd->bhnij', kb, k) * dmask, 0.0) + np.eye(CHUNK, dtype=np.float32)
    u = lax.linalg.triangular_solve(tri, v * beta[..., None], left_side=True, lower=True, unit_diagonal=True)
    w = lax.linalg.triangular_solve(tri, kb * jnp.exp(cum)[..., None], left_side=True, lower=True, unit_diagonal=True)
    cum_last = cum[..., -1]
    k_end = k * jnp.exp(cum_last[..., None] - cum)[..., None]
    mv = lambda a: jnp.moveaxis(a, 2, 0)

    def advance(s, w_n, u_n, ke_n, gl_n):
        v_new = u_n - jnp.einsum('bhcd,bhde->bhce', w_n, s)
        s_new = s * jnp.exp(gl_n)[..., None, None] + jnp.einsum('bhcd,bhce->bhde', ke_n, v_new)
        return v_new, s_new

    if not with_out:
        def step_state(s, inp):
            _, s_new = advance(s, *inp)
            return s_new, None
        s_fin, _ = lax.scan(step_state, s0, tuple(mv(a) for a in (w, u, k_end, cum_last)))
        return None, s_fin
    q = q.reshape(b, h, n, CHUNK, dk)
    attn = jnp.einsum('bhnid,bhnjd->bhnij', q, k) * dmask
    q_dec = q * jnp.exp(cum)[..., None]

    def step(s, inp):
        w_n, u_n, ke_n, gl_n, qd_n, at_n = inp
        v_new, s_new = advance(s, w_n, u_n, ke_n, gl_n)
        o_n = jnp.einsum('bhcd,bhde->bhce', qd_n, s) + jnp.einsum('bhcs,bhse->bhce', at_n, v_new)
        return s_new, o_n

    s_fin, o = lax.scan(step, s0, tuple(mv(a) for a in (w, u, k_end, cum_last, q_dec, attn)))
    return jnp.moveaxis(o, 0, 2).reshape(b, h, t, dv), s_fin


def hgrn2_direction(q, zf, v, lb, s0, reverse, with_out):
    if reverse:
        zf, v = jnp.flip(zf, 1), jnp.flip(v, 1)
        q = None if q is None else jnp.flip(q, 1)
    logf = jnp.log(lb + (1.0 - lb) * jax.nn.sigmoid(zf))
    k = (1.0 - lb) * jax.nn.sigmoid(-zf)
    qh = to_heads(q, HG_H) if with_out else None
    o, s = gla_chunked(qh, to_heads(k, HG_H), to_heads(v, HG_H), to_heads(logf, HG_H), s0, with_out)
    if with_out and reverse:
        o = jnp.flip(o, 2)
    return o, s


def gdn_inputs(q, k, v, conv_w, grid):
    wq, wk, wv = split_last(conv_w, (GD_W, GD_W, GD_W))

    def prep(a, w):
        bsz, t, ch = a.shape
        if grid:
            rows = t // GRID_W
            a = short_conv(a.reshape(bsz * rows, GRID_W, ch), w).reshape(bsz, t, ch)
        else:
            a = short_conv(a, w)
        return to_heads(jax.nn.silu(a), GD_H)

    qh = None if q is None else l2norm(prep(q, wq)) * GD_DK ** -0.5
    return qh, l2norm(prep(k, wk)), prep(v, wv)


def gdn_direction(q, k, v, a, bb, a_log, dt_bias, s0, reverse, with_out):
    g = (-jnp.exp(a_log) * jax.nn.softplus(a + dt_bias)).transpose(0, 2, 1)
    beta = jax.nn.sigmoid(bb).transpose(0, 2, 1)
    if reverse:
        k, v, g, beta = (jnp.flip(t_, 2) for t_ in (k, v, g, beta))
        q = None if q is None else jnp.flip(q, 2)
    o, s = gdn_chunked(q, k, v, g, beta, s0, with_out)
    if with_out and reverse:
        o = jnp.flip(o, 2)
    return o, s


def gated_head_norm(o, z, w, h):
    bsz, _, t, d = o.shape
    on = rmsnorm(o.transpose(0, 2, 1, 3), w)
    return (on * jax.nn.silu(z.reshape(bsz, t, h, d))).reshape(bsz, t, h * d)


def token_mixers(px, pc, lb_f, lb_b, hg_norm_w, conv_w, a_log_f, a_log_b,
                 dt_bias_f, dt_bias_b, gd_norm_w, ctx_out):
    f32 = jnp.float32
    bsz = px.shape[0]
    (hq_x, hff_x, hfb_x, hi_x, hg_x, gq_x, gk_x, gv_x, gz_x,
     gaf_x, gab_x, gbf_x, gbb_x) = split_last(px.astype(f32), IN_SIZES)
    (hq_c, hff_c, hfb_c, hi_c, hg_c, gq_c, gk_c, gv_c, gz_c,
     gaf_c, gab_c, gbf_c, gbb_c) = split_last(pc.astype(f32), IN_SIZES)
    s0 = jnp.zeros((bsz, HG_H, HG_DK, HG_DV), f32)
    hq_c_use = hq_c if ctx_out else None
    oc_f, sc_f = hgrn2_direction(hq_c_use, hff_c, hi_c, lb_f, s0, False, ctx_out)
    oc_b, sc_b = hgrn2_direction(hq_c_use, hfb_c, hi_c, lb_b, s0, True, ctx_out)
    ox_f, _ = hgrn2_direction(hq_x, hff_x, hi_x, lb_f, sc_f, False, True)
    ox_b, _ = hgrn2_direction(hq_x, hfb_x, hi_x, lb_b, sc_b, True, True)
    hg_out_x = gated_head_norm(ox_f + ox_b, hg_x, hg_norm_w, HG_H)
    qx, kx, vx = gdn_inputs(gq_x, gk_x, gv_x, conv_w, True)
    qc, kc, vc = gdn_inputs(gq_c if ctx_out else None, gk_c, gv_c, conv_w, False)
    z0 = jnp.zeros((bsz, GD_H, GD_DK, GD_DV), f32)
    pc_f, tc_f = gdn_direction(qc, kc, vc, gaf_c, gbf_c, a_log_f, dt_bias_f, z0, False, ctx_out)
    pc_b, tc_b = gdn_direction(qc, kc, vc, gab_c, gbb_c, a_log_b, dt_bias_b, z0, True, ctx_out)
    px_f, _ = gdn_direction(qx, kx, vx, gaf_x, gbf_x, a_log_f, dt_bias_f, tc_f, False, True)
    px_b, _ = gdn_direction(qx, kx, vx, gab_x, gbb_x, a_log_b, dt_bias_b, tc_b, True, True)
    gd_out_x = gated_head_norm(px_f + px_b, gz_x, gd_norm_w, GD_H)
    mix_x = jnp.concatenate([hg_out_x, gd_out_x], axis=-1).astype(px.dtype)
    if not ctx_out:
        return mix_x, None
    mix_c = jnp.concatenate([gated_head_norm(oc_f + oc_b, hg_c, hg_norm_w, HG_H),
                             gated_head_norm(pc_f + pc_b, gz_c, gd_norm_w, GD_H)], axis=-1)
    return mix_x, mix_c.astype(pc.dtype)


def moe(h, w_router, b_router, w_gate, b_gate, w_up, b_up, w_down, b_down):
    n_tok, d = h.shape
    logits = jnp.dot(h, w_router).astype(jnp.float32) + b_router
    top_val, top_idx = lax.top_k(logits, TOP_K)
    gates = jax.nn.softmax(top_val, axis=-1)
    m = n_tok * TOP_K
    e_flat = top_idx.reshape(m)
    order = jnp.argsort(e_flat)
    e_s = e_flat[order]
    tok_s = (order // TOP_K).astype(jnp.int32)
    g_s = gates.reshape(m)[order]
    counts = jax.ops.segment_sum(jnp.ones((m,), jnp.int32), e_flat, num_segments=N_EXPERTS)
    padded = (counts + MOE_BLOCK - 1) // MOE_BLOCK * MOE_BLOCK
    start = jnp.cumsum(counts) - counts
    pend = jnp.cumsum(padded)
    pstart = pend - padded
    dest = pstart[e_s] + (jnp.arange(m, dtype=jnp.int32) - start[e_s])
    n_blocks = -(-m // MOE_BLOCK) + N_EXPERTS
    row_tok = jnp.full((n_blocks * MOE_BLOCK,), n_tok, jnp.int32).at[dest].set(tok_s)
    h_pad = jnp.concatenate([h, jnp.zeros((1, d), h.dtype)], axis=0)
    xs = h_pad[row_tok].reshape(n_blocks, MOE_BLOCK, d)
    blk_exp = jnp.minimum(jnp.searchsorted(pend, jnp.arange(n_blocks, dtype=jnp.int32) * MOE_BLOCK,
                                           side='right'), N_EXPERTS - 1)

    def expert_block(args):
        xb, e = args
        gate = jnp.minimum(xb @ w_gate[e] + b_gate[e], SWIGLU_LIMIT)
        up = jnp.clip(xb @ w_up[e] + b_up[e], -SWIGLU_LIMIT, SWIGLU_LIMIT)
        act = (up + 1.0) * gate * jax.nn.sigmoid(SWIGLU_ALPHA * gate)
        return act @ w_down[e] + b_down[e]

    ys = lax.map(expert_block, (xs, blk_exp)).reshape(n_blocks * MOE_BLOCK, d)
    out = jnp.zeros((n_tok, d), jnp.float32).at[tok_s].add(ys[dest].astype(jnp.float32) * g_s[:, None])
    return out.astype(h.dtype)


def setup_inputs(seed: int = 0) -> dict:
    key = jax.random.key(seed)
    ks = jax.random.split(key, 32)
    f32 = jnp.float32
    d = D_MODEL
    nrm = lambda k, shape, scale: jax.random.normal(k, shape, f32) * scale

    def inv_softplus_dt(k, shape):
        dt = jnp.exp(jax.random.uniform(k, shape, f32, np.log(1e-3), np.log(1e-1)))
        return dt + jnp.log(-jnp.expm1(-dt))

    return {
        "x": nrm(ks[0], (BATCH, SEQ, d), 1.0),
        "c": nrm(ks[1], (BATCH, d), 1.0),
        "ctx": nrm(ks[2], (BATCH, CTX_LEN, d), 1.0),
        "c_ctx": nrm(ks[3], (d,), 1.0),
        "w_ada": nrm(ks[4], (DEPTH, d, 6 * d), 0.5 * d ** -0.5),
        "b_ada": nrm(ks[5], (DEPTH, 6 * d), 0.02),
        "norm_mix_w": 1.0 + nrm(ks[6], (DEPTH, d), 0.02),
        "w_in": nrm(ks[7], (DEPTH, d, IN_DIM), d ** -0.5),
        "hg_lb_f": nrm(ks[8], (DEPTH + 1, HG_W), 0.1),
        "hg_lb_b": nrm(ks[9], (DEPTH + 1, HG_W), 0.1),
        "hg_norm_w": 1.0 + nrm(ks[10], (DEPTH, HG_DV), 0.02),
        "gd_conv_w": nrm(ks[11], (DEPTH, CONV_W, 3 * GD_W), CONV_W ** -0.5),
        "gd_a_log_f": jnp.log(jax.random.uniform(ks[12], (DEPTH, GD_H), f32, 1.0, 16.0)),
        "gd_a_log_b": jnp.log(jax.random.uniform(ks[13], (DEPTH, GD_H), f32, 1.0, 16.0)),
        "gd_dt_bias_f": inv_softplus_dt(ks[14], (DEPTH, GD_H)),
        "gd_dt_bias_b": inv_softplus_dt(ks[15], (DEPTH, GD_H)),
        "gd_norm_w": 1.0 + nrm(ks[16], (DEPTH, GD_DV), 0.02),
        "w_out": nrm(ks[17], (DEPTH, D_MIX, d), D_MIX ** -0.5),
        "norm_ffn_w": 1.0 + nrm(ks[18], (DEPTH, d), 0.02),
        "w_router": nrm(ks[19], (DEPTH, d, N_EXPERTS), d ** -0.5),
        "b_router": nrm(ks[20], (DEPTH, N_EXPERTS), 0.01),
        "w_gate": nrm(ks[21], (DEPTH, N_EXPERTS, d, D_FF), d ** -0.5),
        "b_gate": nrm(ks[22], (DEPTH, N_EXPERTS, D_FF), 0.02),
        "w_up": nrm(ks[23], (DEPTH, N_EXPERTS, d, D_FF), d ** -0.5),
        "b_up": nrm(ks[24], (DEPTH, N_EXPERTS, D_FF), 0.02),
        "w_down": nrm(ks[25], (DEPTH, N_EXPERTS, D_FF, d), D_FF ** -0.5),
        "b_down": nrm(ks[26], (DEPTH, N_EXPERTS, d), 0.02),
        "norm_out_w": 1.0 + nrm(ks[27], (d,), 0.02),
    }


def reference(x, c, ctx, c_ctx, w_ada, b_ada, norm_mix_w, w_in, hg_lb_f, hg_lb_b, hg_norm_w,
              gd_conv_w, gd_a_log_f, gd_a_log_b, gd_dt_bias_f, gd_dt_bias_b, gd_norm_w, w_out,
              norm_ffn_w, w_router, b_router, w_gate, b_gate, w_up, b_up, w_down, b_down, norm_out_w):
    bsz, t, d = x.shape
    lb_f_all = jnp.cumsum(jax.nn.softmax(hg_lb_f.astype(jnp.float32), axis=0), axis=0)
    lb_b_all = jnp.cumsum(jax.nn.softmax(hg_lb_b.astype(jnp.float32), axis=0), axis=0)
    for l in range(DEPTH):
        ctx_out = l < DEPTH - 1
        mod_x = jax.nn.silu(c) @ w_ada[l] + b_ada[l]
        mod_c = jax.nn.silu(c_ctx) @ w_ada[l] + b_ada[l]
        sh1, sc1, gt1, sh2, sc2, gt2 = jnp.split(mod_x[:, None, :], 6, axis=-1)
        csh1, csc1, cgt1, csh2, csc2, cgt2 = jnp.split(mod_c, 6, axis=-1)
        hx = rmsnorm(x, norm_mix_w[l]) * (1.0 + sc1) + sh1
        hc = rmsnorm(ctx, norm_mix_w[l]) * (1.0 + csc1) + csh1
        mix_x, mix_c = token_mixers(hx @ w_in[l], hc @ w_in[l], lb_f_all[l], lb_b_all[l], hg_norm_w[l],
                                    gd_conv_w[l], gd_a_log_f[l], gd_a_log_b[l], gd_dt_bias_f[l],
                                    gd_dt_bias_b[l], gd_norm_w[l], ctx_out)
        x = x + gt1 * (mix_x @ w_out[l])
        hx2 = rmsnorm(x, norm_ffn_w[l]) * (1.0 + sc2) + sh2
        x = x + gt2 * moe(hx2.reshape(bsz * t, d), w_router[l], b_router[l], w_gate[l], b_gate[l],
                          w_up[l], b_up[l], w_down[l], b_down[l]).reshape(bsz, t, d)
        if ctx_out:
            ctx = ctx + cgt1 * (mix_c @ w_out[l])
            hc2 = rmsnorm(ctx, norm_ffn_w[l]) * (1.0 + csc2) + csh2
            ctx = ctx + cgt2 * moe(hc2.reshape(-1, d), w_router[l], b_router[l], w_gate[l], b_gate[l],
                                   w_up[l], b_up[l], w_down[l], b_down[l]).reshape(ctx.shape)
    return rmsnorm(x, norm_out_w)
```

```python
import functools

import jax
import jax.numpy as jnp
from jax import lax
from jax.experimental import pallas as pl
from jax.experimental.pallas import tpu as pltpu

F32 = jnp.float32
BF16 = jnp.bfloat16
HIGHEST = lax.Precision.HIGHEST

EPS = 1e-6
CHUNK = 64
HEAD_DIM = 128
N_HEADS = 8
GRID_W = 64
TOP_K = 4
SWIGLU_LIMIT = 7.0
SWIGLU_ALPHA = 1.702
LANES = 128
NEG_BIG = -1e30

MOE_TM = 512
MOE_TF = 1024
VMEM_LIMIT = 56 * 1024 * 1024


def _cparams(*sem):
    return pltpu.CompilerParams(dimension_semantics=sem, vmem_limit_bytes=VMEM_LIMIT)


def _dot(a, b):
    return jnp.dot(a, b, preferred_element_type=F32)


def _dot_nt(a, b):
    return lax.dot_general(a, b, (((1,), (1,)), ((), ())), preferred_element_type=F32)


def _dot_f32(a, b):
    return jnp.dot(a, b, preferred_element_type=F32, precision=HIGHEST)


def _sigmoid(x):
    return 1.0 / (1.0 + jnp.exp(-x))


def _ada_kernel(c_ref, w_ref, b_ref, o_ref):
    c = c_ref[...]
    s = (c * _sigmoid(c)).astype(BF16)
    o_ref[...] = _dot(s, w_ref[...].astype(BF16)) + b_ref[...]


def _ada(cc, w_ada, b_ada):
    rows, d = cc.shape
    n = w_ada.shape[1]
    tn = 1024
    return pl.pallas_call(
        _ada_kernel,
        grid=(n // tn,),
        in_specs=[pl.BlockSpec((rows, d), lambda j: (0, 0)),
                  pl.BlockSpec((d, tn), lambda j: (0, j)),
                  pl.BlockSpec((1, tn), lambda j: (0, j))],
        out_specs=pl.BlockSpec((rows, tn), lambda j: (0, j)),
        out_shape=jax.ShapeDtypeStruct((rows, n), F32),
        compiler_params=_cparams("arbitrary"),
        name="ada",
    )(cc, w_ada, b_ada.reshape(1, n))


def _modulated_norm(x, nw, sc, sh):
    ms = jnp.mean(x * x, axis=-1, keepdims=True)
    return (x * lax.rsqrt(ms + EPS) * nw) * (1.0 + sc) + sh


def _inproj_kernel(x_ref, nw_ref, sc_ref, sh_ref, w_ref, wg_ref, o_ref, og_ref, h_scr):
    @pl.when(pl.program_id(1) == 0)
    def _():
        h = _modulated_norm(x_ref[...], nw_ref[...], sc_ref[...], sh_ref[...]).astype(BF16)
        h_scr[...] = h
        og_ref[...] = _dot(h, wg_ref[...])

    o_ref[...] = _dot(h_scr[...], w_ref[...])


def _inproj(x, nw, sc, sh, w_main, w_gates):
    b, t, d = x.shape
    n = w_main.shape[1]
    tm = min(t, 1024)
    tn = 1024
    tpb = t // tm
    return pl.pallas_call(
        _inproj_kernel,
        grid=(b * tpb, n // tn),
        in_specs=[pl.BlockSpec((None, tm, d), lambda i, j: (i // tpb, i % tpb, 0)),
                  pl.BlockSpec((1, d), lambda i, j: (0, 0)),
                  pl.BlockSpec((None, 1, d), lambda i, j: (i // tpb, 0, 0)),
                  pl.BlockSpec((None, 1, d), lambda i, j: (i // tpb, 0, 0)),
                  pl.BlockSpec((d, tn), lambda i, j: (0, j)),
                  pl.BlockSpec((d, LANES), lambda i, j: (0, 0))],
        out_specs=[pl.BlockSpec((None, tm, tn), lambda i, j: (i // tpb, i % tpb, j)),
                   pl.BlockSpec((None, tm, LANES), lambda i, j: (i // tpb, i % tpb, 0))],
        out_shape=[jax.ShapeDtypeStruct((b, t, n), F32),
                   jax.ShapeDtypeStruct((b, t, LANES), F32)],
        scratch_shapes=[pltpu.VMEM((tm, d), BF16)],
        compiler_params=_cparams("parallel", "arbitrary"),
        name="inproj",
    )(x, nw.reshape(1, d), sc, sh, w_main, w_gates)


def _chunk_masks(rev):
    r = lax.broadcasted_iota(jnp.int32, (CHUNK, CHUNK), 0)
    c = lax.broadcasted_iota(jnp.int32, (CHUNK, CHUNK), 1)
    incl = (c >= r) if rev else (c <= r)
    strict = (c > r) if rev else (c < r)
    return incl, strict


def _gated_head_norm(o, z, nw):
    ms = jnp.mean(o * o, axis=-1, keepdims=True)
    return (o * lax.rsqrt(ms + EPS) * nw) * (z * _sigmoid(z))


def _gla_chunk(zf, v, lb, rev, st, q):
    incl, _ = _chunk_masks(rev)
    one_m = 1.0 - lb
    logf = jnp.log(lb + one_m * _sigmoid(zf))
    k = one_m * _sigmoid(-zf)
    cum = _dot_f32(incl.astype(F32), logf)
    last = cum[0:1] if rev else cum[CHUNK - 1:CHUNK]
    kdec = k * jnp.exp(last - cum)
    vb = v.astype(BF16)
    u_t = _dot(v.T.astype(BF16), kdec.astype(BF16))
    st_new = st * jnp.exp(last) + u_t
    if q is None:
        return None, st_new
    mid = CHUNK // 2 - 1 if rev else CHUNK // 2
    ref = cum[mid:mid + 1]
    scores = _dot_nt((q * jnp.exp(cum - ref)).astype(BF16), (k * jnp.exp(ref - cum)).astype(BF16))
    scores = jnp.where(incl, scores, 0.0)
    o = _dot(scores.astype(BF16), vb) + _dot_nt((q * jnp.exp(cum)).astype(BF16), st.astype(BF16))
    return o, st_new


def _hgrn_out_kernel(q_ref, ff_ref, fb_ref, v_ref, g_ref, lbf_ref, lbb_ref, nw_ref, s0f_ref, s0b_ref,
                     o_ref, o_scr, sf_scr, sb_scr, *, n_chunks):
    sf_scr[...] = s0f_ref[...]
    sb_scr[...] = s0b_ref[...]
    lbf, lbb, nw = lbf_ref[...], lbb_ref[...], nw_ref[...]

    def body(n, carry, final):
        for rev in (False, True):
            idx = (n_chunks - 1 - n) if rev else n
            sl = pl.ds(pl.multiple_of(idx * CHUNK, CHUNK), CHUNK)
            f_ref, s_scr, lb = (fb_ref, sb_scr, lbb) if rev else (ff_ref, sf_scr, lbf)
            o, st = _gla_chunk(f_ref[sl, :], v_ref[sl, :], lb, rev, s_scr[...], q_ref[sl, :])
            s_scr[...] = st
            if final:
                o_ref[sl, :] = _gated_head_norm(o_scr[sl, :] + o, g_ref[sl, :], nw).astype(o_ref.dtype)
            else:
                o_scr[sl, :] = o
        return carry

    half = n_chunks // 2
    lax.fori_loop(0, half, functools.partial(body, final=False), 0)
    lax.fori_loop(half, n_chunks, functools.partial(body, final=True), 0)


def _hgrn_state_kernel(ff_ref, fb_ref, v_ref, lbf_ref, lbb_ref, sf_ref, sb_ref, *, n_chunks):
    sf_ref[...] = jnp.zeros_like(sf_ref)
    sb_ref[...] = jnp.zeros_like(sb_ref)
    lbf, lbb = lbf_ref[...], lbb_ref[...]

    def body(n, carry):
        for rev in (False, True):
            idx = (n_chunks - 1 - n) if rev else n
            sl = pl.ds(pl.multiple_of(idx * CHUNK, CHUNK), CHUNK)
            f_ref, s_ref, lb = (fb_ref, sb_ref, lbb) if rev else (ff_ref, sf_ref, lbf)
            _, st = _gla_chunk(f_ref[sl, :], v_ref[sl, :], lb, rev, s_ref[...], None)
            s_ref[...] = st
        return carry

    lax.fori_loop(0, n_chunks, body, 0)


def _col_spec(t, section):
    return pl.BlockSpec((None, t, HEAD_DIM), lambda b, h: (b, 0, section * N_HEADS + h))


def _head_row_spec(offset=0):
    return pl.BlockSpec((1, HEAD_DIM), lambda b, h: (0, offset + h))


_STATE_SPEC = pl.BlockSpec((None, None, HEAD_DIM, HEAD_DIM), lambda b, h: (b, h, 0, 0))


def _hgrn_states(pc, lbf, lbb):
    b, t, _ = pc.shape
    shp = jax.ShapeDtypeStruct((b, N_HEADS, HEAD_DIM, HEAD_DIM), F32)
    return pl.pallas_call(
        functools.partial(_hgrn_state_kernel, n_chunks=t // CHUNK),
        grid=(b, N_HEADS),
        in_specs=[_col_spec(t, 1), _col_spec(t, 2), _col_spec(t, 3), _head_row_spec(), _head_row_spec()],
        out_specs=[_STATE_SPEC, _STATE_SPEC],
        out_shape=[shp, shp],
        compiler_params=_cparams("parallel", "parallel"),
        name="hgrn_ctx",
    )(pc, pc, pc, lbf, lbb)


def _hgrn_out(px, lbf, lbb, nw, s0f, s0b):
    b, t, _ = px.shape
    return pl.pallas_call(
        functools.partial(_hgrn_out_kernel, n_chunks=t // CHUNK),
        grid=(b, N_HEADS),
        in_specs=[_col_spec(t, 0), _col_spec(t, 1), _col_spec(t, 2), _col_spec(t, 3), _col_spec(t, 4),
                  _head_row_spec(), _head_row_spec(),
                  pl.BlockSpec((1, HEAD_DIM), lambda b_, h: (0, 0)),
                  _STATE_SPEC, _STATE_SPEC],
        out_specs=pl.BlockSpec((None, t, HEAD_DIM), lambda b_, h: (b_, 0, h)),
        out_shape=jax.ShapeDtypeStruct((b, t, N_HEADS * HEAD_DIM), BF16),
        scratch_shapes=[pltpu.VMEM((t, HEAD_DIM), F32),
                        pltpu.VMEM((HEAD_DIM, HEAD_DIM), F32),
                        pltpu.VMEM((HEAD_DIM, HEAD_DIM), F32)],
        compiler_params=_cparams("parallel", "parallel"),
        name="hgrn_x",
    )(px, px, px, px, px, lbf, lbb, nw, s0f, s0b)


def _conv_silu(a, w, period):
    rows = a.shape[0]
    pos = lax.broadcasted_iota(jnp.int32, a.shape, 0) % period
    prev = jnp.where(pos == 0, 0.0, pltpu.roll(a, 1, axis=0))
    nxt = jnp.where(pos == period - 1, 0.0, pltpu.roll(a, rows - 1, axis=0))
    y = prev * w[0:1] + a * w[1:2] + nxt * w[2:3]
    return y * _sigmoid(y)


def _l2norm(x):
    return x * lax.rsqrt(jnp.sum(x * x, axis=-1, keepdims=True) + EPS)


def _softplus(x):
    return jnp.maximum(x, 0.0) + jnp.log1p(jnp.exp(-jnp.abs(x)))


def _unit_tri_inverse(a):
    r = lax.broadcasted_iota(jnp.int32, (CHUNK, CHUNK), 0)
    c = lax.broadcasted_iota(jnp.int32, (CHUNK, CHUNK), 1)
    p = -a
    inv = jnp.where(r == c, 1.0, 0.0) + p
    steps = CHUNK.bit_length() - 2
    for _ in range(steps):
        p = _dot_f32(p, p)
        inv = inv + _dot_f32(inv, p)
    return inv


def _gdn_intra(q, k, v, a_c, b_c, a_r, alog, dtb, rev):
    incl, strict = _chunk_masks(rev)
    scale = -jnp.exp(alog)
    g_c = scale * _softplus(a_c + dtb)
    g_r = scale * _softplus(a_r + dtb)
    beta = _sigmoid(b_c)
    cum_c = jnp.sum(jnp.where(incl, g_r, 0.0), axis=1, keepdims=True)
    incl_t = _chunk_masks(not rev)[0]
    cum_r = jnp.sum(jnp.where(incl_t, g_c, 0.0), axis=0, keepdims=True)
    dmask = jnp.exp(jnp.where(incl, cum_c - cum_r, -jnp.inf))
    kb = k * beta
    kbf = k.astype(BF16)
    tri = jnp.where(strict, _dot_nt(kb.astype(BF16), kbf) * dmask, 0.0)
    tinv = _unit_tri_inverse(tri).astype(BF16)
    ecum = jnp.exp(cum_c)
    u = _dot(tinv, (v * beta).astype(BF16))
    w = _dot(tinv, (kb * ecum).astype(BF16))
    last = cum_c[0:1] if rev else cum_c[CHUNK - 1:CHUNK]
    k_end = k * jnp.exp(last - cum_c)
    decay = jnp.exp(last)
    if q is None:
        return u, w, k_end, decay, None, None
    attn = _dot_nt(q.astype(BF16), kbf) * dmask
    return u, w, k_end, decay, q * ecum, attn


def _gdn_kernel(*refs, n_chunks, period, prep_rows, with_out):
    if with_out:
        (q_ref, k_ref, v_ref, z_ref, gc_ref, gr_ref, wq_ref, wk_ref, wv_ref, hp_ref, nw_ref,
         s0f_ref, s0b_ref, o_ref, qn, kn, vn, u_s, w_s, ke_s, qd_s, at_s, dc_s, o_scr, sf, sb) = refs
    else:
        (k_ref, v_ref, gc_ref, gr_ref, wk_ref, wv_ref, hp_ref,
         sf, sb, kn, vn, u_s, w_s, ke_s, dc_s) = refs
        q_ref = None

    t = n_chunks * CHUNK

    def prep(i, carry):
        sl = pl.ds(pl.multiple_of(i * prep_rows, prep_rows), prep_rows)
        kn[sl, :] = _l2norm(_conv_silu(k_ref[sl, :], wk_ref[...], period))
        vn[sl, :] = _conv_silu(v_ref[sl, :], wv_ref[...], period)
        if with_out:
            qn[sl, :] = _l2norm(_conv_silu(q_ref[sl, :], wq_ref[...], period)) * (HEAD_DIM ** -0.5)
        return carry

    lax.fori_loop(0, t // prep_rows, prep, 0)

    hp = hp_ref[...]

    def intra(n, carry):
        sl = pl.ds(pl.multiple_of(n * CHUNK, CHUNK), CHUNK)
        gc = gc_ref[sl, :]
        gr = gr_ref[n]
        q = qn[sl, :] if with_out else None
        for d, rev in enumerate((False, True)):
            u, w, ke, dec, qd, at = _gdn_intra(
                q, kn[sl, :], vn[sl, :], gc[:, d:d + 1], gc[:, 2 + d:3 + d], gr[d:d + 1, :],
                hp[2 * d:2 * d + 1, 0:1], hp[2 * d + 1:2 * d + 2, 0:1], rev)
            u_s[d, sl, :] = u
            w_s[d, sl, :] = w
            ke_s[d, sl, :] = ke
            dc_s[d, n] = jnp.broadcast_to(dec, (8, HEAD_DIM))
            if with_out:
                qd_s[d, sl, :] = qd
                at_s[d, sl, :] = at
        return carry

    lax.fori_loop(0, n_chunks, intra, 0)

    if with_out:
        sf[...] = s0f_ref[...]
        sb[...] = s0b_ref[...]
        nw = nw_ref[...]
    else:
        sf[...] = jnp.zeros_like(sf)
        sb[...] = jnp.zeros_like(sb)

    def scan(n, carry, final):
        for d, rev in enumerate((False, True)):
            idx = (n_chunks - 1 - n) if rev else n
            sl = pl.ds(pl.multiple_of(idx * CHUNK, CHUNK), CHUNK)
            s_ref = sb if rev else sf
            s = s_ref[...]
            sbf = s.astype(BF16)
            v_new = u_s[d, sl, :] - _dot(w_s[d, sl, :].astype(BF16), sbf)
            v_newb = v_new.astype(BF16)
            s_ref[...] = s * dc_s[d, idx][0:1, :] + _dot(ke_s[d, sl, :].T.astype(BF16), v_newb)
            if with_out:
                o = _dot(qd_s[d, sl, :].astype(BF16), sbf) + _dot(at_s[d, sl, :].astype(BF16), v_newb)
                if final:
                    o_ref[sl, :] = _gated_head_norm(o_scr[sl, :] + o, z_ref[sl, :], nw).astype(o_ref.dtype)
                else:
                    o_scr[sl, :] = o
        return carry

    if with_out:
        half = n_chunks // 2
        lax.fori_loop(0, half, functools.partial(scan, final=False), 0)
        lax.fori_loop(half, n_chunks, functools.partial(scan, final=True), 0)
    else:
        lax.fori_loop(0, n_chunks, functools.partial(scan, final=False), 0)


def _gdn_call(p, gates, conv_w, hp, nw, s0f, s0b, period, with_out, sec0):
    b, t, _ = p.shape
    n_chunks = t // CHUNK
    prep_rows = max(period, min(t, 256))
    assert prep_rows % period == 0 and t % prep_rows == 0 and n_chunks % 2 == 0
    g4 = gates[:, :, :4 * N_HEADS].reshape(b, t, 4, N_HEADS)
    g_col = g4.transpose(0, 3, 1, 2)
    g_row = jnp.pad(g4.transpose(0, 3, 2, 1), ((0, 0), (0, 0), (0, 4), (0, 0)))
    g_row = g_row.reshape(b, N_HEADS, 8, n_chunks, CHUNK).transpose(0, 1, 3, 2, 4)

    gc_spec = pl.BlockSpec((None, None, t, 4), lambda b_, h: (b_, h, 0, 0))
    gr_spec = pl.BlockSpec((None, None, n_chunks, 8, CHUNK), lambda b_, h: (b_, h, 0, 0, 0))
    conv_spec = lambda sec: pl.BlockSpec((3, HEAD_DIM), lambda b_, h: (0, sec * N_HEADS + h))
    hp_spec = pl.BlockSpec((None, 8, HEAD_DIM), lambda b_, h: (h, 0, 0))
    seq = lambda: pltpu.VMEM((t, HEAD_DIM), F32)
    seq2 = lambda: pltpu.VMEM((2, t, HEAD_DIM), F32)
    dc = pltpu.VMEM((2, n_chunks, 8, HEAD_DIM), F32)
    state = lambda: pltpu.VMEM((HEAD_DIM, HEAD_DIM), F32)
    kern = functools.partial(_gdn_kernel, n_chunks=n_chunks, period=period, prep_rows=prep_rows,
                             with_out=with_out)
    if with_out:
        return pl.pallas_call(
            kern,
            grid=(b, N_HEADS),
            in_specs=[_col_spec(t, sec0), _col_spec(t, sec0 + 1), _col_spec(t, sec0 + 2), _col_spec(t, sec0 + 3),
                      gc_spec, gr_spec, conv_spec(0), conv_spec(1), conv_spec(2), hp_spec,
                      pl.BlockSpec((1, HEAD_DIM), lambda b_, h: (0, 0)), _STATE_SPEC, _STATE_SPEC],
            out_specs=pl.BlockSpec((None, t, HEAD_DIM), lambda b_, h: (b_, 0, h)),
            out_shape=jax.ShapeDtypeStruct((b, t, N_HEADS * HEAD_DIM), BF16),
            scratch_shapes=[seq(), seq(), seq(), seq2(), seq2(), seq2(), seq2(),
                            pltpu.VMEM((2, t, CHUNK), F32), dc, seq(), state(), state()],
            compiler_params=_cparams("parallel", "parallel"),
            name="gdn_x",
        )(p, p, p, p, g_col, g_row, conv_w, conv_w, conv_w, hp, nw, s0f, s0b)
    shp = jax.ShapeDtypeStruct((b, N_HEADS, HEAD_DIM, HEAD_DIM), F32)
    return pl.pallas_call(
        kern,
        grid=(b, N_HEADS),
        in_specs=[_col_spec(t, sec0 + 1), _col_spec(t, sec0 + 2), gc_spec, gr_spec,
                  conv_spec(1), conv_spec(2), hp_spec],
        out_specs=[_STATE_SPEC, _STATE_SPEC],
        out_shape=[shp, shp],
        scratch_shapes=[seq(), seq(), seq2(), seq2(), seq2(), dc],
        compiler_params=_cparams("parallel", "parallel"),
        name="gdn_ctx",
    )(p, p, g_col, g_row, conv_w, conv_w, hp)


def _outproj_kernel(mh_ref, mg_ref, x_ref, gt_ref, sc_ref, sh_ref, nw_ref, woh_ref, wog_ref,
                    wrh_ref, wrl_ref, br_ref, x1_ref, h2_ref, idx_ref, gate_ref):
    y = _dot(mh_ref[...], woh_ref[...]) + _dot(mg_ref[...], wog_ref[...])
    x1 = x_ref[...] + gt_ref[...] * y
    x1_ref[...] = x1
    h = _modulated_norm(x1, nw_ref[...], sc_ref[...], sh_ref[...])
    hh = h.astype(BF16)
    h2_ref[...] = hh
    hl = (h - hh.astype(F32)).astype(BF16)
    wrh = wrh_ref[...]
    logits = _dot(hh, wrh) + _dot(hl, wrh) + _dot(hh, wrl_ref[...]) + br_ref[...]
    lane = lax.broadcasted_iota(jnp.int32, logits.shape, 1).astype(F32)
    vals, idxs = [], []
    for _ in range(TOP_K):
        m = jnp.max(logits, axis=-1, keepdims=True)
        i = jnp.min(jnp.where(logits == m, lane, float(LANES)), axis=-1, keepdims=True)
        vals.append(m)
        idxs.append(i)
        logits = jnp.where(lane == i, -jnp.inf, logits)
    es = [jnp.exp(v - vals[0]) for v in vals]
    inv = 1.0 / functools.reduce(lambda a, b_: a + b_, es)
    idx_out = jnp.zeros(lane.shape, F32)
    gate_out = jnp.zeros(lane.shape, F32)
    for k in range(TOP_K):
        idx_out = jnp.where(lane == k, idxs[k], idx_out)
        gate_out = jnp.where(lane == k, es[k] * inv, gate_out)
    idx_ref[...] = idx_out.astype(jnp.int32)
    gate_ref[...] = gate_out


def _outproj(mix_h, mix_g, x, gt1, sc2, sh2, nw, wo_h, wo_g, wr_hi, wr_lo, br):
    b, t, d = x.shape
    w = mix_h.shape[-1]
    tm = min(t, 512)
    tpb = t // tm
    row = lambda width: pl.BlockSpec((None, tm, width), lambda i: (i // tpb, i % tpb, 0))
    per_b = pl.BlockSpec((None, 1, d), lambda i: (i // tpb, 0, 0))
    const = lambda r, c: pl.BlockSpec((r, c), lambda i: (0, 0))
    return pl.pallas_call(
        _outproj_kernel,
        grid=(b * tpb,),
        in_specs=[row(w), row(w), row(d), per_b, per_b, per_b, const(1, d), const(w, d), const(w, d),
                  const(d, LANES), const(d, LANES), const(1, LANES)],
        out_specs=[row(d), row(d), row(LANES), row(LANES)],
        out_shape=[jax.ShapeDtypeStruct((b, t, d), F32), jax.ShapeDtypeStruct((b, t, d), BF16),
                   jax.ShapeDtypeStruct((b, t, LANES), jnp.int32), jax.ShapeDtypeStruct((b, t, LANES), F32)],
        compiler_params=_cparams("parallel"),
        name="outproj",
    )(mix_h, mix_g, x, gt1, sc2, sh2, nw.reshape(1, d), wo_h, wo_g, wr_hi, wr_lo, br)


def _moe_kernel(be_ref, nu_ref, x_ref, wg_ref, wu_ref, wd_ref, bg_ref, bu_ref, bd_ref, o_ref, *, n_f):
    i, j = pl.program_id(0), pl.program_id(1)

    @pl.when(i < nu_ref[0])
    def _():
        x = x_ref[...]
        gate = jnp.minimum(_dot(x, wg_ref[...]) + bg_ref[...], SWIGLU_LIMIT)
        up = jnp.clip(_dot(x, wu_ref[...]) + bu_ref[...], -SWIGLU_LIMIT, SWIGLU_LIMIT)
        act = (up + 1.0) * gate * _sigmoid(SWIGLU_ALPHA * gate)
        part = _dot(act.astype(BF16), wd_ref[...])

        @pl.when(j == 0)
        def _():
            o_ref[...] = part + bd_ref[...]

        @pl.when(j > 0)
        def _():
            o_ref[...] += part


def _moe_experts(xs, blk_exp, n_used, wg, wu, wd, bg, bu, bd, tm, tf):
    rows, d = xs.shape
    n_exp, _, f = wg.shape
    n_blocks, n_f = rows // tm, f // tf

    def blk(i, nu):
        return jnp.minimum(i, nu[0] - 1)

    def ftile(i, j, nu):
        return jnp.where(i < nu[0], j, n_f - 1)

    grid_spec = pltpu.PrefetchScalarGridSpec(
        num_scalar_prefetch=2,
        grid=(n_blocks, n_f),
        in_specs=[pl.BlockSpec((tm, d), lambda i, j, be, nu: (blk(i, nu), 0)),
                  pl.BlockSpec((None, d, tf), lambda i, j, be, nu: (be[blk(i, nu)], 0, ftile(i, j, nu))),
                  pl.BlockSpec((None, d, tf), lambda i, j, be, nu: (be[blk(i, nu)], 0, ftile(i, j, nu))),
                  pl.BlockSpec((None, tf, d), lambda i, j, be, nu: (be[blk(i, nu)], ftile(i, j, nu), 0)),
                  pl.BlockSpec((None, 1, tf), lambda i, j, be, nu: (be[blk(i, nu)], 0, ftile(i, j, nu))),
                  pl.BlockSpec((None, 1, tf), lambda i, j, be, nu: (be[blk(i, nu)], 0, ftile(i, j, nu))),
                  pl.BlockSpec((None, 1, d), lambda i, j, be, nu: (be[blk(i, nu)], 0, 0))],
        out_specs=pl.BlockSpec((tm, d), lambda i, j, be, nu: (blk(i, nu), 0)),
    )
    return pl.pallas_call(
        functools.partial(_moe_kernel, n_f=n_f),
        grid_spec=grid_spec,
        out_shape=jax.ShapeDtypeStruct((rows, d), F32),
        compiler_params=_cparams("arbitrary", "arbitrary"),
        name="moe_experts",
    )(blk_exp, n_used, xs, wg, wu, wd, bg.reshape(n_exp, 1, f), bu.reshape(n_exp, 1, f),
      bd.reshape(n_exp, 1, d))


def _route(top_idx, n_exp, tm):
    n_tok, k = top_idx.shape
    m = n_tok * k
    e_flat = top_idx.reshape(m)
    onehot = (e_flat[:, None] == jnp.arange(n_exp, dtype=jnp.int32)[None, :]).astype(jnp.int32)
    csum = jnp.cumsum(onehot, axis=0)
    counts = csum[-1]
    rank = jnp.take_along_axis(csum, e_flat[:, None], axis=1)[:, 0] - 1
    padded = (counts + tm - 1) // tm * tm
    pend = jnp.cumsum(padded)
    dest = (pend - padded)[e_flat] + rank
    n_blocks = -(-m // tm) + n_exp
    row_tok = jnp.zeros((n_blocks * tm,), jnp.int32).at[dest].set(jnp.arange(m, dtype=jnp.int32) // k)
    blk_exp = jnp.minimum(jnp.searchsorted(pend, jnp.arange(n_blocks, dtype=jnp.int32) * tm, side='right'),
                          n_exp - 1).astype(jnp.int32)
    n_used = (pend[-1:] // tm).astype(jnp.int32)
    return dest, row_tok, blk_exp, n_used


def _final_kernel(x1_ref, yg_ref, gate_ref, gt_ref, nw_ref, o_ref):
    g = gate_ref[...]
    moe = yg_ref[0] * g[:, 0:1]
    for k in range(1, TOP_K):
        moe = moe + yg_ref[k] * g[:, k:k + 1]
    x = x1_ref[...] + gt_ref[...] * moe
    ms = jnp.mean(x * x, axis=-1, keepdims=True)
    o_ref[...] = x * lax.rsqrt(ms + EPS) * nw_ref[...]


def _final(x1, yg, gates, gt2, nw):
    b, t, d = x1.shape
    tm = min(t, 256)
    tpb = t // tm
    row = lambda width: pl.BlockSpec((None, tm, width), lambda i: (i // tpb, i % tpb, 0))
    return pl.pallas_call(
        _final_kernel,
        grid=(b * tpb,),
        in_specs=[row(d),
                  pl.BlockSpec((TOP_K, None, tm, d), lambda i: (0, i // tpb, i % tpb, 0)),
                  row(LANES),
                  pl.BlockSpec((None, 1, d), lambda i: (i // tpb, 0, 0)),
                  pl.BlockSpec((1, d), lambda i: (0, 0))],
        out_specs=row(d),
        out_shape=jax.ShapeDtypeStruct((b, t, d), F32),
        compiler_params=_cparams("parallel"),
        name="final",
    )(x1, yg, gates, gt2, nw.reshape(1, d))


def kernel(x, c, ctx, c_ctx, w_ada, b_ada, norm_mix_w, w_in, hg_lb_f, hg_lb_b, hg_norm_w, gd_conv_w,
           gd_a_log_f, gd_a_log_b, gd_dt_bias_f, gd_dt_bias_b, gd_norm_w, w_out, norm_ffn_w, w_router,
           b_router, w_gate, b_gate, w_up, b_up, w_down, b_down, norm_out_w):
    bsz, t, d = x.shape
    t_ctx = ctx.shape[1]
    n_exp = w_router.shape[-1]
    l = 0
    hg_w = N_HEADS * HEAD_DIM
    n_main = 9 * hg_w

    lb_f = jnp.cumsum(jax.nn.softmax(hg_lb_f.astype(F32), axis=0), axis=0)[l].reshape(1, hg_w)
    lb_b = jnp.cumsum(jax.nn.softmax(hg_lb_b.astype(F32), axis=0), axis=0)[l].reshape(1, hg_w)

    rows = -(-(bsz + 1) // 8) * 8
    cc = jnp.zeros((rows, d), F32).at[:bsz].set(c).at[bsz].set(c_ctx)
    mod = _ada(cc, w_ada[l], b_ada[l])
    sh1, sc1, gt1, sh2, sc2, gt2 = (mod[:bsz, i * d:(i + 1) * d].reshape(bsz, 1, d) for i in range(6))
    csh1 = jnp.broadcast_to(mod[bsz, 0:d].reshape(1, 1, d), (bsz, 1, d))
    csc1 = jnp.broadcast_to(mod[bsz, d:2 * d].reshape(1, 1, d), (bsz, 1, d))

    w_in_b = w_in[l].astype(BF16)
    w_main = w_in_b[:, :n_main]
    w_gates = jnp.pad(w_in_b[:, n_main:], ((0, 0), (0, LANES - (w_in_b.shape[1] - n_main))))
    px, gx = _inproj(x, norm_mix_w[l], sc1, sh1, w_main, w_gates)
    pc, gc = _inproj(ctx, norm_mix_w[l], csc1, csh1, w_main, w_gates)

    hg_nw = hg_norm_w[l].reshape(1, HEAD_DIM)
    hs_f, hs_b = _hgrn_states(pc, lb_f, lb_b)
    mix_h = _hgrn_out(px, lb_f, lb_b, hg_nw, hs_f, hs_b)

    hp = jnp.stack([gd_a_log_f[l], gd_dt_bias_f[l], gd_a_log_b[l], gd_dt_bias_b[l]], axis=1)
    hp = jnp.broadcast_to(jnp.pad(hp, ((0, 0), (0, 4)))[:, :, None], (N_HEADS, 8, HEAD_DIM)).astype(F32)
    gd_nw = gd_norm_w[l].reshape(1, HEAD_DIM)
    gs_f, gs_b = _gdn_call(pc, gc, gd_conv_w[l], hp, gd_nw, None, None, t_ctx, False, 5)
    mix_g = _gdn_call(px, gx, gd_conv_w[l], hp, gd_nw, gs_f, gs_b, GRID_W, True, 5)

    w_out_b = w_out[l].astype(BF16)
    wr = jnp.pad(w_router[l], ((0, 0), (0, LANES - n_exp)))
    wr_hi = wr.astype(BF16)
    wr_lo = (wr - wr_hi.astype(F32)).astype(BF16)
    br = jnp.full((1, LANES), NEG_BIG, F32).at[0, :n_exp].set(b_router[l])
    x1, h2, idx_pad, gate_pad = _outproj(mix_h, mix_g, x, gt1, sc2, sh2, norm_ffn_w[l],
                                         w_out_b[:hg_w], w_out_b[hg_w:], wr_hi, wr_lo, br)

    n_tok = bsz * t
    top_idx = idx_pad.reshape(n_tok, LANES)[:, :TOP_K]
    dest, row_tok, blk_exp, n_used = _route(top_idx, n_exp, MOE_TM)
    xs = h2.reshape(n_tok, d)[row_tok]
    ys = _moe_experts(xs, blk_exp, n_used, w_gate[l].astype(BF16), w_up[l].astype(BF16),
                      w_down[l].astype(BF16), b_gate[l], b_up[l], b_down[l], MOE_TM, MOE_TF)

    yg = ys[dest.reshape(n_tok, TOP_K).T].reshape(TOP_K, bsz, t, d)
    return _final(x1, yg, gate_pad, gt2, norm_out_w)
```

```python
import functools

import jax
import jax.numpy as jnp
from jax import lax
from jax.experimental import pallas as pl
from jax.experimental.pallas import tpu as pltpu

F32 = jnp.float32
BF16 = jnp.bfloat16

EPS = 1e-6
CHUNK = 64
HEAD_DIM = 128
N_HEADS = 8
GRID_W = 64
TOP_K = 4
SWIGLU_LIMIT = 7.0
SWIGLU_ALPHA = 1.702
LANES = 128
NEG_BIG = -1e30

HGRN_GROUP = 4
GDN_GROUP = 4
MOE_TM = 512
MOE_TF = 1024
VMEM_LIMIT = 56 * 1024 * 1024


def _cparams(*sem):
    return pltpu.CompilerParams(dimension_semantics=sem, vmem_limit_bytes=VMEM_LIMIT)


def _dot(a, b):
    return jnp.dot(a, b, preferred_element_type=F32)


def _dot_nt(a, b):
    return lax.dot_general(a, b, (((1,), (1,)), ((), ())), preferred_element_type=F32)


def _split2(x):
    hi = x.astype(BF16)
    return hi, (x - hi.astype(F32)).astype(BF16)


def _split3(x):
    hi = x.astype(BF16)
    rest = x - hi.astype(F32)
    mid = rest.astype(BF16)
    return hi, mid, (rest - mid.astype(F32)).astype(BF16)


def _dot_split(a, b):
    ah, al = a
    bh, bl = b
    return _dot(ah, bh) + (_dot(ah, bl) + _dot(al, bh))


def _sigmoid(x):
    return 1.0 / (1.0 + jnp.exp(-x))


def _ada_kernel(c_ref, w_ref, b_ref, o_ref):
    c = c_ref[...]
    s = (c * _sigmoid(c)).astype(BF16)
    o_ref[...] = _dot(s, w_ref[...].astype(BF16)) + b_ref[...]


def _ada(cc, w_ada, b_ada):
    rows, d = cc.shape
    n = w_ada.shape[1]
    tn = 1024
    return pl.pallas_call(
        _ada_kernel,
        grid=(n // tn,),
        in_specs=[pl.BlockSpec((rows, d), lambda j: (0, 0)),
                  pl.BlockSpec((d, tn), lambda j: (0, j)),
                  pl.BlockSpec((1, tn), lambda j: (0, j))],
        out_specs=pl.BlockSpec((rows, tn), lambda j: (0, j)),
        out_shape=jax.ShapeDtypeStruct((rows, n), F32),
        compiler_params=_cparams("arbitrary"),
        name="ada",
    )(cc, w_ada, b_ada.reshape(1, n))


def _modulated_norm(x, nw, sc, sh):
    ms = jnp.mean(x * x, axis=-1, keepdims=True)
    return (x * lax.rsqrt(ms + EPS) * nw) * (1.0 + sc) + sh


def _inproj_kernel(x_ref, nw_ref, sc_ref, sh_ref, w_ref, wg_ref, o_ref, og_ref, h_scr):
    @pl.when(pl.program_id(1) == 0)
    def _():
        h = _modulated_norm(x_ref[...], nw_ref[...], sc_ref[...], sh_ref[...]).astype(BF16)
        h_scr[...] = h
        og_ref[...] = _dot(h, wg_ref[...])

    o_ref[...] = _dot(h_scr[...], w_ref[...])


def _inproj(x, nw, sc, sh, w_main, w_gates):
    b, t, d = x.shape
    n = w_main.shape[1]
    tm = min(t, 1024)
    tn = 1024
    tpb = t // tm
    return pl.pallas_call(
        _inproj_kernel,
        grid=(b * tpb, n // tn),
        in_specs=[pl.BlockSpec((None, tm, d), lambda i, j: (i // tpb, i % tpb, 0)),
                  pl.BlockSpec((1, d), lambda i, j: (0, 0)),
                  pl.BlockSpec((None, 1, d), lambda i, j: (i // tpb, 0, 0)),
                  pl.BlockSpec((None, 1, d), lambda i, j: (i // tpb, 0, 0)),
                  pl.BlockSpec((d, tn), lambda i, j: (0, j)),
                  pl.BlockSpec((d, LANES), lambda i, j: (0, 0))],
        out_specs=[pl.BlockSpec((None, tm, tn), lambda i, j: (i // tpb, i % tpb, j)),
                   pl.BlockSpec((None, tm, LANES), lambda i, j: (i // tpb, i % tpb, 0))],
        out_shape=[jax.ShapeDtypeStruct((b, t, n), F32),
                   jax.ShapeDtypeStruct((b, t, LANES), F32)],
        scratch_shapes=[pltpu.VMEM((tm, d), BF16)],
        compiler_params=_cparams("parallel", "arbitrary"),
        name="inproj",
    )(x, nw.reshape(1, d), sc, sh, w_main, w_gates)


def _chunk_masks(rev):
    r = lax.broadcasted_iota(jnp.int32, (CHUNK, CHUNK), 0)
    c = lax.broadcasted_iota(jnp.int32, (CHUNK, CHUNK), 1)
    incl = (c >= r) if rev else (c <= r)
    strict = (c > r) if rev else (c < r)
    return incl, strict


def _gated_head_norm(o, z, nw):
    ms = jnp.mean(o * o, axis=-1, keepdims=True)
    return (o * lax.rsqrt(ms + EPS) * nw) * (z * _sigmoid(z))


def _group_slices(g, group, n_chunks):
    idxs, revs = [], []
    for i in range(group):
        n = g * group + i
        idxs += [n, n_chunks - 1 - n]
        revs += [False, True]
    return [pl.ds(pl.multiple_of(ix * CHUNK, CHUNK), CHUNK) for ix in idxs], idxs, revs


def _scan_group(n_chunks, want, two_phase):
    g = want
    while g > 1 and (n_chunks % g or (two_phase and (n_chunks // g) % 2)):
        g //= 2
    assert n_chunks % g == 0 and not (two_phase and (n_chunks // g) % 2)
    return g


def _gla_group(zfs, vs, qs, lbs, revs):
    n = len(zfs)
    incl = [_chunk_masks(r)[0] for r in revs]
    logf, k = [], []
    for i in range(n):
        one_m = 1.0 - lbs[i]
        logf.append(jnp.log(lbs[i] + one_m * _sigmoid(zfs[i])))
        k.append(one_m * _sigmoid(-zfs[i]))
    cum = []
    for i in range(n):
        tri = incl[i].astype(BF16)
        hi, mid, lo = _split3(logf[i])
        cum.append(_dot(tri, hi) + (_dot(tri, mid) + _dot(tri, lo)))
    last = [cum[i][0:1] if revs[i] else cum[i][CHUNK - 1:CHUNK] for i in range(n)]
    u_t = [_dot(vs[i].T.astype(BF16), (k[i] * jnp.exp(last[i] - cum[i])).astype(BF16)) for i in range(n)]
    dec = [jnp.exp(x) for x in last]
    if qs is None:
        return u_t, dec, None, None
    scores = []
    for i in range(n):
        mid_row = CHUNK // 2 - 1 if revs[i] else CHUNK // 2
        ref = cum[i][mid_row:mid_row + 1]
        sc = _dot_nt((qs[i] * jnp.exp(cum[i] - ref)).astype(BF16), (k[i] * jnp.exp(ref - cum[i])).astype(BF16))
        scores.append(jnp.where(incl[i], sc, 0.0).astype(BF16))
    o_intra = [_dot(scores[i], vs[i].astype(BF16)) for i in range(n)]
    qe = [(qs[i] * jnp.exp(cum[i])).astype(BF16) for i in range(n)]
    return u_t, dec, o_intra, qe


def _hgrn_out_kernel(q_ref, ff_ref, fb_ref, v_ref, g_ref, lbf_ref, lbb_ref, nw_ref, s0f_ref, s0b_ref,
                     o_ref, o_scr, sf_scr, sb_scr, *, n_chunks, group):
    sf_scr[...] = s0f_ref[...]
    sb_scr[...] = s0b_ref[...]
    lbf, lbb, nw = lbf_ref[...], lbb_ref[...], nw_ref[...]

    def body(g, carry, final):
        sls, _, revs = _group_slices(g, group, n_chunks)
        zfs = [(fb_ref if r else ff_ref)[sl, :] for sl, r in zip(sls, revs)]
        vs = [v_ref[sl, :] for sl in sls]
        qs = [q_ref[sl, :] for sl in sls]
        u_t, dec, o_intra, qe = _gla_group(zfs, vs, qs, [lbb if r else lbf for r in revs], revs)
        for i, (sl, rev) in enumerate(zip(sls, revs)):
            s_scr = sb_scr if rev else sf_scr
            st = s_scr[...]
            o = o_intra[i] + _dot_nt(qe[i], st.astype(BF16))
            s_scr[...] = st * dec[i] + u_t[i]
            if final:
                o_ref[sl, :] = _gated_head_norm(o_scr[sl, :] + o, g_ref[sl, :], nw).astype(o_ref.dtype)
            else:
                o_scr[sl, :] = o
        return carry

    trips = n_chunks // group
    lax.fori_loop(0, trips // 2, functools.partial(body, final=False), 0)
    lax.fori_loop(trips // 2, trips, functools.partial(body, final=True), 0)


def _hgrn_state_kernel(ff_ref, fb_ref, v_ref, lbf_ref, lbb_ref, sf_ref, sb_ref, *, n_chunks, group):
    sf_ref[...] = jnp.zeros_like(sf_ref)
    sb_ref[...] = jnp.zeros_like(sb_ref)
    lbf, lbb = lbf_ref[...], lbb_ref[...]

    def body(g, carry):
        sls, _, revs = _group_slices(g, group, n_chunks)
        zfs = [(fb_ref if r else ff_ref)[sl, :] for sl, r in zip(sls, revs)]
        vs = [v_ref[sl, :] for sl in sls]
        u_t, dec, _, _ = _gla_group(zfs, vs, None, [lbb if r else lbf for r in revs], revs)
        for i, rev in enumerate(revs):
            s_ref = sb_ref if rev else sf_ref
            s_ref[...] = s_ref[...] * dec[i] + u_t[i]
        return carry

    lax.fori_loop(0, n_chunks // group, body, 0)


def _col_spec(t, section):
    return pl.BlockSpec((None, t, HEAD_DIM), lambda b, h: (b, 0, section * N_HEADS + h))


def _head_row_spec(offset=0):
    return pl.BlockSpec((1, HEAD_DIM), lambda b, h: (0, offset + h))


_STATE_SPEC = pl.BlockSpec((None, None, HEAD_DIM, HEAD_DIM), lambda b, h: (b, h, 0, 0))


def _hgrn_states(pc, lbf, lbb):
    b, t, _ = pc.shape
    n_chunks = t // CHUNK
    shp = jax.ShapeDtypeStruct((b, N_HEADS, HEAD_DIM, HEAD_DIM), F32)
    return pl.pallas_call(
        functools.partial(_hgrn_state_kernel, n_chunks=n_chunks, group=_scan_group(n_chunks, HGRN_GROUP, False)),
        grid=(b, N_HEADS),
        in_specs=[_col_spec(t, 1), _col_spec(t, 2), _col_spec(t, 3), _head_row_spec(), _head_row_spec()],
        out_specs=[_STATE_SPEC, _STATE_SPEC],
        out_shape=[shp, shp],
        compiler_params=_cparams("parallel", "parallel"),
        name="hgrn_ctx",
    )(pc, pc, pc, lbf, lbb)


def _hgrn_out(px, lbf, lbb, nw, s0f, s0b):
    b, t, _ = px.shape
    n_chunks = t // CHUNK
    return pl.pallas_call(
        functools.partial(_hgrn_out_kernel, n_chunks=n_chunks, group=_scan_group(n_chunks, HGRN_GROUP, True)),
        grid=(b, N_HEADS),
        in_specs=[_col_spec(t, 0), _col_spec(t, 1), _col_spec(t, 2), _col_spec(t, 3), _col_spec(t, 4),
                  _head_row_spec(), _head_row_spec(),
                  pl.BlockSpec((1, HEAD_DIM), lambda b_, h: (0, 0)),
                  _STATE_SPEC, _STATE_SPEC],
        out_specs=pl.BlockSpec((None, t, HEAD_DIM), lambda b_, h: (b_, 0, h)),
        out_shape=jax.ShapeDtypeStruct((b, t, N_HEADS * HEAD_DIM), BF16),
        scratch_shapes=[pltpu.VMEM((t, HEAD_DIM), F32),
                        pltpu.VMEM((HEAD_DIM, HEAD_DIM), F32),
                        pltpu.VMEM((HEAD_DIM, HEAD_DIM), F32)],
        compiler_params=_cparams("parallel", "parallel"),
        name="hgrn_x",
    )(px, px, px, px, px, lbf, lbb, nw, s0f, s0b)


def _conv_silu(a, w, period):
    rows = a.shape[0]
    pos = lax.broadcasted_iota(jnp.int32, a.shape, 0) % period
    prev = jnp.where(pos == 0, 0.0, pltpu.roll(a, 1, axis=0))
    nxt = jnp.where(pos == period - 1, 0.0, pltpu.roll(a, rows - 1, axis=0))
    y = prev * w[0:1] + a * w[1:2] + nxt * w[2:3]
    return y * _sigmoid(y)


def _l2norm(x):
    return x * lax.rsqrt(jnp.sum(x * x, axis=-1, keepdims=True) + EPS)


def _softplus(x):
    return jnp.maximum(x, 0.0) + jnp.log1p(jnp.exp(-jnp.abs(x)))


def _unit_tri_inverses(tris):
    r = lax.broadcasted_iota(jnp.int32, (CHUNK, CHUNK), 0)
    c = lax.broadcasted_iota(jnp.int32, (CHUNK, CHUNK), 1)
    eye = jnp.where(r == c, 1.0, 0.0)
    ps = [_split2(-a) for a in tris]
    invs = [eye - a for a in tris]
    for _ in range(CHUNK.bit_length() - 2):
        ps = [_split2(_dot_split(p, p)) for p in ps]
        invs = [inv + _dot_split(_split2(inv), p) for inv, p in zip(invs, ps)]
    return invs


def _gdn_group(qs, ks, vs, a_cs, b_cs, a_rs, alogs, dtbs, revs):
    n = len(ks)
    dmask, ecum, beta, eend, dec, strict = [], [], [], [], [], []
    for i in range(n):
        incl, st = _chunk_masks(revs[i])
        incl_t = _chunk_masks(not revs[i])[0]
        scale = -jnp.exp(alogs[i])
        g_c = scale * _softplus(a_cs[i] + dtbs[i])
        g_r = scale * _softplus(a_rs[i] + dtbs[i])
        cum_c = jnp.sum(jnp.where(incl, g_r, 0.0), axis=1, keepdims=True)
        cum_r = jnp.sum(jnp.where(incl_t, g_c, 0.0), axis=0, keepdims=True)
        dmask.append(jnp.exp(jnp.where(incl, cum_c - cum_r, -jnp.inf)))
        ecum.append(jnp.exp(cum_c))
        beta.append(_sigmoid(b_cs[i]))
        last = cum_c[0:1] if revs[i] else cum_c[CHUNK - 1:CHUNK]
        eend.append(jnp.exp(last - cum_c))
        dec.append(jnp.exp(last))
        strict.append(st)
    kbf = [k.astype(BF16) for k in ks]
    kb = [ks[i] * beta[i] for i in range(n)]
    tris = [jnp.where(strict[i], _dot_nt(kb[i].astype(BF16), kbf[i]) * dmask[i], 0.0) for i in range(n)]
    tinv = [t.astype(BF16) for t in _unit_tri_inverses(tris)]
    u = [_dot(tinv[i], (vs[i] * beta[i]).astype(BF16)).astype(BF16) for i in range(n)]
    w = [_dot(tinv[i], (kb[i] * ecum[i]).astype(BF16)).astype(BF16) for i in range(n)]
    ke_t = [(ks[i] * eend[i]).T.astype(BF16) for i in range(n)]
    mp = [_dot(ke_t[i], w[i]) for i in range(n)]
    cc = [_dot(ke_t[i], u[i]) for i in range(n)]
    if qs is None:
        return mp, cc, dec, None, None
    attn = [(_dot_nt(qs[i].astype(BF16), kbf[i]) * dmask[i]).astype(BF16) for i in range(n)]
    qp = [qs[i] * ecum[i] - _dot(attn[i], w[i]) for i in range(n)]
    oi = [_dot(attn[i], u[i]) for i in range(n)]
    return mp, cc, dec, qp, oi


def _gdn_kernel(*refs, n_chunks, period, prep_rows, group, with_out):
    if with_out:
        (q_ref, k_ref, v_ref, z_ref, gc_ref, gr_ref, wq_ref, wk_ref, wv_ref, hp_ref, nw_ref,
         s0f_ref, s0b_ref, o_ref, qn, kn, vn, mp_s, cc_s, dc_s, qp_s, o_scr, sf, sb) = refs
    else:
        (k_ref, v_ref, gc_ref, gr_ref, wk_ref, wv_ref, hp_ref,
         sf, sb, kn, vn, mp_s, cc_s, dc_s) = refs
        q_ref = None

    t = n_chunks * CHUNK

    def prep(i, carry):
        sl = pl.ds(pl.multiple_of(i * prep_rows, prep_rows), prep_rows)
        kn[sl, :] = _l2norm(_conv_silu(k_ref[sl, :], wk_ref[...], period))
        vn[sl, :] = _conv_silu(v_ref[sl, :], wv_ref[...], period)
        if with_out:
            qn[sl, :] = _l2norm(_conv_silu(q_ref[sl, :], wq_ref[...], period)) * (HEAD_DIM ** -0.5)
        return carry

    lax.fori_loop(0, t // prep_rows, prep, 0)

    hp = hp_ref[...]

    def intra(g, carry):
        qs, ks, vs, a_cs, b_cs, a_rs, alogs, dtbs, revs, where = [], [], [], [], [], [], [], [], [], []
        for i in range(group):
            n = g * group + i
            sl = pl.ds(pl.multiple_of(n * CHUNK, CHUNK), CHUNK)
            gc = gc_ref[sl, :]
            gr = gr_ref[n]
            for d, rev in enumerate((False, True)):
                if with_out:
                    qs.append(qn[sl, :])
                ks.append(kn[sl, :])
                vs.append(vn[sl, :])
                a_cs.append(gc[:, d:d + 1])
                b_cs.append(gc[:, 2 + d:3 + d])
                a_rs.append(gr[d:d + 1, :])
                alogs.append(hp[2 * d:2 * d + 1, 0:1])
                dtbs.append(hp[2 * d + 1:2 * d + 2, 0:1])
                revs.append(rev)
                where.append((d, n, sl))
        mp, cc, dec, qp, oi = _gdn_group(qs if with_out else None, ks, vs, a_cs, b_cs, a_rs, alogs, dtbs, revs)
        for i, (d, n, sl) in enumerate(where):
            mp_s[d, n] = mp[i].astype(BF16)
            cc_s[d, n] = cc[i]
            dc_s[d, n] = jnp.broadcast_to(dec[i], (8, HEAD_DIM))
            if with_out:
                qp_s[d, sl, :] = qp[i].astype(BF16)
                if d == 1:
                    o_scr[sl, :] = oi[i - 1] + oi[i]
        return carry

    lax.fori_loop(0, n_chunks // group, intra, 0)

    if with_out:
        sf[...] = s0f_ref[...]
        sb[...] = s0b_ref[...]
        nw = nw_ref[...]
    else:
        sf[...] = jnp.zeros_like(sf)
        sb[...] = jnp.zeros_like(sb)

    def scan(g, carry, final):
        sls, idxs, revs = _group_slices(g, group, n_chunks)
        for sl, idx, rev in zip(sls, idxs, revs):
            d = int(rev)
            s_ref = sb if rev else sf
            s = s_ref[...]
            sbf = s.astype(BF16)
            s_ref[...] = s * dc_s[d, idx][0:1, :] - _dot(mp_s[d, idx], sbf) + cc_s[d, idx]
            if with_out:
                o = o_scr[sl, :] + _dot(qp_s[d, sl, :], sbf)
                if final:
                    o_ref[sl, :] = _gated_head_norm(o, z_ref[sl, :], nw).astype(o_ref.dtype)
                else:
                    o_scr[sl, :] = o
        return carry

    trips = n_chunks // group
    if with_out:
        lax.fori_loop(0, trips // 2, functools.partial(scan, final=False), 0)
        lax.fori_loop(trips // 2, trips, functools.partial(scan, final=True), 0)
    else:
        lax.fori_loop(0, trips, functools.partial(scan, final=False), 0)


def _gdn_call(p, gates, conv_w, hp, nw, s0f, s0b, period, with_out, sec0):
    b, t, _ = p.shape
    n_chunks = t // CHUNK
    prep_rows = max(period, min(t, 256))
    assert prep_rows % period == 0 and t % prep_rows == 0
    group = _scan_group(n_chunks, GDN_GROUP, with_out)
    g4 = gates[:, :, :4 * N_HEADS].reshape(b, t, 4, N_HEADS)
    g_col = g4.transpose(0, 3, 1, 2)
    g_row = jnp.pad(g4.transpose(0, 3, 2, 1), ((0, 0), (0, 0), (0, 4), (0, 0)))
    g_row = g_row.reshape(b, N_HEADS, 8, n_chunks, CHUNK).transpose(0, 1, 3, 2, 4)

    gc_spec = pl.BlockSpec((None, None, t, 4), lambda b_, h: (b_, h, 0, 0))
    gr_spec = pl.BlockSpec((None, None, n_chunks, 8, CHUNK), lambda b_, h: (b_, h, 0, 0, 0))
    conv_spec = lambda sec: pl.BlockSpec((3, HEAD_DIM), lambda b_, h: (0, sec * N_HEADS + h))
    hp_spec = pl.BlockSpec((None, 8, HEAD_DIM), lambda b_, h: (h, 0, 0))
    seq = lambda: pltpu.VMEM((t, HEAD_DIM), F32)
    mats = lambda dt: pltpu.VMEM((2, n_chunks, HEAD_DIM, HEAD_DIM), dt)
    dc = pltpu.VMEM((2, n_chunks, 8, HEAD_DIM), F32)
    state = lambda: pltpu.VMEM((HEAD_DIM, HEAD_DIM), F32)
    kern = functools.partial(_gdn_kernel, n_chunks=n_chunks, period=period, prep_rows=prep_rows,
                             group=group, with_out=with_out)
    if with_out:
        return pl.pallas_call(
            kern,
            grid=(b, N_HEADS),
            in_specs=[_col_spec(t, sec0), _col_spec(t, sec0 + 1), _col_spec(t, sec0 + 2), _col_spec(t, sec0 + 3),
                      gc_spec, gr_spec, conv_spec(0), conv_spec(1), conv_spec(2), hp_spec,
                      pl.BlockSpec((1, HEAD_DIM), lambda b_, h: (0, 0)), _STATE_SPEC, _STATE_SPEC],
            out_specs=pl.BlockSpec((None, t, HEAD_DIM), lambda b_, h: (b_, 0, h)),
            out_shape=jax.ShapeDtypeStruct((b, t, N_HEADS * HEAD_DIM), BF16),
            scratch_shapes=[seq(), seq(), seq(), mats(BF16), mats(F32), dc,
                            pltpu.VMEM((2, t, HEAD_DIM), BF16), seq(), state(), state()],
            compiler_params=_cparams("parallel", "parallel"),
            name="gdn_x",
        )(p, p, p, p, g_col, g_row, conv_w, conv_w, conv_w, hp, nw, s0f, s0b)
    shp = jax.ShapeDtypeStruct((b, N_HEADS, HEAD_DIM, HEAD_DIM), F32)
    return pl.pallas_call(
        kern,
        grid=(b, N_HEADS),
        in_specs=[_col_spec(t, sec0 + 1), _col_spec(t, sec0 + 2), gc_spec, gr_spec,
                  conv_spec(1), conv_spec(2), hp_spec],
        out_specs=[_STATE_SPEC, _STATE_SPEC],
        out_shape=[shp, shp],
        scratch_shapes=[seq(), seq(), mats(BF16), mats(F32), dc],
        compiler_params=_cparams("parallel", "parallel"),
        name="gdn_ctx",
    )(p, p, g_col, g_row, conv_w, conv_w, hp)


def _outproj_kernel(mh_ref, mg_ref, x_ref, gt_ref, sc_ref, sh_ref, nw_ref, woh_ref, wog_ref,
                    wrh_ref, wrl_ref, br_ref, x1_ref, h2_ref, idx_ref, gate_ref):
    y = _dot(mh_ref[...], woh_ref[...]) + _dot(mg_ref[...], wog_ref[...])
    x1 = x_ref[...] + gt_ref[...] * y
    x1_ref[...] = x1
    h = _modulated_norm(x1, nw_ref[...], sc_ref[...], sh_ref[...])
    hh = h.astype(BF16)
    h2_ref[...] = hh
    hl = (h - hh.astype(F32)).astype(BF16)
    wrh = wrh_ref[...]
    logits = _dot(hh, wrh) + _dot(hl, wrh) + _dot(hh, wrl_ref[...]) + br_ref[...]
    lane = lax.broadcasted_iota(jnp.int32, logits.shape, 1).astype(F32)
    vals, idxs = [], []
    for _ in range(TOP_K):
        m = jnp.max(logits, axis=-1, keepdims=True)
        i = jnp.min(jnp.where(logits == m, lane, float(LANES)), axis=-1, keepdims=True)
        vals.append(m)
        idxs.append(i)
        logits = jnp.where(lane == i, -jnp.inf, logits)
    es = [jnp.exp(v - vals[0]) for v in vals]
    inv = 1.0 / functools.reduce(lambda a, b_: a + b_, es)
    idx_out = jnp.zeros(lane.shape, F32)
    gate_out = jnp.zeros(lane.shape, F32)
    for k in range(TOP_K):
        idx_out = jnp.where(lane == k, idxs[k], idx_out)
        gate_out = jnp.where(lane == k, es[k] * inv, gate_out)
    idx_ref[...] = idx_out.astype(jnp.int32)
    gate_ref[...] = gate_out


def _outproj(mix_h, mix_g, x, gt1, sc2, sh2, nw, wo_h, wo_g, wr_hi, wr_lo, br):
    b, t, d = x.shape
    w = mix_h.shape[-1]
    tm = min(t, 512)
    tpb = t // tm
    row = lambda width: pl.BlockSpec((None, tm, width), lambda i: (i // tpb, i % tpb, 0))
    per_b = pl.BlockSpec((None, 1, d), lambda i: (i // tpb, 0, 0))
    const = lambda r, c: pl.BlockSpec((r, c), lambda i: (0, 0))
    return pl.pallas_call(
        _outproj_kernel,
        grid=(b * tpb,),
        in_specs=[row(w), row(w), row(d), per_b, per_b, per_b, const(1, d), const(w, d), const(w, d),
                  const(d, LANES), const(d, LANES), const(1, LANES)],
        out_specs=[row(d), row(d), row(LANES), row(LANES)],
        out_shape=[jax.ShapeDtypeStruct((b, t, d), F32), jax.ShapeDtypeStruct((b, t, d), BF16),
                   jax.ShapeDtypeStruct((b, t, LANES), jnp.int32), jax.ShapeDtypeStruct((b, t, LANES), F32)],
        compiler_params=_cparams("parallel"),
        name="outproj",
    )(mix_h, mix_g, x, gt1, sc2, sh2, nw.reshape(1, d), wo_h, wo_g, wr_hi, wr_lo, br)


def _moe_kernel(be_ref, nu_ref, x_ref, wg_ref, wu_ref, wd_ref, bg_ref, bu_ref, bd_ref, o_ref, *, n_f):
    i, j = pl.program_id(0), pl.program_id(1)

    @pl.when(i < nu_ref[0])
    def _():
        x = x_ref[...]
        gate = jnp.minimum(_dot(x, wg_ref[...]) + bg_ref[...], SWIGLU_LIMIT)
        up = jnp.clip(_dot(x, wu_ref[...]) + bu_ref[...], -SWIGLU_LIMIT, SWIGLU_LIMIT)
        act = (up + 1.0) * gate * _sigmoid(SWIGLU_ALPHA * gate)
        part = _dot(act.astype(BF16), wd_ref[...])

        @pl.when(j == 0)
        def _():
            o_ref[...] = part + bd_ref[...]

        @pl.when(j > 0)
        def _():
            o_ref[...] += part


def _moe_experts(xs, blk_exp, n_used, wg, wu, wd, bg, bu, bd, tm, tf):
    rows, d = xs.shape
    n_exp, _, f = wg.shape
    n_blocks, n_f = rows // tm, f // tf

    def blk(i, nu):
        return jnp.minimum(i, nu[0] - 1)

    def ftile(i, j, nu):
        return jnp.where(i < nu[0], j, n_f - 1)

    grid_spec = pltpu.PrefetchScalarGridSpec(
        num_scalar_prefetch=2,
        grid=(n_blocks, n_f),
        in_specs=[pl.BlockSpec((tm, d), lambda i, j, be, nu: (blk(i, nu), 0)),
                  pl.BlockSpec((None, d, tf), lambda i, j, be, nu: (be[blk(i, nu)], 0, ftile(i, j, nu))),
                  pl.BlockSpec((None, d, tf), lambda i, j, be, nu: (be[blk(i, nu)], 0, ftile(i, j, nu))),
                  pl.BlockSpec((None, tf, d), lambda i, j, be, nu: (be[blk(i, nu)], ftile(i, j, nu), 0)),
                  pl.BlockSpec((None, 1, tf), lambda i, j, be, nu: (be[blk(i, nu)], 0, ftile(i, j, nu))),
                  pl.BlockSpec((None, 1, tf), lambda i, j, be, nu: (be[blk(i, nu)], 0, ftile(i, j, nu))),
                  pl.BlockSpec((None, 1, d), lambda i, j, be, nu: (be[blk(i, nu)], 0, 0))],
        out_specs=pl.BlockSpec((tm, d), lambda i, j, be, nu: (blk(i, nu), 0)),
    )
    return pl.pallas_call(
        functools.partial(_moe_kernel, n_f=n_f),
        grid_spec=grid_spec,
        out_shape=jax.ShapeDtypeStruct((rows, d), F32),
        compiler_params=_cparams("arbitrary", "arbitrary"),
        name="moe_experts",
    )(blk_exp, n_used, xs, wg, wu, wd, bg.reshape(n_exp, 1, f), bu.reshape(n_exp, 1, f),
      bd.reshape(n_exp, 1, d))


def _route(top_idx, n_exp, tm):
    n_tok, k = top_idx.shape
    m = n_tok * k
    e_flat = top_idx.reshape(m)
    onehot = (e_flat[:, None] == jnp.arange(n_exp, dtype=jnp.int32)[None, :]).astype(jnp.int32)
    csum = jnp.cumsum(onehot, axis=0)
    counts = csum[-1]
    rank = jnp.take_along_axis(csum, e_flat[:, None], axis=1)[:, 0] - 1
    padded = (counts + tm - 1) // tm * tm
    pend = jnp.cumsum(padded)
    dest = (pend - padded)[e_flat] + rank
    n_blocks = -(-m // tm) + n_exp
    row_tok = jnp.zeros((n_blocks * tm,), jnp.int32).at[dest].set(jnp.arange(m, dtype=jnp.int32) // k)
    blk_exp = jnp.minimum(jnp.searchsorted(pend, jnp.arange(n_blocks, dtype=jnp.int32) * tm, side='right'),
                          n_exp - 1).astype(jnp.int32)
    n_used = (pend[-1:] // tm).astype(jnp.int32)
    return dest, row_tok, blk_exp, n_used


def _final_kernel(x1_ref, yg_ref, gate_ref, gt_ref, nw_ref, o_ref):
    g = gate_ref[...]
    moe = yg_ref[0] * g[:, 0:1]
    for k in range(1, TOP_K):
        moe = moe + yg_ref[k] * g[:, k:k + 1]
    x = x1_ref[...] + gt_ref[...] * moe
    ms = jnp.mean(x * x, axis=-1, keepdims=True)
    o_ref[...] = x * lax.rsqrt(ms + EPS) * nw_ref[...]


def _final(x1, yg, gates, gt2, nw):
    b, t, d = x1.shape
    tm = min(t, 256)
    tpb = t // tm
    row = lambda width: pl.BlockSpec((None, tm, width), lambda i: (i // tpb, i % tpb, 0))
    return pl.pallas_call(
        _final_kernel,
        grid=(b * tpb,),
        in_specs=[row(d),
                  pl.BlockSpec((TOP_K, None, tm, d), lambda i: (0, i // tpb, i % tpb, 0)),
                  row(LANES),
                  pl.BlockSpec((None, 1, d), lambda i: (i // tpb, 0, 0)),
                  pl.BlockSpec((1, d), lambda i: (0, 0))],
        out_specs=row(d),
        out_shape=jax.ShapeDtypeStruct((b, t, d), F32),
        compiler_params=_cparams("parallel"),
        name="final",
    )(x1, yg, gates, gt2, nw.reshape(1, d))


def kernel(x, c, ctx, c_ctx, w_ada, b_ada, norm_mix_w, w_in, hg_lb_f, hg_lb_b, hg_norm_w, gd_conv_w,
           gd_a_log_f, gd_a_log_b, gd_dt_bias_f, gd_dt_bias_b, gd_norm_w, w_out, norm_ffn_w, w_router,
           b_router, w_gate, b_gate, w_up, b_up, w_down, b_down, norm_out_w):
    bsz, t, d = x.shape
    t_ctx = ctx.shape[1]
    n_exp = w_router.shape[-1]
    l = 0
    hg_w = N_HEADS * HEAD_DIM
    n_main = 9 * hg_w

    lb_f = jnp.cumsum(jax.nn.softmax(hg_lb_f.astype(F32), axis=0), axis=0)[l].reshape(1, hg_w)
    lb_b = jnp.cumsum(jax.nn.softmax(hg_lb_b.astype(F32), axis=0), axis=0)[l].reshape(1, hg_w)

    rows = -(-(bsz + 1) // 8) * 8
    cc = jnp.zeros((rows, d), F32).at[:bsz].set(c).at[bsz].set(c_ctx)
    mod = _ada(cc, w_ada[l], b_ada[l])
    sh1, sc1, gt1, sh2, sc2, gt2 = (mod[:bsz, i * d:(i + 1) * d].reshape(bsz, 1, d) for i in range(6))
    csh1 = jnp.broadcast_to(mod[bsz, 0:d].reshape(1, 1, d), (bsz, 1, d))
    csc1 = jnp.broadcast_to(mod[bsz, d:2 * d].reshape(1, 1, d), (bsz, 1, d))

    w_in_b = w_in[l].astype(BF16)
    w_main = w_in_b[:, :n_main]
    w_gates = jnp.pad(w_in_b[:, n_main:], ((0, 0), (0, LANES - (w_in_b.shape[1] - n_main))))
    px, gx = _inproj(x, norm_mix_w[l], sc1, sh1, w_main, w_gates)
    pc, gc = _inproj(ctx, norm_mix_w[l], csc1, csh1, w_main, w_gates)

    hg_nw = hg_norm_w[l].reshape(1, HEAD_DIM)
    hs_f, hs_b = _hgrn_states(pc, lb_f, lb_b)
    mix_h = _hgrn_out(px, lb_f, lb_b, hg_nw, hs_f, hs_b)

    hp = jnp.stack([gd_a_log_f[l], gd_dt_bias_f[l], gd_a_log_b[l], gd_dt_bias_b[l]], axis=1)
    hp = jnp.broadcast_to(jnp.pad(hp, ((0, 0), (0, 4)))[:, :, None], (N_HEADS, 8, HEAD_DIM)).astype(F32)
    gd_nw = gd_norm_w[l].reshape(1, HEAD_DIM)
    gs_f, gs_b = _gdn_call(pc, gc, gd_conv_w[l], hp, gd_nw, None, None, t_ctx, False, 5)
    mix_g = _gdn_call(px, gx, gd_conv_w[l], hp, gd_nw, gs_f, gs_b, GRID_W, True, 5)

    w_out_b = w_out[l].astype(BF16)
    wr = jnp.pad(w_router[l], ((0, 0), (0, LANES - n_exp)))
    wr_hi = wr.astype(BF16)
    wr_lo = (wr - wr_hi.astype(F32)).astype(BF16)
    br = jnp.full((1, LANES), NEG_BIG, F32).at[0, :n_exp].set(b_router[l])
    x1, h2, idx_pad, gate_pad = _outproj(mix_h, mix_g, x, gt1, sc2, sh2, norm_ffn_w[l],
                                         w_out_b[:hg_w], w_out_b[hg_w:], wr_hi, wr_lo, br)

    n_tok = bsz * t
    top_idx = idx_pad.reshape(n_tok, LANES)[:, :TOP_K]
    dest, row_tok, blk_exp, n_used = _route(top_idx, n_exp, MOE_TM)
    xs = h2.reshape(n_tok, d)[row_tok]
    ys = _moe_experts(xs, blk_exp, n_used, w_gate[l].astype(BF16), w_up[l].astype(BF16),
                      w_down[l].astype(BF16), b_gate[l], b_up[l], b_down[l], MOE_TM, MOE_TF)

    yg = ys[dest.reshape(n_tok, TOP_K).T].reshape(TOP_K, bsz, t, d)
    return _final(x1, yg, gate_pad, gt2, norm_out_w)
```

```python
import functools

import jax
import jax.numpy as jnp
from jax import lax
from jax.experimental import pallas as pl
from jax.experimental.pallas import tpu as pltpu

F32 = jnp.float32
BF16 = jnp.bfloat16

EPS = 1e-6
CHUNK = 64
HEAD_DIM = 128
N_HEADS = 8
GRID_W = 64
TOP_K = 4
SWIGLU_LIMIT = 7.0
SWIGLU_ALPHA = 1.702
LANES = 128
NEG_BIG = -1e30

HGRN_GROUP = 4
GDN_GROUP = 4
GDN_INTRA_GROUP = 8
MOE_TM = 512
MOE_TF = 1024
VMEM_LIMIT = 56 * 1024 * 1024


def _cparams(*sem):
    return pltpu.CompilerParams(dimension_semantics=sem, vmem_limit_bytes=VMEM_LIMIT)


def _dot(a, b):
    return jnp.dot(a, b, preferred_element_type=F32)


def _dot_nt(a, b):
    return lax.dot_general(a, b, (((1,), (1,)), ((), ())), preferred_element_type=F32)


def _split2(x):
    hi = x.astype(BF16)
    return hi, (x - hi.astype(F32)).astype(BF16)


def _split3(x):
    hi = x.astype(BF16)
    rest = x - hi.astype(F32)
    mid = rest.astype(BF16)
    return hi, mid, (rest - mid.astype(F32)).astype(BF16)


def _dot_split(a, b):
    ah, al = a
    bh, bl = b
    return _dot(ah, bh) + (_dot(ah, bl) + _dot(al, bh))


def _sigmoid(x):
    return 1.0 / (1.0 + jnp.exp(-x))


def _ada_kernel(c_ref, w_ref, b_ref, o_ref):
    c = c_ref[...]
    s = (c * _sigmoid(c)).astype(BF16)
    o_ref[...] = _dot(s, w_ref[...].astype(BF16)) + b_ref[...]


def _ada(cc, w_ada, b_ada):
    rows, d = cc.shape
    n = w_ada.shape[1]
    tn = 1024
    return pl.pallas_call(
        _ada_kernel,
        grid=(n // tn,),
        in_specs=[pl.BlockSpec((rows, d), lambda j: (0, 0)),
                  pl.BlockSpec((d, tn), lambda j: (0, j)),
                  pl.BlockSpec((1, tn), lambda j: (0, j))],
        out_specs=pl.BlockSpec((rows, tn), lambda j: (0, j)),
        out_shape=jax.ShapeDtypeStruct((rows, n), F32),
        compiler_params=_cparams("arbitrary"),
        name="ada",
    )(cc, w_ada, b_ada.reshape(1, n))


def _modulated_norm(x, nw, sc, sh):
    ms = jnp.mean(x * x, axis=-1, keepdims=True)
    return (x * lax.rsqrt(ms + EPS) * nw) * (1.0 + sc) + sh


def _inproj_kernel(x_ref, nw_ref, sc_ref, sh_ref, w_ref, wg_ref, o_ref, og_ref, h_scr):
    @pl.when(pl.program_id(1) == 0)
    def _():
        h = _modulated_norm(x_ref[...], nw_ref[...], sc_ref[...], sh_ref[...]).astype(BF16)
        h_scr[...] = h
        og_ref[...] = _dot(h, wg_ref[...])

    o_ref[...] = _dot(h_scr[...], w_ref[...])


def _inproj(x, nw, sc, sh, w_main, w_gates):
    b, t, d = x.shape
    n = w_main.shape[1]
    tm = min(t, 1024)
    tn = 1024
    tpb = t // tm
    return pl.pallas_call(
        _inproj_kernel,
        grid=(b * tpb, n // tn),
        in_specs=[pl.BlockSpec((None, tm, d), lambda i, j: (i // tpb, i % tpb, 0)),
                  pl.BlockSpec((1, d), lambda i, j: (0, 0)),
                  pl.BlockSpec((None, 1, d), lambda i, j: (i // tpb, 0, 0)),
                  pl.BlockSpec((None, 1, d), lambda i, j: (i // tpb, 0, 0)),
                  pl.BlockSpec((d, tn), lambda i, j: (0, j)),
                  pl.BlockSpec((d, LANES), lambda i, j: (0, 0))],
        out_specs=[pl.BlockSpec((None, tm, tn), lambda i, j: (i // tpb, i % tpb, j)),
                   pl.BlockSpec((None, tm, LANES), lambda i, j: (i // tpb, i % tpb, 0))],
        out_shape=[jax.ShapeDtypeStruct((b, t, n), F32),
                   jax.ShapeDtypeStruct((b, t, LANES), F32)],
        scratch_shapes=[pltpu.VMEM((tm, d), BF16)],
        compiler_params=_cparams("parallel", "arbitrary"),
        name="inproj",
    )(x, nw.reshape(1, d), sc, sh, w_main, w_gates)


def _chunk_masks(rev):
    r = lax.broadcasted_iota(jnp.int32, (CHUNK, CHUNK), 0)
    c = lax.broadcasted_iota(jnp.int32, (CHUNK, CHUNK), 1)
    incl = (c >= r) if rev else (c <= r)
    strict = (c > r) if rev else (c < r)
    return incl, strict


def _gated_head_norm(o, z, nw):
    ms = jnp.mean(o * o, axis=-1, keepdims=True)
    return (o * lax.rsqrt(ms + EPS) * nw) * (z * _sigmoid(z))


def _group_slices(g, group, n_chunks):
    idxs, revs = [], []
    for i in range(group):
        n = g * group + i
        idxs += [n, n_chunks - 1 - n]
        revs += [False, True]
    return [pl.ds(pl.multiple_of(ix * CHUNK, CHUNK), CHUNK) for ix in idxs], idxs, revs


def _scan_group(n_chunks, want, two_phase):
    g = want
    while g > 1 and (n_chunks % g or (two_phase and (n_chunks // g) % 2)):
        g //= 2
    assert n_chunks % g == 0 and not (two_phase and (n_chunks // g) % 2)
    return g


def _gla_group(zfs, vs, qs, lbs, revs):
    n = len(zfs)
    incl = [_chunk_masks(r)[0] for r in revs]
    logf, k = [], []
    for i in range(n):
        one_m = 1.0 - lbs[i]
        logf.append(jnp.log(lbs[i] + one_m * _sigmoid(zfs[i])))
        k.append(one_m * _sigmoid(-zfs[i]))
    cum = []
    for i in range(n):
        tri = incl[i].astype(BF16)
        hi, mid, lo = _split3(logf[i])
        cum.append(_dot(tri, hi) + (_dot(tri, mid) + _dot(tri, lo)))
    last = [cum[i][0:1] if revs[i] else cum[i][CHUNK - 1:CHUNK] for i in range(n)]
    u_t = [_dot(vs[i].T.astype(BF16), (k[i] * jnp.exp(last[i] - cum[i])).astype(BF16)) for i in range(n)]
    dec = [jnp.exp(x) for x in last]
    if qs is None:
        return u_t, dec, None, None
    scores = []
    for i in range(n):
        mid_row = CHUNK // 2 - 1 if revs[i] else CHUNK // 2
        ref = cum[i][mid_row:mid_row + 1]
        sc = _dot_nt((qs[i] * jnp.exp(cum[i] - ref)).astype(BF16), (k[i] * jnp.exp(ref - cum[i])).astype(BF16))
        scores.append(jnp.where(incl[i], sc, 0.0).astype(BF16))
    o_intra = [_dot(scores[i], vs[i].astype(BF16)) for i in range(n)]
    qe = [(qs[i] * jnp.exp(cum[i])).astype(BF16) for i in range(n)]
    return u_t, dec, o_intra, qe


def _hgrn_out_kernel(q_ref, ff_ref, fb_ref, v_ref, g_ref, lbf_ref, lbb_ref, nw_ref, s0f_ref, s0b_ref,
                     o_ref, o_scr, sf_scr, sb_scr, *, n_chunks, group):
    sf_scr[...] = s0f_ref[...]
    sb_scr[...] = s0b_ref[...]
    lbf, lbb, nw = lbf_ref[...], lbb_ref[...], nw_ref[...]

    def body(g, carry, final):
        sls, _, revs = _group_slices(g, group, n_chunks)
        zfs = [(fb_ref if r else ff_ref)[sl, :] for sl, r in zip(sls, revs)]
        vs = [v_ref[sl, :] for sl in sls]
        qs = [q_ref[sl, :] for sl in sls]
        u_t, dec, o_intra, qe = _gla_group(zfs, vs, qs, [lbb if r else lbf for r in revs], revs)
        for i, (sl, rev) in enumerate(zip(sls, revs)):
            s_scr = sb_scr if rev else sf_scr
            st = s_scr[...]
            o = o_intra[i] + _dot_nt(qe[i], st.astype(BF16))
            s_scr[...] = st * dec[i] + u_t[i]
            if final:
                o_ref[sl, :] = _gated_head_norm(o_scr[sl, :] + o, g_ref[sl, :], nw).astype(o_ref.dtype)
            else:
                o_scr[sl, :] = o
        return carry

    trips = n_chunks // group
    lax.fori_loop(0, trips // 2, functools.partial(body, final=False), 0)
    lax.fori_loop(trips // 2, trips, functools.partial(body, final=True), 0)


def _hgrn_state_kernel(ff_ref, fb_ref, v_ref, lbf_ref, lbb_ref, sf_ref, sb_ref, *, n_chunks, group):
    sf_ref[...] = jnp.zeros_like(sf_ref)
    sb_ref[...] = jnp.zeros_like(sb_ref)
    lbf, lbb = lbf_ref[...], lbb_ref[...]

    def body(g, carry):
        sls, _, revs = _group_slices(g, group, n_chunks)
        zfs = [(fb_ref if r else ff_ref)[sl, :] for sl, r in zip(sls, revs)]
        vs = [v_ref[sl, :] for sl in sls]
        u_t, dec, _, _ = _gla_group(zfs, vs, None, [lbb if r else lbf for r in revs], revs)
        for i, rev in enumerate(revs):
            s_ref = sb_ref if rev else sf_ref
            s_ref[...] = s_ref[...] * dec[i] + u_t[i]
        return carry

    lax.fori_loop(0, n_chunks // group, body, 0)


def _col_spec(t, section):
    return pl.BlockSpec((None, t, HEAD_DIM), lambda b, h: (b, 0, section * N_HEADS + h))


def _head_row_spec(offset=0):
    return pl.BlockSpec((1, HEAD_DIM), lambda b, h: (0, offset + h))


_STATE_SPEC = pl.BlockSpec((None, None, HEAD_DIM, HEAD_DIM), lambda b, h: (b, h, 0, 0))


def _hgrn_states(pc, lbf, lbb):
    b, t, _ = pc.shape
    n_chunks = t // CHUNK
    shp = jax.ShapeDtypeStruct((b, N_HEADS, HEAD_DIM, HEAD_DIM), F32)
    return pl.pallas_call(
        functools.partial(_hgrn_state_kernel, n_chunks=n_chunks, group=_scan_group(n_chunks, HGRN_GROUP, False)),
        grid=(b, N_HEADS),
        in_specs=[_col_spec(t, 1), _col_spec(t, 2), _col_spec(t, 3), _head_row_spec(), _head_row_spec()],
        out_specs=[_STATE_SPEC, _STATE_SPEC],
        out_shape=[shp, shp],
        compiler_params=_cparams("parallel", "parallel"),
        name="hgrn_ctx",
    )(pc, pc, pc, lbf, lbb)


def _hgrn_out(px, lbf, lbb, nw, s0f, s0b):
    b, t, _ = px.shape
    n_chunks = t // CHUNK
    return pl.pallas_call(
        functools.partial(_hgrn_out_kernel, n_chunks=n_chunks, group=_scan_group(n_chunks, HGRN_GROUP, True)),
        grid=(b, N_HEADS),
        in_specs=[_col_spec(t, 0), _col_spec(t, 1), _col_spec(t, 2), _col_spec(t, 3), _col_spec(t, 4),
                  _head_row_spec(), _head_row_spec(),
                  pl.BlockSpec((1, HEAD_DIM), lambda b_, h: (0, 0)),
                  _STATE_SPEC, _STATE_SPEC],
        out_specs=pl.BlockSpec((None, t, HEAD_DIM), lambda b_, h: (b_, 0, h)),
        out_shape=jax.ShapeDtypeStruct((b, t, N_HEADS * HEAD_DIM), BF16),
        scratch_shapes=[pltpu.VMEM((t, HEAD_DIM), F32),
                        pltpu.VMEM((HEAD_DIM, HEAD_DIM), F32),
                        pltpu.VMEM((HEAD_DIM, HEAD_DIM), F32)],
        compiler_params=_cparams("parallel", "parallel"),
        name="hgrn_x",
    )(px, px, px, px, px, lbf, lbb, nw, s0f, s0b)


def _conv_silu(a, w, period):
    rows = a.shape[0]
    pos = lax.broadcasted_iota(jnp.int32, a.shape, 0) % period
    prev = jnp.where(pos == 0, 0.0, pltpu.roll(a, 1, axis=0))
    nxt = jnp.where(pos == period - 1, 0.0, pltpu.roll(a, rows - 1, axis=0))
    y = prev * w[0:1] + a * w[1:2] + nxt * w[2:3]
    return y * _sigmoid(y)


def _l2norm(x):
    return x * lax.rsqrt(jnp.sum(x * x, axis=-1, keepdims=True) + EPS)


def _softplus(x):
    return jnp.maximum(x, 0.0) + jnp.log1p(jnp.exp(-jnp.abs(x)))


def _unit_tri_inverses(tris):
    r = lax.broadcasted_iota(jnp.int32, (CHUNK, CHUNK), 0)
    c = lax.broadcasted_iota(jnp.int32, (CHUNK, CHUNK), 1)
    eye = jnp.where(r == c, 1.0, 0.0)
    ps = [_split2(-a) for a in tris]
    invs = [eye - a for a in tris]
    for _ in range(CHUNK.bit_length() - 2):
        ps = [_split2(_dot_split(p, p)) for p in ps]
        invs = [inv + _dot_split(_split2(inv), p) for inv, p in zip(invs, ps)]
    return invs


def _gdn_group(qs, ks, vs, a_cs, b_cs, a_rs, alogs, dtbs, revs):
    n = len(ks)
    dmask, ecum, beta, eend, dec, strict = [], [], [], [], [], []
    for i in range(n):
        incl, st = _chunk_masks(revs[i])
        incl_t = _chunk_masks(not revs[i])[0]
        scale = -jnp.exp(alogs[i])
        g_c = scale * _softplus(a_cs[i] + dtbs[i])
        g_r = scale * _softplus(a_rs[i] + dtbs[i])
        cum_c = jnp.sum(jnp.where(incl, g_r, 0.0), axis=1, keepdims=True)
        cum_r = jnp.sum(jnp.where(incl_t, g_c, 0.0), axis=0, keepdims=True)
        dmask.append(jnp.exp(jnp.where(incl, cum_c - cum_r, -jnp.inf)))
        ecum.append(jnp.exp(cum_c))
        beta.append(_sigmoid(b_cs[i]))
        last = cum_c[0:1] if revs[i] else cum_c[CHUNK - 1:CHUNK]
        eend.append(jnp.exp(last - cum_c))
        dec.append(jnp.exp(last))
        strict.append(st)
    kbf = [k.astype(BF16) for k in ks]
    kb = [ks[i] * beta[i] for i in range(n)]
    tris = [jnp.where(strict[i], _dot_nt(kb[i].astype(BF16), kbf[i]) * dmask[i], 0.0) for i in range(n)]
    tinv = [t.astype(BF16) for t in _unit_tri_inverses(tris)]
    u = [_dot(tinv[i], (vs[i] * beta[i]).astype(BF16)).astype(BF16) for i in range(n)]
    w = [_dot(tinv[i], (kb[i] * ecum[i]).astype(BF16)).astype(BF16) for i in range(n)]
    ke_t = [(ks[i] * eend[i]).T.astype(BF16) for i in range(n)]
    mp = [_dot(ke_t[i], w[i]) for i in range(n)]
    cc = [_dot(ke_t[i], u[i]) for i in range(n)]
    if qs is None:
        return mp, cc, dec, None, None
    attn = [(_dot_nt(qs[i].astype(BF16), kbf[i]) * dmask[i]).astype(BF16) for i in range(n)]
    qp = [qs[i] * ecum[i] - _dot(attn[i], w[i]) for i in range(n)]
    oi = [_dot(attn[i], u[i]) for i in range(n)]
    return mp, cc, dec, qp, oi


def _gdn_kernel(*refs, n_chunks, period, prep_rows, group, igroup, with_out):
    if with_out:
        (q_ref, k_ref, v_ref, z_ref, gc_ref, gr_ref, wq_ref, wk_ref, wv_ref, hp_ref, nw_ref,
         s0f_ref, s0b_ref, o_ref, qn, kn, vn, mp_s, cc_s, dc_s, qp_s, o_scr, sf, sb) = refs
    else:
        (k_ref, v_ref, gc_ref, gr_ref, wk_ref, wv_ref, hp_ref,
         sf, sb, kn, vn, mp_s, cc_s, dc_s) = refs
        q_ref = None

    t = n_chunks * CHUNK

    def prep(i, carry):
        sl = pl.ds(pl.multiple_of(i * prep_rows, prep_rows), prep_rows)
        kn[sl, :] = _l2norm(_conv_silu(k_ref[sl, :], wk_ref[...], period))
        vn[sl, :] = _conv_silu(v_ref[sl, :], wv_ref[...], period)
        if with_out:
            qn[sl, :] = _l2norm(_conv_silu(q_ref[sl, :], wq_ref[...], period)) * (HEAD_DIM ** -0.5)
        return carry

    lax.fori_loop(0, t // prep_rows, prep, 0)

    hp = hp_ref[...]

    def intra(g, carry):
        qs, ks, vs, a_cs, b_cs, a_rs, alogs, dtbs, revs, where = [], [], [], [], [], [], [], [], [], []
        for i in range(igroup):
            n = g * igroup + i
            sl = pl.ds(pl.multiple_of(n * CHUNK, CHUNK), CHUNK)
            gc = gc_ref[sl, :]
            gr = gr_ref[n]
            for d, rev in enumerate((False, True)):
                if with_out:
                    qs.append(qn[sl, :])
                ks.append(kn[sl, :])
                vs.append(vn[sl, :])
                a_cs.append(gc[:, d:d + 1])
                b_cs.append(gc[:, 2 + d:3 + d])
                a_rs.append(gr[d:d + 1, :])
                alogs.append(hp[2 * d:2 * d + 1, 0:1])
                dtbs.append(hp[2 * d + 1:2 * d + 2, 0:1])
                revs.append(rev)
                where.append((d, n, sl))
        mp, cc, dec, qp, oi = _gdn_group(qs if with_out else None, ks, vs, a_cs, b_cs, a_rs, alogs, dtbs, revs)
        for i, (d, n, sl) in enumerate(where):
            mp_s[d, n] = mp[i].astype(BF16)
            cc_s[d, n] = cc[i]
            dc_s[d, n] = jnp.broadcast_to(dec[i], (8, HEAD_DIM))
            if with_out:
                qp_s[d, sl, :] = qp[i].astype(BF16)
                if d == 1:
                    o_scr[sl, :] = oi[i - 1] + oi[i]
        return carry

    lax.fori_loop(0, n_chunks // igroup, intra, 0)

    if with_out:
        sf[...] = s0f_ref[...]
        sb[...] = s0b_ref[...]
        nw = nw_ref[...]
    else:
        sf[...] = jnp.zeros_like(sf)
        sb[...] = jnp.zeros_like(sb)

    def scan(g, carry, final):
        sls, idxs, revs = _group_slices(g, group, n_chunks)
        for sl, idx, rev in zip(sls, idxs, revs):
            d = int(rev)
            s_ref = sb if rev else sf
            s = s_ref[...]
            sbf = s.astype(BF16)
            s_ref[...] = s * dc_s[d, idx][0:1, :] - _dot(mp_s[d, idx], sbf) + cc_s[d, idx]
            if with_out:
                o = o_scr[sl, :] + _dot(qp_s[d, sl, :], sbf)
                if final:
                    o_ref[sl, :] = _gated_head_norm(o, z_ref[sl, :], nw).astype(o_ref.dtype)
                else:
                    o_scr[sl, :] = o
        return carry

    trips = n_chunks // group
    if with_out:
        lax.fori_loop(0, trips // 2, functools.partial(scan, final=False), 0)
        lax.fori_loop(trips // 2, trips, functools.partial(scan, final=True), 0)
    else:
        lax.fori_loop(0, trips, functools.partial(scan, final=False), 0)


def _gdn_call(p, gates, conv_w, hp, nw, s0f, s0b, period, with_out, sec0):
    b, t, _ = p.shape
    n_chunks = t // CHUNK
    prep_rows = max(period, min(t, 256))
    assert prep_rows % period == 0 and t % prep_rows == 0
    group = _scan_group(n_chunks, GDN_GROUP, with_out)
    g4 = gates[:, :, :4 * N_HEADS].reshape(b, t, 4, N_HEADS)
    g_col = g4.transpose(0, 3, 1, 2)
    g_row = jnp.pad(g4.transpose(0, 3, 2, 1), ((0, 0), (0, 0), (0, 4), (0, 0)))
    g_row = g_row.reshape(b, N_HEADS, 8, n_chunks, CHUNK).transpose(0, 1, 3, 2, 4)

    gc_spec = pl.BlockSpec((None, None, t, 4), lambda b_, h: (b_, h, 0, 0))
    gr_spec = pl.BlockSpec((None, None, n_chunks, 8, CHUNK), lambda b_, h: (b_, h, 0, 0, 0))
    conv_spec = lambda sec: pl.BlockSpec((3, HEAD_DIM), lambda b_, h: (0, sec * N_HEADS + h))
    hp_spec = pl.BlockSpec((None, 8, HEAD_DIM), lambda b_, h: (h, 0, 0))
    seq = lambda: pltpu.VMEM((t, HEAD_DIM), F32)
    mats = lambda dt: pltpu.VMEM((2, n_chunks, HEAD_DIM, HEAD_DIM), dt)
    dc = pltpu.VMEM((2, n_chunks, 8, HEAD_DIM), F32)
    state = lambda: pltpu.VMEM((HEAD_DIM, HEAD_DIM), F32)
    kern = functools.partial(_gdn_kernel, n_chunks=n_chunks, period=period, prep_rows=prep_rows,
                             group=group, igroup=_scan_group(n_chunks, GDN_INTRA_GROUP, False),
                             with_out=with_out)
    if with_out:
        return pl.pallas_call(
            kern,
            grid=(b, N_HEADS),
            in_specs=[_col_spec(t, sec0), _col_spec(t, sec0 + 1), _col_spec(t, sec0 + 2), _col_spec(t, sec0 + 3),
                      gc_spec, gr_spec, conv_spec(0), conv_spec(1), conv_spec(2), hp_spec,
                      pl.BlockSpec((1, HEAD_DIM), lambda b_, h: (0, 0)), _STATE_SPEC, _STATE_SPEC],
            out_specs=pl.BlockSpec((None, t, HEAD_DIM), lambda b_, h: (b_, 0, h)),
            out_shape=jax.ShapeDtypeStruct((b, t, N_HEADS * HEAD_DIM), BF16),
            scratch_shapes=[seq(), seq(), seq(), mats(BF16), mats(F32), dc,
                            pltpu.VMEM((2, t, HEAD_DIM), BF16), seq(), state(), state()],
            compiler_params=_cparams("parallel", "parallel"),
            name="gdn_x",
        )(p, p, p, p, g_col, g_row, conv_w, conv_w, conv_w, hp, nw, s0f, s0b)
    shp = jax.ShapeDtypeStruct((b, N_HEADS, HEAD_DIM, HEAD_DIM), F32)
    return pl.pallas_call(
        kern,
        grid=(b, N_HEADS),
        in_specs=[_col_spec(t, sec0 + 1), _col_spec(t, sec0 + 2), gc_spec, gr_spec,
                  conv_spec(1), conv_spec(2), hp_spec],
        out_specs=[_STATE_SPEC, _STATE_SPEC],
        out_shape=[shp, shp],
        scratch_shapes=[seq(), seq(), mats(BF16), mats(F32), dc],
        compiler_params=_cparams("parallel", "parallel"),
        name="gdn_ctx",
    )(p, p, g_col, g_row, conv_w, conv_w, hp)


def _outproj_kernel(mh_ref, mg_ref, x_ref, gt_ref, sc_ref, sh_ref, nw_ref, woh_ref, wog_ref,
                    wrh_ref, wrl_ref, br_ref, x1_ref, h2_ref, idx_ref, gate_ref):
    y = _dot(mh_ref[...], woh_ref[...]) + _dot(mg_ref[...], wog_ref[...])
    x1 = x_ref[...] + gt_ref[...] * y
    x1_ref[...] = x1
    h = _modulated_norm(x1, nw_ref[...], sc_ref[...], sh_ref[...])
    hh = h.astype(BF16)
    h2_ref[...] = hh
    hl = (h - hh.astype(F32)).astype(BF16)
    wrh = wrh_ref[...]
    logits = _dot(hh, wrh) + _dot(hl, wrh) + _dot(hh, wrl_ref[...]) + br_ref[...]
    lane = lax.broadcasted_iota(jnp.int32, logits.shape, 1).astype(F32)
    vals, idxs = [], []
    for _ in range(TOP_K):
        m = jnp.max(logits, axis=-1, keepdims=True)
        i = jnp.min(jnp.where(logits == m, lane, float(LANES)), axis=-1, keepdims=True)
        vals.append(m)
        idxs.append(i)
        logits = jnp.where(lane == i, -jnp.inf, logits)
    es = [jnp.exp(v - vals[0]) for v in vals]
    inv = 1.0 / functools.reduce(lambda a, b_: a + b_, es)
    idx_out = jnp.zeros(lane.shape, F32)
    gate_out = jnp.zeros(lane.shape, F32)
    for k in range(TOP_K):
        idx_out = jnp.where(lane == k, idxs[k], idx_out)
        gate_out = jnp.where(lane == k, es[k] * inv, gate_out)
    idx_ref[...] = idx_out.astype(jnp.int32)
    gate_ref[...] = gate_out


def _outproj(mix_h, mix_g, x, gt1, sc2, sh2, nw, wo_h, wo_g, wr_hi, wr_lo, br):
    b, t, d = x.shape
    w = mix_h.shape[-1]
    tm = min(t, 512)
    tpb = t // tm
    row = lambda width: pl.BlockSpec((None, tm, width), lambda i: (i // tpb, i % tpb, 0))
    per_b = pl.BlockSpec((None, 1, d), lambda i: (i // tpb, 0, 0))
    const = lambda r, c: pl.BlockSpec((r, c), lambda i: (0, 0))
    return pl.pallas_call(
        _outproj_kernel,
        grid=(b * tpb,),
        in_specs=[row(w), row(w), row(d), per_b, per_b, per_b, const(1, d), const(w, d), const(w, d),
                  const(d, LANES), const(d, LANES), const(1, LANES)],
        out_specs=[row(d), row(d), row(LANES), row(LANES)],
        out_shape=[jax.ShapeDtypeStruct((b, t, d), F32), jax.ShapeDtypeStruct((b, t, d), BF16),
                   jax.ShapeDtypeStruct((b, t, LANES), jnp.int32), jax.ShapeDtypeStruct((b, t, LANES), F32)],
        compiler_params=_cparams("parallel"),
        name="outproj",
    )(mix_h, mix_g, x, gt1, sc2, sh2, nw.reshape(1, d), wo_h, wo_g, wr_hi, wr_lo, br)


def _cast_tile(src, dst):
    rows = 256

    def body(i, carry):
        sl = pl.ds(pl.multiple_of(i * rows, rows), rows)
        dst[sl, :] = src[sl, :].astype(dst.dtype)
        return carry

    lax.fori_loop(0, src.shape[0] // rows, body, 0)


def _moe_up_kernel(e_ref, j_ref, r_ref, ro_ref, jo_ref, first_ref, valid_ref,
                   x_ref, wg_ref, wu_ref, bg_ref, bu_ref, act_ref, wg_b, wu_b):
    s = pl.program_id(0)

    @pl.when(first_ref[s] == 1)
    def _():
        _cast_tile(wg_ref, wg_b)
        _cast_tile(wu_ref, wu_b)

    @pl.when(valid_ref[s] == 1)
    def _():
        x = x_ref[...]
        gate = jnp.minimum(_dot(x, wg_b[...]) + bg_ref[...], SWIGLU_LIMIT)
        up = jnp.clip(_dot(x, wu_b[...]) + bu_ref[...], -SWIGLU_LIMIT, SWIGLU_LIMIT)
        act_ref[...] = ((up + 1.0) * gate * _sigmoid(SWIGLU_ALPHA * gate)).astype(act_ref.dtype)

    @pl.when(valid_ref[s] == 0)
    def _():
        act_ref[...] = jnp.zeros_like(act_ref)


def _moe_down_kernel(e_ref, j_ref, r_ref, ro_ref, jo_ref, first_ref, valid_ref,
                     a_ref, wd_ref, bd_ref, y_ref, wd_b):
    s = pl.program_id(0)

    @pl.when(first_ref[s] == 1)
    def _():
        _cast_tile(wd_ref, wd_b)

    @pl.when(valid_ref[s] == 1)
    def _():
        y_ref[...] = _dot(a_ref[...], wd_b[...]) + bd_ref[...]

    @pl.when(valid_ref[s] == 0)
    def _():
        y_ref[...] = jnp.zeros_like(y_ref)


def _moe_experts(xs, sched, wg, wu, wd, bg, bu, bd, tm, tf):
    rows, d = xs.shape
    n_exp, _, f = wg.shape
    steps = sched[0].shape[0]
    w_tile = lambda shape: pl.BlockSpec((None,) + shape, lambda s, e, j, r, ro, jo, fi, va: (e[s], 0, j[s]))
    act = pl.pallas_call(
        _moe_up_kernel,
        grid_spec=pltpu.PrefetchScalarGridSpec(
            num_scalar_prefetch=7,
            grid=(steps,),
            in_specs=[pl.BlockSpec((tm, d), lambda s, e, j, r, ro, jo, fi, va: (r[s], 0)),
                      w_tile((d, tf)), w_tile((d, tf)), w_tile((1, tf)), w_tile((1, tf))],
            out_specs=pl.BlockSpec((tm, tf), lambda s, e, j, r, ro, jo, fi, va: (ro[s], jo[s])),
            scratch_shapes=[pltpu.VMEM((d, tf), BF16), pltpu.VMEM((d, tf), BF16)]),
        out_shape=jax.ShapeDtypeStruct((rows, f), BF16),
        compiler_params=_cparams("arbitrary"),
        name="moe_up",
    )(*sched, xs, wg, wu, bg.reshape(n_exp, 1, f), bu.reshape(n_exp, 1, f))
    return pl.pallas_call(
        _moe_down_kernel,
        grid_spec=pltpu.PrefetchScalarGridSpec(
            num_scalar_prefetch=7,
            grid=(steps,),
            in_specs=[pl.BlockSpec((tm, f), lambda s, e, j, r, ro, jo, fi, va: (r[s], 0)),
                      w_tile((f, tf)), w_tile((1, tf))],
            out_specs=pl.BlockSpec((tm, tf), lambda s, e, j, r, ro, jo, fi, va: (ro[s], jo[s])),
            scratch_shapes=[pltpu.VMEM((f, tf), BF16)]),
        out_shape=jax.ShapeDtypeStruct((rows, d), F32),
        compiler_params=_cparams("arbitrary"),
        name="moe_down",
    )(*sched, act, wd, bd.reshape(n_exp, 1, d))


def _route(top_idx, n_exp, tm, n_tiles):
    n_tok, k = top_idx.shape
    m = n_tok * k
    e_flat = top_idx.reshape(m)
    onehot = (e_flat[:, None] == jnp.arange(n_exp, dtype=jnp.int32)[None, :]).astype(jnp.int32)
    csum = jnp.cumsum(onehot, axis=0)
    counts = csum[-1]
    rank = jnp.take_along_axis(csum, e_flat[:, None], axis=1)[:, 0] - 1
    padded = (counts + tm - 1) // tm * tm
    pend = jnp.cumsum(padded)
    dest = (pend - padded)[e_flat] + rank
    n_blocks = -(-m // tm) + n_exp
    row_tok = jnp.zeros((n_blocks * tm,), jnp.int32).at[dest].set(jnp.arange(m, dtype=jnp.int32) // k)

    nb = padded // tm
    blk0 = (pend - padded) // tm
    cum = jnp.cumsum(nb * n_tiles)
    total = cum[-1]
    s = jnp.arange(n_tiles * n_blocks, dtype=jnp.int32)
    valid = s < total
    sc = jnp.minimum(s, total - 1)
    e = jnp.minimum(jnp.searchsorted(cum, sc, side='right'), n_exp - 1).astype(jnp.int32)
    local = sc - (cum[e] - nb[e] * n_tiles)
    nbe = jnp.maximum(nb[e], 1)
    j = local // nbe
    r = blk0[e] + local % nbe
    first = valid & (local % nbe == 0)
    extra = s - total
    r_out = jnp.where(valid, r, total // n_tiles + extra // n_tiles)
    j_out = jnp.where(valid, j, extra % n_tiles)
    sched = tuple(a.astype(jnp.int32) for a in (e, j, r, r_out, j_out, first, valid))
    return dest, row_tok, sched


def _final_kernel(x1_ref, yg_ref, gate_ref, gt_ref, nw_ref, o_ref):
    g = gate_ref[...]
    moe = yg_ref[0] * g[:, 0:1]
    for k in range(1, TOP_K):
        moe = moe + yg_ref[k] * g[:, k:k + 1]
    x = x1_ref[...] + gt_ref[...] * moe
    ms = jnp.mean(x * x, axis=-1, keepdims=True)
    o_ref[...] = x * lax.rsqrt(ms + EPS) * nw_ref[...]


def _final(x1, yg, gates, gt2, nw):
    b, t, d = x1.shape
    tm = min(t, 256)
    tpb = t // tm
    row = lambda width: pl.BlockSpec((None, tm, width), lambda i: (i // tpb, i % tpb, 0))
    return pl.pallas_call(
        _final_kernel,
        grid=(b * tpb,),
        in_specs=[row(d),
                  pl.BlockSpec((TOP_K, None, tm, d), lambda i: (0, i // tpb, i % tpb, 0)),
                  row(LANES),
                  pl.BlockSpec((None, 1, d), lambda i: (i // tpb, 0, 0)),
                  pl.BlockSpec((1, d), lambda i: (0, 0))],
        out_specs=row(d),
        out_shape=jax.ShapeDtypeStruct((b, t, d), F32),
        compiler_params=_cparams("parallel"),
        name="final",
    )(x1, yg, gates, gt2, nw.reshape(1, d))


def kernel(x, c, ctx, c_ctx, w_ada, b_ada, norm_mix_w, w_in, hg_lb_f, hg_lb_b, hg_norm_w, gd_conv_w,
           gd_a_log_f, gd_a_log_b, gd_dt_bias_f, gd_dt_bias_b, gd_norm_w, w_out, norm_ffn_w, w_router,
           b_router, w_gate, b_gate, w_up, b_up, w_down, b_down, norm_out_w):
    bsz, t, d = x.shape
    t_ctx = ctx.shape[1]
    n_exp = w_router.shape[-1]
    l = 0
    hg_w = N_HEADS * HEAD_DIM
    n_main = 9 * hg_w

    lb_f = jnp.cumsum(jax.nn.softmax(hg_lb_f.astype(F32), axis=0), axis=0)[l].reshape(1, hg_w)
    lb_b = jnp.cumsum(jax.nn.softmax(hg_lb_b.astype(F32), axis=0), axis=0)[l].reshape(1, hg_w)

    rows = -(-(bsz + 1) // 8) * 8
    cc = jnp.zeros((rows, d), F32).at[:bsz].set(c).at[bsz].set(c_ctx)
    mod = _ada(cc, w_ada[l], b_ada[l])
    sh1, sc1, gt1, sh2, sc2, gt2 = (mod[:bsz, i * d:(i + 1) * d].reshape(bsz, 1, d) for i in range(6))
    csh1 = jnp.broadcast_to(mod[bsz, 0:d].reshape(1, 1, d), (bsz, 1, d))
    csc1 = jnp.broadcast_to(mod[bsz, d:2 * d].reshape(1, 1, d), (bsz, 1, d))

    w_in_b = w_in[l].astype(BF16)
    w_main = w_in_b[:, :n_main]
    w_gates = jnp.pad(w_in_b[:, n_main:], ((0, 0), (0, LANES - (w_in_b.shape[1] - n_main))))
    px, gx = _inproj(x, norm_mix_w[l], sc1, sh1, w_main, w_gates)
    pc, gc = _inproj(ctx, norm_mix_w[l], csc1, csh1, w_main, w_gates)

    hg_nw = hg_norm_w[l].reshape(1, HEAD_DIM)
    hs_f, hs_b = _hgrn_states(pc, lb_f, lb_b)
    mix_h = _hgrn_out(px, lb_f, lb_b, hg_nw, hs_f, hs_b)

    hp = jnp.stack([gd_a_log_f[l], gd_dt_bias_f[l], gd_a_log_b[l], gd_dt_bias_b[l]], axis=1)
    hp = jnp.broadcast_to(jnp.pad(hp, ((0, 0), (0, 4)))[:, :, None], (N_HEADS, 8, HEAD_DIM)).astype(F32)
    gd_nw = gd_norm_w[l].reshape(1, HEAD_DIM)
    gs_f, gs_b = _gdn_call(pc, gc, gd_conv_w[l], hp, gd_nw, None, None, t_ctx, False, 5)
    mix_g = _gdn_call(px, gx, gd_conv_w[l], hp, gd_nw, gs_f, gs_b, GRID_W, True, 5)

    w_out_b = w_out[l].astype(BF16)
    wr = jnp.pad(w_router[l], ((0, 0), (0, LANES - n_exp)))
    wr_hi = wr.astype(BF16)
    wr_lo = (wr - wr_hi.astype(F32)).astype(BF16)
    br = jnp.full((1, LANES), NEG_BIG, F32).at[0, :n_exp].set(b_router[l])
    x1, h2, idx_pad, gate_pad = _outproj(mix_h, mix_g, x, gt1, sc2, sh2, norm_ffn_w[l],
                                         w_out_b[:hg_w], w_out_b[hg_w:], wr_hi, wr_lo, br)

    n_tok = bsz * t
    top_idx = idx_pad.reshape(n_tok, LANES)[:, :TOP_K]
    dest, row_tok, sched = _route(top_idx, n_exp, MOE_TM, w_gate.shape[-1] // MOE_TF)
    xs = h2.reshape(n_tok, d)[row_tok]
    ys = _moe_experts(xs, sched, w_gate[l], w_up[l], w_down[l], b_gate[l], b_up[l], b_down[l], MOE_TM, MOE_TF)

    yg = ys[dest.reshape(n_tok, TOP_K).T].reshape(TOP_K, bsz, t, d)
    return _final(x1, yg, gate_pad, gt2, norm_out_w)
```

```python
import functools

import jax
import jax.numpy as jnp
from jax import lax
from jax.experimental import pallas as pl
from jax.experimental.pallas import tpu as pltpu

F32 = jnp.float32
BF16 = jnp.bfloat16

EPS = 1e-6
CHUNK = 64
HEAD_DIM = 128
N_HEADS = 8
GRID_W = 64
TOP_K = 4
SWIGLU_LIMIT = 7.0
SWIGLU_ALPHA = 1.702
LANES = 128
NEG_BIG = -1e30

HGRN_GROUP = 4
GDN_GROUP = 4
GDN_INTRA_GROUP = 8
MOE_TM = 512
MOE_TF = 1024
VMEM_LIMIT = 56 * 1024 * 1024


def _cparams(*sem):
    return pltpu.CompilerParams(dimension_semantics=sem, vmem_limit_bytes=VMEM_LIMIT)


def _dot(a, b):
    return jnp.dot(a, b, preferred_element_type=F32)


def _dot_nt(a, b):
    return lax.dot_general(a, b, (((1,), (1,)), ((), ())), preferred_element_type=F32)


def _split2(x):
    hi = x.astype(BF16)
    return hi, (x - hi.astype(F32)).astype(BF16)


def _split3(x):
    hi = x.astype(BF16)
    rest = x - hi.astype(F32)
    mid = rest.astype(BF16)
    return hi, mid, (rest - mid.astype(F32)).astype(BF16)


def _dot_split(a, b):
    ah, al = a
    bh, bl = b
    return _dot(ah, bh) + (_dot(ah, bl) + _dot(al, bh))


def _sigmoid(x):
    return 1.0 / (1.0 + jnp.exp(-x))


def _ada_kernel(c_ref, w_ref, b_ref, o_ref):
    c = c_ref[...]
    s = (c * _sigmoid(c)).astype(BF16)
    o_ref[...] = _dot(s, w_ref[...].astype(BF16)) + b_ref[...]


def _ada(cc, w_ada, b_ada):
    rows, d = cc.shape
    n = w_ada.shape[1]
    tn = 1024
    return pl.pallas_call(
        _ada_kernel,
        grid=(n // tn,),
        in_specs=[pl.BlockSpec((rows, d), lambda j: (0, 0)),
                  pl.BlockSpec((d, tn), lambda j: (0, j)),
                  pl.BlockSpec((1, tn), lambda j: (0, j))],
        out_specs=pl.BlockSpec((rows, tn), lambda j: (0, j)),
        out_shape=jax.ShapeDtypeStruct((rows, n), F32),
        compiler_params=_cparams("arbitrary"),
        name="ada",
    )(cc, w_ada, b_ada.reshape(1, n))


def _modulated_norm(x, nw, sc, sh):
    ms = jnp.mean(x * x, axis=-1, keepdims=True)
    return (x * lax.rsqrt(ms + EPS) * nw) * (1.0 + sc) + sh


def _inproj_kernel(x_ref, nw_ref, sc_ref, sh_ref, w_ref, wg_ref, o_ref, og_ref, h_scr):
    @pl.when(pl.program_id(1) == 0)
    def _():
        h = _modulated_norm(x_ref[...], nw_ref[...], sc_ref[...], sh_ref[...]).astype(BF16)
        h_scr[...] = h
        og_ref[...] = _dot(h, wg_ref[...])

    o_ref[...] = _dot(h_scr[...], w_ref[...])


def _inproj(x, nw, sc, sh, w_main, w_gates):
    b, t, d = x.shape
    n = w_main.shape[1]
    tm = min(t, 1024)
    tn = 1024
    tpb = t // tm
    return pl.pallas_call(
        _inproj_kernel,
        grid=(b * tpb, n // tn),
        in_specs=[pl.BlockSpec((None, tm, d), lambda i, j: (i // tpb, i % tpb, 0)),
                  pl.BlockSpec((1, d), lambda i, j: (0, 0)),
                  pl.BlockSpec((None, 1, d), lambda i, j: (i // tpb, 0, 0)),
                  pl.BlockSpec((None, 1, d), lambda i, j: (i // tpb, 0, 0)),
                  pl.BlockSpec((d, tn), lambda i, j: (0, j)),
                  pl.BlockSpec((d, LANES), lambda i, j: (0, 0))],
        out_specs=[pl.BlockSpec((None, tm, tn), lambda i, j: (i // tpb, i % tpb, j)),
                   pl.BlockSpec((None, tm, LANES), lambda i, j: (i // tpb, i % tpb, 0))],
        out_shape=[jax.ShapeDtypeStruct((b, t, n), F32),
                   jax.ShapeDtypeStruct((b, t, LANES), F32)],
        scratch_shapes=[pltpu.VMEM((tm, d), BF16)],
        compiler_params=_cparams("parallel", "arbitrary"),
        name="inproj",
    )(x, nw.reshape(1, d), sc, sh, w_main, w_gates)


def _chunk_masks(rev):
    r = lax.broadcasted_iota(jnp.int32, (CHUNK, CHUNK), 0)
    c = lax.broadcasted_iota(jnp.int32, (CHUNK, CHUNK), 1)
    incl = (c >= r) if rev else (c <= r)
    strict = (c > r) if rev else (c < r)
    return incl, strict


def _gated_head_norm(o, z, nw):
    ms = jnp.mean(o * o, axis=-1, keepdims=True)
    return (o * lax.rsqrt(ms + EPS) * nw) * (z * _sigmoid(z))


def _group_slices(g, group, n_chunks):
    idxs, revs = [], []
    for i in range(group):
        n = g * group + i
        idxs += [n, n_chunks - 1 - n]
        revs += [False, True]
    return [pl.ds(pl.multiple_of(ix * CHUNK, CHUNK), CHUNK) for ix in idxs], idxs, revs


def _scan_group(n_chunks, want, two_phase):
    g = want
    while g > 1 and (n_chunks % g or (two_phase and (n_chunks // g) % 2)):
        g //= 2
    assert n_chunks % g == 0 and not (two_phase and (n_chunks // g) % 2)
    return g


def _gla_group(zfs, vs, qs, lbs, revs):
    n = len(zfs)
    incl = [_chunk_masks(r)[0] for r in revs]
    logf, k = [], []
    for i in range(n):
        one_m = 1.0 - lbs[i]
        logf.append(jnp.log(lbs[i] + one_m * _sigmoid(zfs[i])))
        k.append(one_m * _sigmoid(-zfs[i]))
    cum = []
    for i in range(n):
        tri = incl[i].astype(BF16)
        hi, mid, lo = _split3(logf[i])
        cum.append(_dot(tri, hi) + (_dot(tri, mid) + _dot(tri, lo)))
    last = [cum[i][0:1] if revs[i] else cum[i][CHUNK - 1:CHUNK] for i in range(n)]
    u_t = [_dot(vs[i].T.astype(BF16), (k[i] * jnp.exp(last[i] - cum[i])).astype(BF16)) for i in range(n)]
    dec = [jnp.exp(x) for x in last]
    if qs is None:
        return u_t, dec, None, None
    scores = []
    for i in range(n):
        mid_row = CHUNK // 2 - 1 if revs[i] else CHUNK // 2
        ref = cum[i][mid_row:mid_row + 1]
        sc = _dot_nt((qs[i] * jnp.exp(cum[i] - ref)).astype(BF16), (k[i] * jnp.exp(ref - cum[i])).astype(BF16))
        scores.append(jnp.where(incl[i], sc, 0.0).astype(BF16))
    o_intra = [_dot(scores[i], vs[i].astype(BF16)) for i in range(n)]
    qe = [(qs[i] * jnp.exp(cum[i])).astype(BF16) for i in range(n)]
    return u_t, dec, o_intra, qe


def _hgrn_out_kernel(q_ref, ff_ref, fb_ref, v_ref, g_ref, lbf_ref, lbb_ref, nw_ref, s0f_ref, s0b_ref,
                     o_ref, o_scr, sf_scr, sb_scr, *, n_chunks, group):
    sf_scr[...] = s0f_ref[...]
    sb_scr[...] = s0b_ref[...]
    lbf, lbb, nw = lbf_ref[...], lbb_ref[...], nw_ref[...]

    def body(g, carry, final):
        sls, _, revs = _group_slices(g, group, n_chunks)
        zfs = [(fb_ref if r else ff_ref)[sl, :] for sl, r in zip(sls, revs)]
        vs = [v_ref[sl, :] for sl in sls]
        qs = [q_ref[sl, :] for sl in sls]
        u_t, dec, o_intra, qe = _gla_group(zfs, vs, qs, [lbb if r else lbf for r in revs], revs)
        for i, (sl, rev) in enumerate(zip(sls, revs)):
            s_scr = sb_scr if rev else sf_scr
            st = s_scr[...]
            o = o_intra[i] + _dot_nt(qe[i], st.astype(BF16))
            s_scr[...] = st * dec[i] + u_t[i]
            if final:
                o_ref[sl, :] = _gated_head_norm(o_scr[sl, :] + o, g_ref[sl, :], nw).astype(o_ref.dtype)
            else:
                o_scr[sl, :] = o
        return carry

    trips = n_chunks // group
    lax.fori_loop(0, trips // 2, functools.partial(body, final=False), 0)
    lax.fori_loop(trips // 2, trips, functools.partial(body, final=True), 0)


def _hgrn_state_kernel(ff_ref, fb_ref, v_ref, lbf_ref, lbb_ref, sf_ref, sb_ref, *, n_chunks, group):
    sf_ref[...] = jnp.zeros_like(sf_ref)
    sb_ref[...] = jnp.zeros_like(sb_ref)
    lbf, lbb = lbf_ref[...], lbb_ref[...]

    def body(g, carry):
        sls, _, revs = _group_slices(g, group, n_chunks)
        zfs = [(fb_ref if r else ff_ref)[sl, :] for sl, r in zip(sls, revs)]
        vs = [v_ref[sl, :] for sl in sls]
        u_t, dec, _, _ = _gla_group(zfs, vs, None, [lbb if r else lbf for r in revs], revs)
        for i, rev in enumerate(revs):
            s_ref = sb_ref if rev else sf_ref
            s_ref[...] = s_ref[...] * dec[i] + u_t[i]
        return carry

    lax.fori_loop(0, n_chunks // group, body, 0)


def _col_spec(t, section):
    return pl.BlockSpec((None, t, HEAD_DIM), lambda b, h: (b, 0, section * N_HEADS + h))


def _head_row_spec(offset=0):
    return pl.BlockSpec((1, HEAD_DIM), lambda b, h: (0, offset + h))


_STATE_SPEC = pl.BlockSpec((None, None, HEAD_DIM, HEAD_DIM), lambda b, h: (b, h, 0, 0))


def _hgrn_states(pc, lbf, lbb):
    b, t, _ = pc.shape
    n_chunks = t // CHUNK
    shp = jax.ShapeDtypeStruct((b, N_HEADS, HEAD_DIM, HEAD_DIM), F32)
    return pl.pallas_call(
        functools.partial(_hgrn_state_kernel, n_chunks=n_chunks, group=_scan_group(n_chunks, HGRN_GROUP, False)),
        grid=(b, N_HEADS),
        in_specs=[_col_spec(t, 1), _col_spec(t, 2), _col_spec(t, 3), _head_row_spec(), _head_row_spec()],
        out_specs=[_STATE_SPEC, _STATE_SPEC],
        out_shape=[shp, shp],
        compiler_params=_cparams("parallel", "parallel"),
        name="hgrn_ctx",
    )(pc, pc, pc, lbf, lbb)


def _hgrn_out(px, lbf, lbb, nw, s0f, s0b):
    b, t, _ = px.shape
    n_chunks = t // CHUNK
    return pl.pallas_call(
        functools.partial(_hgrn_out_kernel, n_chunks=n_chunks, group=_scan_group(n_chunks, HGRN_GROUP, True)),
        grid=(b, N_HEADS),
        in_specs=[_col_spec(t, 0), _col_spec(t, 1), _col_spec(t, 2), _col_spec(t, 3), _col_spec(t, 4),
                  _head_row_spec(), _head_row_spec(),
                  pl.BlockSpec((1, HEAD_DIM), lambda b_, h: (0, 0)),
                  _STATE_SPEC, _STATE_SPEC],
        out_specs=pl.BlockSpec((None, t, HEAD_DIM), lambda b_, h: (b_, 0, h)),
        out_shape=jax.ShapeDtypeStruct((b, t, N_HEADS * HEAD_DIM), BF16),
        scratch_shapes=[pltpu.VMEM((t, HEAD_DIM), F32),
                        pltpu.VMEM((HEAD_DIM, HEAD_DIM), F32),
                        pltpu.VMEM((HEAD_DIM, HEAD_DIM), F32)],
        compiler_params=_cparams("parallel", "parallel"),
        name="hgrn_x",
    )(px, px, px, px, px, lbf, lbb, nw, s0f, s0b)


def _conv_silu(a, w, period):
    rows = a.shape[0]
    pos = lax.broadcasted_iota(jnp.int32, a.shape, 0) % period
    prev = jnp.where(pos == 0, 0.0, pltpu.roll(a, 1, axis=0))
    nxt = jnp.where(pos == period - 1, 0.0, pltpu.roll(a, rows - 1, axis=0))
    y = prev * w[0:1] + a * w[1:2] + nxt * w[2:3]
    return y * _sigmoid(y)


def _l2norm(x):
    return x * lax.rsqrt(jnp.sum(x * x, axis=-1, keepdims=True) + EPS)


def _softplus(x):
    return jnp.maximum(x, 0.0) + jnp.log1p(jnp.exp(-jnp.abs(x)))


def _unit_tri_inverses(tris):
    r = lax.broadcasted_iota(jnp.int32, (CHUNK, CHUNK), 0)
    c = lax.broadcasted_iota(jnp.int32, (CHUNK, CHUNK), 1)
    eye = jnp.where(r == c, 1.0, 0.0)
    ps = [_split2(-a) for a in tris]
    invs = [eye - a for a in tris]
    for _ in range(CHUNK.bit_length() - 2):
        ps = [_split2(_dot_split(p, p)) for p in ps]
        invs = [inv + _dot_split(_split2(inv), p) for inv, p in zip(invs, ps)]
    return invs


def _gdn_group(qs, ks, vs, a_cs, b_cs, a_rs, alogs, dtbs, revs):
    n = len(ks)
    dmask, ecum, beta, eend, dec, strict = [], [], [], [], [], []
    for i in range(n):
        incl, st = _chunk_masks(revs[i])
        incl_t = _chunk_masks(not revs[i])[0]
        scale = -jnp.exp(alogs[i])
        g_c = scale * _softplus(a_cs[i] + dtbs[i])
        g_r = scale * _softplus(a_rs[i] + dtbs[i])
        cum_c = jnp.sum(jnp.where(incl, g_r, 0.0), axis=1, keepdims=True)
        cum_r = jnp.sum(jnp.where(incl_t, g_c, 0.0), axis=0, keepdims=True)
        dmask.append(jnp.exp(jnp.where(incl, cum_c - cum_r, -jnp.inf)))
        ecum.append(jnp.exp(cum_c))
        beta.append(_sigmoid(b_cs[i]))
        last = cum_c[0:1] if revs[i] else cum_c[CHUNK - 1:CHUNK]
        eend.append(jnp.exp(last - cum_c))
        dec.append(jnp.exp(last))
        strict.append(st)
    kbf = [k.astype(BF16) for k in ks]
    kb = [ks[i] * beta[i] for i in range(n)]
    tris = [jnp.where(strict[i], _dot_nt(kb[i].astype(BF16), kbf[i]) * dmask[i], 0.0) for i in range(n)]
    tinv = [t.astype(BF16) for t in _unit_tri_inverses(tris)]
    u = [_dot(tinv[i], (vs[i] * beta[i]).astype(BF16)).astype(BF16) for i in range(n)]
    w = [_dot(tinv[i], (kb[i] * ecum[i]).astype(BF16)).astype(BF16) for i in range(n)]
    ke_t = [(ks[i] * eend[i]).T.astype(BF16) for i in range(n)]
    mp = [_dot(ke_t[i], w[i]) for i in range(n)]
    cc = [_dot(ke_t[i], u[i]) for i in range(n)]
    if qs is None:
        return mp, cc, dec, None, None
    attn = [(_dot_nt(qs[i].astype(BF16), kbf[i]) * dmask[i]).astype(BF16) for i in range(n)]
    qp = [qs[i] * ecum[i] - _dot(attn[i], w[i]) for i in range(n)]
    oi = [_dot(attn[i], u[i]) for i in range(n)]
    return mp, cc, dec, qp, oi


def _gdn_kernel(*refs, n_chunks, period, prep_rows, group, igroup, with_out):
    if with_out:
        (q_ref, k_ref, v_ref, z_ref, gc_ref, gr_ref, wq_ref, wk_ref, wv_ref, hp_ref, nw_ref,
         s0f_ref, s0b_ref, o_ref, qn, kn, vn, mp_s, cc_s, dc_s, qp_s, o_scr, sf, sb) = refs
    else:
        (k_ref, v_ref, gc_ref, gr_ref, wk_ref, wv_ref, hp_ref,
         sf, sb, kn, vn, mp_s, cc_s, dc_s) = refs
        q_ref = None

    t = n_chunks * CHUNK

    def prep(i, carry):
        sl = pl.ds(pl.multiple_of(i * prep_rows, prep_rows), prep_rows)
        kn[sl, :] = _l2norm(_conv_silu(k_ref[sl, :], wk_ref[...], period))
        vn[sl, :] = _conv_silu(v_ref[sl, :], wv_ref[...], period)
        if with_out:
            qn[sl, :] = _l2norm(_conv_silu(q_ref[sl, :], wq_ref[...], period)) * (HEAD_DIM ** -0.5)
        return carry

    lax.fori_loop(0, t // prep_rows, prep, 0)

    hp = hp_ref[...]

    def intra(g, carry):
        qs, ks, vs, a_cs, b_cs, a_rs, alogs, dtbs, revs, where = [], [], [], [], [], [], [], [], [], []
        for i in range(igroup):
            n = g * igroup + i
            sl = pl.ds(pl.multiple_of(n * CHUNK, CHUNK), CHUNK)
            gc = gc_ref[sl, :]
            gr = gr_ref[n]
            for d, rev in enumerate((False, True)):
                if with_out:
                    qs.append(qn[sl, :])
                ks.append(kn[sl, :])
                vs.append(vn[sl, :])
                a_cs.append(gc[:, d:d + 1])
                b_cs.append(gc[:, 2 + d:3 + d])
                a_rs.append(gr[d:d + 1, :])
                alogs.append(hp[2 * d:2 * d + 1, 0:1])
                dtbs.append(hp[2 * d + 1:2 * d + 2, 0:1])
                revs.append(rev)
                where.append((d, n, sl))
        mp, cc, dec, qp, oi = _gdn_group(qs if with_out else None, ks, vs, a_cs, b_cs, a_rs, alogs, dtbs, revs)
        for i, (d, n, sl) in enumerate(where):
            mp_s[d, n] = mp[i].astype(BF16)
            cc_s[d, n] = cc[i]
            dc_s[d, n] = jnp.broadcast_to(dec[i], (8, HEAD_DIM))
            if with_out:
                qp_s[d, sl, :] = qp[i].astype(BF16)
                if d == 1:
                    o_scr[sl, :] = oi[i - 1] + oi[i]
        return carry

    lax.fori_loop(0, n_chunks // igroup, intra, 0)

    if with_out:
        sf[...] = s0f_ref[...]
        sb[...] = s0b_ref[...]
        nw = nw_ref[...]
    else:
        sf[...] = jnp.zeros_like(sf)
        sb[...] = jnp.zeros_like(sb)

    def scan(g, carry, final):
        sls, idxs, revs = _group_slices(g, group, n_chunks)
        for sl, idx, rev in zip(sls, idxs, revs):
            d = int(rev)
            s_ref = sb if rev else sf
            s = s_ref[...]
            sbf = s.astype(BF16)
            s_ref[...] = s * dc_s[d, idx][0:1, :] - _dot(mp_s[d, idx], sbf) + cc_s[d, idx]
            if with_out:
                o = o_scr[sl, :] + _dot(qp_s[d, sl, :], sbf)
                if final:
                    o_ref[sl, :] = _gated_head_norm(o, z_ref[sl, :], nw).astype(o_ref.dtype)
                else:
                    o_scr[sl, :] = o
        return carry

    trips = n_chunks // group
    if with_out:
        lax.fori_loop(0, trips // 2, functools.partial(scan, final=False), 0)
        lax.fori_loop(trips // 2, trips, functools.partial(scan, final=True), 0)
    else:
        lax.fori_loop(0, trips, functools.partial(scan, final=False), 0)


def _gdn_call(p, gates, conv_w, hp, nw, s0f, s0b, period, with_out, sec0):
    b, t, _ = p.shape
    n_chunks = t // CHUNK
    prep_rows = max(period, min(t, 256))
    assert prep_rows % period == 0 and t % prep_rows == 0
    group = _scan_group(n_chunks, GDN_GROUP, with_out)
    g4 = gates[:, :, :4 * N_HEADS].reshape(b, t, 4, N_HEADS)
    g_col = g4.transpose(0, 3, 1, 2)
    g_row = jnp.pad(g4.transpose(0, 3, 2, 1), ((0, 0), (0, 0), (0, 4), (0, 0)))
    g_row = g_row.reshape(b, N_HEADS, 8, n_chunks, CHUNK).transpose(0, 1, 3, 2, 4)

    gc_spec = pl.BlockSpec((None, None, t, 4), lambda b_, h: (b_, h, 0, 0))
    gr_spec = pl.BlockSpec((None, None, n_chunks, 8, CHUNK), lambda b_, h: (b_, h, 0, 0, 0))
    conv_spec = lambda sec: pl.BlockSpec((3, HEAD_DIM), lambda b_, h: (0, sec * N_HEADS + h))
    hp_spec = pl.BlockSpec((None, 8, HEAD_DIM), lambda b_, h: (h, 0, 0))
    seq = lambda: pltpu.VMEM((t, HEAD_DIM), F32)
    mats = lambda dt: pltpu.VMEM((2, n_chunks, HEAD_DIM, HEAD_DIM), dt)
    dc = pltpu.VMEM((2, n_chunks, 8, HEAD_DIM), F32)
    state = lambda: pltpu.VMEM((HEAD_DIM, HEAD_DIM), F32)
    kern = functools.partial(_gdn_kernel, n_chunks=n_chunks, period=period, prep_rows=prep_rows,
                             group=group, igroup=_scan_group(n_chunks, GDN_INTRA_GROUP, False),
                             with_out=with_out)
    if with_out:
        return pl.pallas_call(
            kern,
            grid=(b, N_HEADS),
            in_specs=[_col_spec(t, sec0), _col_spec(t, sec0 + 1), _col_spec(t, sec0 + 2), _col_spec(t, sec0 + 3),
                      gc_spec, gr_spec, conv_spec(0), conv_spec(1), conv_spec(2), hp_spec,
                      pl.BlockSpec((1, HEAD_DIM), lambda b_, h: (0, 0)), _STATE_SPEC, _STATE_SPEC],
            out_specs=pl.BlockSpec((None, t, HEAD_DIM), lambda b_, h: (b_, 0, h)),
            out_shape=jax.ShapeDtypeStruct((b, t, N_HEADS * HEAD_DIM), BF16),
            scratch_shapes=[seq(), seq(), seq(), mats(BF16), mats(F32), dc,
                            pltpu.VMEM((2, t, HEAD_DIM), BF16), seq(), state(), state()],
            compiler_params=_cparams("parallel", "parallel"),
            name="gdn_x",
        )(p, p, p, p, g_col, g_row, conv_w, conv_w, conv_w, hp, nw, s0f, s0b)
    shp = jax.ShapeDtypeStruct((b, N_HEADS, HEAD_DIM, HEAD_DIM), F32)
    return pl.pallas_call(
        kern,
        grid=(b, N_HEADS),
        in_specs=[_col_spec(t, sec0 + 1), _col_spec(t, sec0 + 2), gc_spec, gr_spec,
                  conv_spec(1), conv_spec(2), hp_spec],
        out_specs=[_STATE_SPEC, _STATE_SPEC],
        out_shape=[shp, shp],
        scratch_shapes=[seq(), seq(), mats(BF16), mats(F32), dc],
        compiler_params=_cparams("parallel", "parallel"),
        name="gdn_ctx",
    )(p, p, g_col, g_row, conv_w, conv_w, hp)


def _outproj_kernel(mh_ref, mg_ref, x_ref, gt_ref, sc_ref, sh_ref, nw_ref, woh_ref, wog_ref,
                    wrh_ref, wrl_ref, br_ref, x1_ref, h2_ref, idx_ref, gate_ref):
    y = _dot(mh_ref[...], woh_ref[...]) + _dot(mg_ref[...], wog_ref[...])
    x1 = x_ref[...] + gt_ref[...] * y
    x1_ref[...] = x1
    h = _modulated_norm(x1, nw_ref[...], sc_ref[...], sh_ref[...])
    hh = h.astype(BF16)
    half = hh.shape[1] // 2
    hi_bits = lax.bitcast_convert_type(hh[:, :half].astype(F32), jnp.uint32)
    lo_bits = lax.bitcast_convert_type(hh[:, half:].astype(F32), jnp.uint32)
    h2_ref[...] = hi_bits | (lo_bits >> 16)
    hl = (h - hh.astype(F32)).astype(BF16)
    wrh = wrh_ref[...]
    logits = _dot(hh, wrh) + _dot(hl, wrh) + _dot(hh, wrl_ref[...]) + br_ref[...]
    lane = lax.broadcasted_iota(jnp.int32, logits.shape, 1).astype(F32)
    vals, idxs = [], []
    for _ in range(TOP_K):
        m = jnp.max(logits, axis=-1, keepdims=True)
        i = jnp.min(jnp.where(logits == m, lane, float(LANES)), axis=-1, keepdims=True)
        vals.append(m)
        idxs.append(i)
        logits = jnp.where(lane == i, -jnp.inf, logits)
    es = [jnp.exp(v - vals[0]) for v in vals]
    inv = 1.0 / functools.reduce(lambda a, b_: a + b_, es)
    idx_out = jnp.zeros(lane.shape, F32)
    gate_out = jnp.zeros(lane.shape, F32)
    for k in range(TOP_K):
        idx_out = jnp.where(lane == k, idxs[k], idx_out)
        gate_out = jnp.where(lane == k, es[k] * inv, gate_out)
    idx_ref[...] = idx_out.astype(jnp.int32)
    gate_ref[...] = gate_out


def _outproj(mix_h, mix_g, x, gt1, sc2, sh2, nw, wo_h, wo_g, wr_hi, wr_lo, br):
    b, t, d = x.shape
    w = mix_h.shape[-1]
    tm = min(t, 512)
    tpb = t // tm
    row = lambda width: pl.BlockSpec((None, tm, width), lambda i: (i // tpb, i % tpb, 0))
    per_b = pl.BlockSpec((None, 1, d), lambda i: (i // tpb, 0, 0))
    const = lambda r, c: pl.BlockSpec((r, c), lambda i: (0, 0))
    return pl.pallas_call(
        _outproj_kernel,
        grid=(b * tpb,),
        in_specs=[row(w), row(w), row(d), per_b, per_b, per_b, const(1, d), const(w, d), const(w, d),
                  const(d, LANES), const(d, LANES), const(1, LANES)],
        out_specs=[row(d), row(d // 2), row(LANES), row(LANES)],
        out_shape=[jax.ShapeDtypeStruct((b, t, d), F32), jax.ShapeDtypeStruct((b, t, d // 2), jnp.uint32),
                   jax.ShapeDtypeStruct((b, t, LANES), jnp.int32), jax.ShapeDtypeStruct((b, t, LANES), F32)],
        compiler_params=_cparams("parallel"),
        name="outproj",
    )(mix_h, mix_g, x, gt1, sc2, sh2, nw.reshape(1, d), wo_h, wo_g, wr_hi, wr_lo, br)


def _dispatch_kernel(dest_ref, h_ref, xs_init_ref, xs_ref, sem, *, tt, k):
    del xs_init_ref
    base = pl.program_id(0) * (tt * k)

    def body(t, carry):
        for kk in range(k):
            row = dest_ref[base + t * k + kk]
            pltpu.make_async_copy(h_ref.at[pl.ds(t, 1), :], xs_ref.at[pl.ds(row, 1), :], sem).start()
        return carry

    lax.fori_loop(0, tt, body, 0)
    done = xs_ref.at[pl.ds(0, tt * k), :]
    pltpu.make_async_copy(done, done, sem).wait()


def _dispatch(h2, dest, rows, k):
    n_tok, w = h2.shape
    tt = min(n_tok, 512)
    return pl.pallas_call(
        functools.partial(_dispatch_kernel, tt=tt, k=k),
        grid_spec=pltpu.PrefetchScalarGridSpec(
            num_scalar_prefetch=1,
            grid=(n_tok // tt,),
            in_specs=[pl.BlockSpec((tt, w), lambda i, dest_: (i, 0)),
                      pl.BlockSpec(memory_space=pl.ANY)],
            out_specs=pl.BlockSpec(memory_space=pl.ANY),
            scratch_shapes=[pltpu.SemaphoreType.DMA(())]),
        out_shape=jax.ShapeDtypeStruct((rows, w), h2.dtype),
        input_output_aliases={2: 0},
        compiler_params=_cparams("arbitrary"),
        name="dispatch",
    )(dest, h2, jnp.zeros((rows, w), h2.dtype))


def _cast_tile(src, dst):
    rows = 256

    def body(i, carry):
        sl = pl.ds(pl.multiple_of(i * rows, rows), rows)
        dst[sl, :] = src[sl, :].astype(dst.dtype)
        return carry

    lax.fori_loop(0, src.shape[0] // rows, body, 0)


def _moe_up_kernel(e_ref, j_ref, r_ref, ro_ref, jo_ref, first_ref, valid_ref,
                   x_ref, wg_ref, wu_ref, bg_ref, bu_ref, act_ref, wg_b, wu_b):
    s = pl.program_id(0)

    @pl.when(first_ref[s] == 1)
    def _():
        _cast_tile(wg_ref, wg_b)
        _cast_tile(wu_ref, wu_b)

    @pl.when(valid_ref[s] == 1)
    def _():
        xp = x_ref[...]
        half = xp.shape[1]
        xa = lax.bitcast_convert_type(xp & jnp.uint32(0xFFFF0000), F32).astype(BF16)
        xb = lax.bitcast_convert_type(xp << 16, F32).astype(BF16)
        gate = _dot(xa, wg_b[:half, :]) + _dot(xb, wg_b[half:, :]) + bg_ref[...]
        up = _dot(xa, wu_b[:half, :]) + _dot(xb, wu_b[half:, :]) + bu_ref[...]
        gate = jnp.minimum(gate, SWIGLU_LIMIT)
        up = jnp.clip(up, -SWIGLU_LIMIT, SWIGLU_LIMIT)
        act_ref[...] = ((up + 1.0) * gate * _sigmoid(SWIGLU_ALPHA * gate)).astype(act_ref.dtype)

    @pl.when(valid_ref[s] == 0)
    def _():
        act_ref[...] = jnp.zeros_like(act_ref)


def _moe_down_kernel(e_ref, j_ref, r_ref, ro_ref, jo_ref, first_ref, valid_ref,
                     a_ref, wd_ref, bd_ref, y_ref, wd_b):
    s = pl.program_id(0)

    @pl.when(first_ref[s] == 1)
    def _():
        _cast_tile(wd_ref, wd_b)

    @pl.when(valid_ref[s] == 1)
    def _():
        y_ref[...] = _dot(a_ref[...], wd_b[...]) + bd_ref[...]

    @pl.when(valid_ref[s] == 0)
    def _():
        y_ref[...] = jnp.zeros_like(y_ref)


def _moe_experts(xs, sched, wg, wu, wd, bg, bu, bd, tm, tf):
    rows = xs.shape[0]
    n_exp, d, f = wg.shape
    assert xs.shape[1] * 2 == d and d // tf == f // tf
    steps = sched[0].shape[0]
    w_tile = lambda shape: pl.BlockSpec((None,) + shape, lambda s, e, j, r, ro, jo, fi, va: (e[s], 0, j[s]))
    act = pl.pallas_call(
        _moe_up_kernel,
        grid_spec=pltpu.PrefetchScalarGridSpec(
            num_scalar_prefetch=7,
            grid=(steps,),
            in_specs=[pl.BlockSpec((tm, d // 2), lambda s, e, j, r, ro, jo, fi, va: (r[s], 0)),
                      w_tile((d, tf)), w_tile((d, tf)), w_tile((1, tf)), w_tile((1, tf))],
            out_specs=pl.BlockSpec((tm, tf), lambda s, e, j, r, ro, jo, fi, va: (ro[s], jo[s])),
            scratch_shapes=[pltpu.VMEM((d, tf), BF16), pltpu.VMEM((d, tf), BF16)]),
        out_shape=jax.ShapeDtypeStruct((rows, f), BF16),
        compiler_params=_cparams("arbitrary"),
        name="moe_up",
    )(*sched, xs, wg, wu, bg.reshape(n_exp, 1, f), bu.reshape(n_exp, 1, f))
    return pl.pallas_call(
        _moe_down_kernel,
        grid_spec=pltpu.PrefetchScalarGridSpec(
            num_scalar_prefetch=7,
            grid=(steps,),
            in_specs=[pl.BlockSpec((tm, f), lambda s, e, j, r, ro, jo, fi, va: (r[s], 0)),
                      w_tile((f, tf)), w_tile((1, tf))],
            out_specs=pl.BlockSpec((tm, tf), lambda s, e, j, r, ro, jo, fi, va: (ro[s], jo[s])),
            scratch_shapes=[pltpu.VMEM((f, tf), BF16)]),
        out_shape=jax.ShapeDtypeStruct((rows, d), F32),
        compiler_params=_cparams("arbitrary"),
        name="moe_down",
    )(*sched, act, wd, bd.reshape(n_exp, 1, d))


def _route(top_idx, n_exp, tm, n_tiles):
    n_tok, k = top_idx.shape
    m = n_tok * k
    e_flat = top_idx.reshape(m)
    onehot = (e_flat[:, None] == jnp.arange(n_exp, dtype=jnp.int32)[None, :]).astype(jnp.int32)
    csum = jnp.cumsum(onehot, axis=0)
    counts = csum[-1]
    rank = jnp.take_along_axis(csum, e_flat[:, None], axis=1)[:, 0] - 1
    padded = (counts + tm - 1) // tm * tm
    pend = jnp.cumsum(padded)
    dest = (pend - padded)[e_flat] + rank
    n_blocks = -(-m // tm) + n_exp

    nb = padded // tm
    blk0 = (pend - padded) // tm
    cum = jnp.cumsum(nb * n_tiles)
    total = cum[-1]
    s = jnp.arange(n_tiles * n_blocks, dtype=jnp.int32)
    valid = s < total
    sc = jnp.minimum(s, total - 1)
    e = jnp.minimum(jnp.searchsorted(cum, sc, side='right'), n_exp - 1).astype(jnp.int32)
    local = sc - (cum[e] - nb[e] * n_tiles)
    nbe = jnp.maximum(nb[e], 1)
    j = local // nbe
    r = blk0[e] + local % nbe
    first = valid & (local % nbe == 0)
    extra = s - total
    r_out = jnp.where(valid, r, total // n_tiles + extra // n_tiles)
    j_out = jnp.where(valid, j, extra % n_tiles)
    sched = tuple(a.astype(jnp.int32) for a in (e, j, r, r_out, j_out, first, valid))
    return dest.astype(jnp.int32), n_blocks * tm, sched


def _final_kernel(x1_ref, yg_ref, gate_ref, gt_ref, nw_ref, o_ref):
    g = gate_ref[...]
    moe = yg_ref[0] * g[:, 0:1]
    for k in range(1, TOP_K):
        moe = moe + yg_ref[k] * g[:, k:k + 1]
    x = x1_ref[...] + gt_ref[...] * moe
    ms = jnp.mean(x * x, axis=-1, keepdims=True)
    o_ref[...] = x * lax.rsqrt(ms + EPS) * nw_ref[...]


def _final(x1, yg, gates, gt2, nw):
    b, t, d = x1.shape
    tm = min(t, 256)
    tpb = t // tm
    row = lambda width: pl.BlockSpec((None, tm, width), lambda i: (i // tpb, i % tpb, 0))
    return pl.pallas_call(
        _final_kernel,
        grid=(b * tpb,),
        in_specs=[row(d),
                  pl.BlockSpec((TOP_K, None, tm, d), lambda i: (0, i // tpb, i % tpb, 0)),
                  row(LANES),
                  pl.BlockSpec((None, 1, d), lambda i: (i // tpb, 0, 0)),
                  pl.BlockSpec((1, d), lambda i: (0, 0))],
        out_specs=row(d),
        out_shape=jax.ShapeDtypeStruct((b, t, d), F32),
        compiler_params=_cparams("parallel"),
        name="final",
    )(x1, yg, gates, gt2, nw.reshape(1, d))


def kernel(x, c, ctx, c_ctx, w_ada, b_ada, norm_mix_w, w_in, hg_lb_f, hg_lb_b, hg_norm_w, gd_conv_w,
           gd_a_log_f, gd_a_log_b, gd_dt_bias_f, gd_dt_bias_b, gd_norm_w, w_out, norm_ffn_w, w_router,
           b_router, w_gate, b_gate, w_up, b_up, w_down, b_down, norm_out_w):
    bsz, t, d = x.shape
    t_ctx = ctx.shape[1]
    n_exp = w_router.shape[-1]
    l = 0
    hg_w = N_HEADS * HEAD_DIM
    n_main = 9 * hg_w

    lb_f = jnp.cumsum(jax.nn.softmax(hg_lb_f.astype(F32), axis=0), axis=0)[l].reshape(1, hg_w)
    lb_b = jnp.cumsum(jax.nn.softmax(hg_lb_b.astype(F32), axis=0), axis=0)[l].reshape(1, hg_w)

    rows = -(-(bsz + 1) // 8) * 8
    cc = jnp.zeros((rows, d), F32).at[:bsz].set(c).at[bsz].set(c_ctx)
    mod = _ada(cc, w_ada[l], b_ada[l])
    sh1, sc1, gt1, sh2, sc2, gt2 = (mod[:bsz, i * d:(i + 1) * d].reshape(bsz, 1, d) for i in range(6))
    csh1 = jnp.broadcast_to(mod[bsz, 0:d].reshape(1, 1, d), (bsz, 1, d))
    csc1 = jnp.broadcast_to(mod[bsz, d:2 * d].reshape(1, 1, d), (bsz, 1, d))

    w_in_b = w_in[l].astype(BF16)
    w_main = w_in_b[:, :n_main]
    w_gates = jnp.pad(w_in_b[:, n_main:], ((0, 0), (0, LANES - (w_in_b.shape[1] - n_main))))
    px, gx = _inproj(x, norm_mix_w[l], sc1, sh1, w_main, w_gates)
    pc, gc = _inproj(ctx, norm_mix_w[l], csc1, csh1, w_main, w_gates)

    hg_nw = hg_norm_w[l].reshape(1, HEAD_DIM)
    hs_f, hs_b = _hgrn_states(pc, lb_f, lb_b)
    mix_h = _hgrn_out(px, lb_f, lb_b, hg_nw, hs_f, hs_b)

    hp = jnp.stack([gd_a_log_f[l], gd_dt_bias_f[l], gd_a_log_b[l], gd_dt_bias_b[l]], axis=1)
    hp = jnp.broadcast_to(jnp.pad(hp, ((0, 0), (0, 4)))[:, :, None], (N_HEADS, 8, HEAD_DIM)).astype(F32)
    gd_nw = gd_norm_w[l].reshape(1, HEAD_DIM)
    gs_f, gs_b = _gdn_call(pc, gc, gd_conv_w[l], hp, gd_nw, None, None, t_ctx, False, 5)
    mix_g = _gdn_call(px, gx, gd_conv_w[l], hp, gd_nw, gs_f, gs_b, GRID_W, True, 5)

    w_out_b = w_out[l].astype(BF16)
    wr = jnp.pad(w_router[l], ((0, 0), (0, LANES - n_exp)))
    wr_hi = wr.astype(BF16)
    wr_lo = (wr - wr_hi.astype(F32)).astype(BF16)
    br = jnp.full((1, LANES), NEG_BIG, F32).at[0, :n_exp].set(b_router[l])
    x1, h2, idx_pad, gate_pad = _outproj(mix_h, mix_g, x, gt1, sc2, sh2, norm_ffn_w[l],
                                         w_out_b[:hg_w], w_out_b[hg_w:], wr_hi, wr_lo, br)

    n_tok = bsz * t
    top_idx = idx_pad.reshape(n_tok, LANES)[:, :TOP_K]
    dest, n_rows, sched = _route(top_idx, n_exp, MOE_TM, w_gate.shape[-1] // MOE_TF)
    xs = _dispatch(h2.reshape(n_tok, d // 2), dest, n_rows, TOP_K)
    ys = _moe_experts(xs, sched, w_gate[l], w_up[l], w_down[l], b_gate[l], b_up[l], b_down[l], MOE_TM, MOE_TF)

    yg = ys[dest.reshape(n_tok, TOP_K).T].reshape(TOP_K, bsz, t, d)
    return _final(x1, yg, gate_pad, gt2, norm_out_w)
```

```python
import functools

import jax
import jax.numpy as jnp
from jax import lax
from jax.experimental import pallas as pl
from jax.experimental.pallas import tpu as pltpu

F32 = jnp.float32
BF16 = jnp.bfloat16

EPS = 1e-6
CHUNK = 64
HEAD_DIM = 128
N_HEADS = 8
GRID_W = 64
TOP_K = 4
SWIGLU_LIMIT = 7.0
SWIGLU_ALPHA = 1.702
LANES = 128
NEG_BIG = -1e30

HGRN_GROUP = 4
GDN_GROUP = 4
GDN_INTRA_GROUP = 8
MOE_TM = 512
MOE_TF = 1024
VMEM_LIMIT = 56 * 1024 * 1024


def _cparams(*sem):
    return pltpu.CompilerParams(dimension_semantics=sem, vmem_limit_bytes=VMEM_LIMIT)


def _dot(a, b):
    return jnp.dot(a, b, preferred_element_type=F32)


def _dot_nt(a, b):
    return lax.dot_general(a, b, (((1,), (1,)), ((), ())), preferred_element_type=F32)


def _split2(x):
    hi = x.astype(BF16)
    return hi, (x - hi.astype(F32)).astype(BF16)


def _split3(x):
    hi = x.astype(BF16)
    rest = x - hi.astype(F32)
    mid = rest.astype(BF16)
    return hi, mid, (rest - mid.astype(F32)).astype(BF16)


def _dot_split(a, b):
    ah, al = a
    bh, bl = b
    return _dot(ah, bh) + (_dot(ah, bl) + _dot(al, bh))


def _sigmoid(x):
    return 1.0 / (1.0 + jnp.exp(-x))


def _ada_kernel(c_ref, w_ref, b_ref, o_ref):
    c = c_ref[...]
    s = (c * _sigmoid(c)).astype(BF16)
    o_ref[...] = _dot(s, w_ref[...].astype(BF16)) + b_ref[...]


def _ada(cc, w_ada, b_ada):
    rows, d = cc.shape
    n = w_ada.shape[1]
    tn = 1024
    return pl.pallas_call(
        _ada_kernel,
        grid=(n // tn,),
        in_specs=[pl.BlockSpec((rows, d), lambda j: (0, 0)),
                  pl.BlockSpec((d, tn), lambda j: (0, j)),
                  pl.BlockSpec((1, tn), lambda j: (0, j))],
        out_specs=pl.BlockSpec((rows, tn), lambda j: (0, j)),
        out_shape=jax.ShapeDtypeStruct((rows, n), F32),
        compiler_params=_cparams("arbitrary"),
        name="ada",
    )(cc, w_ada, b_ada.reshape(1, n))


def _modulated_norm(x, nw, sc, sh):
    ms = jnp.mean(x * x, axis=-1, keepdims=True)
    return (x * lax.rsqrt(ms + EPS) * nw) * (1.0 + sc) + sh


def _inproj_kernel(x_ref, nw_ref, sc_ref, sh_ref, w_ref, wg_ref, o_ref, og_ref, h_scr):
    @pl.when(pl.program_id(1) == 0)
    def _():
        h = _modulated_norm(x_ref[...], nw_ref[...], sc_ref[...], sh_ref[...]).astype(BF16)
        h_scr[...] = h
        og_ref[...] = _dot(h, wg_ref[...])

    o_ref[...] = _dot(h_scr[...], w_ref[...])


def _inproj(x, nw, sc, sh, w_main, w_gates):
    b, t, d = x.shape
    n = w_main.shape[1]
    tm = min(t, 1024)
    tn = 1024
    tpb = t // tm
    return pl.pallas_call(
        _inproj_kernel,
        grid=(b * tpb, n // tn),
        in_specs=[pl.BlockSpec((None, tm, d), lambda i, j: (i // tpb, i % tpb, 0)),
                  pl.BlockSpec((1, d), lambda i, j: (0, 0)),
                  pl.BlockSpec((None, 1, d), lambda i, j: (i // tpb, 0, 0)),
                  pl.BlockSpec((None, 1, d), lambda i, j: (i // tpb, 0, 0)),
                  pl.BlockSpec((d, tn), lambda i, j: (0, j)),
                  pl.BlockSpec((d, LANES), lambda i, j: (0, 0))],
        out_specs=[pl.BlockSpec((None, tm, tn), lambda i, j: (i // tpb, i % tpb, j)),
                   pl.BlockSpec((None, tm, LANES), lambda i, j: (i // tpb, i % tpb, 0))],
        out_shape=[jax.ShapeDtypeStruct((b, t, n), F32),
                   jax.ShapeDtypeStruct((b, t, LANES), F32)],
        scratch_shapes=[pltpu.VMEM((tm, d), BF16)],
        compiler_params=_cparams("parallel", "arbitrary"),
        name="inproj",
    )(x, nw.reshape(1, d), sc, sh, w_main, w_gates)


def _chunk_masks(rev, wide=False):
    shape = (CHUNK, 2 * CHUNK if wide else CHUNK)
    r = lax.broadcasted_iota(jnp.int32, shape, 0)
    c = lax.broadcasted_iota(jnp.int32, shape, 1) & (CHUNK - 1)
    incl = (c >= r) if rev else (c <= r)
    strict = (c > r) if rev else (c < r)
    return incl, strict


def _gated_head_norm(o, z, nw):
    ms = jnp.mean(o * o, axis=-1, keepdims=True)
    return (o * lax.rsqrt(ms + EPS) * nw) * (z * _sigmoid(z))


def _group_slices(g, group, n_chunks):
    idxs, revs = [], []
    for i in range(group):
        n = g * group + i
        idxs += [n, n_chunks - 1 - n]
        revs += [False, True]
    return [pl.ds(pl.multiple_of(ix * CHUNK, CHUNK), CHUNK) for ix in idxs], idxs, revs


def _scan_group(n_chunks, want, two_phase):
    g = want
    while g > 1 and (n_chunks % g or (two_phase and (n_chunks // g) % 2)):
        g //= 2
    assert n_chunks % g == 0 and not (two_phase and (n_chunks // g) % 2)
    return g


def _gla_group(zfs, vs, qs, lbs, revs):
    n = len(zfs)
    incl = [_chunk_masks(r)[0] for r in revs]
    logf, k = [], []
    for i in range(n):
        one_m = 1.0 - lbs[i]
        logf.append(jnp.log(lbs[i] + one_m * _sigmoid(zfs[i])))
        k.append(one_m * _sigmoid(-zfs[i]))
    cum = []
    for i in range(n):
        tri = incl[i].astype(BF16)
        hi, mid, lo = _split3(logf[i])
        parts = _dot(tri, jnp.concatenate([hi, mid, lo], axis=1))
        cum.append(parts[:, :HEAD_DIM] + (parts[:, HEAD_DIM:2 * HEAD_DIM] + parts[:, 2 * HEAD_DIM:]))
    last = [cum[i][0:1] if revs[i] else cum[i][CHUNK - 1:CHUNK] for i in range(n)]
    u_t = [_dot(vs[i].T.astype(BF16), (k[i] * jnp.exp(last[i] - cum[i])).astype(BF16)) for i in range(n)]
    dec = [jnp.exp(x) for x in last]
    if qs is None:
        return u_t, dec, None, None
    scores = []
    for i in range(n):
        mid_row = CHUNK // 2 - 1 if revs[i] else CHUNK // 2
        ref = cum[i][mid_row:mid_row + 1]
        sc = _dot_nt((qs[i] * jnp.exp(cum[i] - ref)).astype(BF16), (k[i] * jnp.exp(ref - cum[i])).astype(BF16))
        scores.append(jnp.where(incl[i], sc, 0.0).astype(BF16))
    o_intra = [_dot(scores[i], vs[i].astype(BF16)) for i in range(n)]
    qe = [(qs[i] * jnp.exp(cum[i])).astype(BF16) for i in range(n)]
    return u_t, dec, o_intra, qe


def _hgrn_out_kernel(q_ref, ff_ref, fb_ref, v_ref, g_ref, lbf_ref, lbb_ref, nw_ref, s0f_ref, s0b_ref,
                     o_ref, o_scr, sf_scr, sb_scr, *, n_chunks, group):
    sf_scr[...] = s0f_ref[...]
    sb_scr[...] = s0b_ref[...]
    lbf, lbb, nw = lbf_ref[...], lbb_ref[...], nw_ref[...]

    def body(g, carry, final):
        sls, _, revs = _group_slices(g, group, n_chunks)
        zfs = [(fb_ref if r else ff_ref)[sl, :] for sl, r in zip(sls, revs)]
        vs = [v_ref[sl, :] for sl in sls]
        qs = [q_ref[sl, :] for sl in sls]
        u_t, dec, o_intra, qe = _gla_group(zfs, vs, qs, [lbb if r else lbf for r in revs], revs)
        for i, (sl, rev) in enumerate(zip(sls, revs)):
            s_scr = sb_scr if rev else sf_scr
            st = s_scr[...]
            o = o_intra[i] + _dot_nt(qe[i], st.astype(BF16))
            s_scr[...] = st * dec[i] + u_t[i]
            if final:
                o_ref[sl, :] = _gated_head_norm(o_scr[sl, :] + o, g_ref[sl, :], nw).astype(o_ref.dtype)
            else:
                o_scr[sl, :] = o
        return carry

    trips = n_chunks // group
    lax.fori_loop(0, trips // 2, functools.partial(body, final=False), 0)
    lax.fori_loop(trips // 2, trips, functools.partial(body, final=True), 0)


def _hgrn_state_kernel(ff_ref, fb_ref, v_ref, lbf_ref, lbb_ref, sf_ref, sb_ref, *, n_chunks, group):
    sf_ref[...] = jnp.zeros_like(sf_ref)
    sb_ref[...] = jnp.zeros_like(sb_ref)
    lbf, lbb = lbf_ref[...], lbb_ref[...]

    def body(g, carry):
        sls, _, revs = _group_slices(g, group, n_chunks)
        zfs = [(fb_ref if r else ff_ref)[sl, :] for sl, r in zip(sls, revs)]
        vs = [v_ref[sl, :] for sl in sls]
        u_t, dec, _, _ = _gla_group(zfs, vs, None, [lbb if r else lbf for r in revs], revs)
        for i, rev in enumerate(revs):
            s_ref = sb_ref if rev else sf_ref
            s_ref[...] = s_ref[...] * dec[i] + u_t[i]
        return carry

    lax.fori_loop(0, n_chunks // group, body, 0)


def _col_spec(t, section):
    return pl.BlockSpec((None, t, HEAD_DIM), lambda b, h: (b, 0, section * N_HEADS + h))


def _head_row_spec(offset=0):
    return pl.BlockSpec((1, HEAD_DIM), lambda b, h: (0, offset + h))


_STATE_SPEC = pl.BlockSpec((None, None, HEAD_DIM, HEAD_DIM), lambda b, h: (b, h, 0, 0))


def _hgrn_states(pc, lbf, lbb):
    b, t, _ = pc.shape
    n_chunks = t // CHUNK
    shp = jax.ShapeDtypeStruct((b, N_HEADS, HEAD_DIM, HEAD_DIM), F32)
    return pl.pallas_call(
        functools.partial(_hgrn_state_kernel, n_chunks=n_chunks, group=_scan_group(n_chunks, HGRN_GROUP, False)),
        grid=(b, N_HEADS),
        in_specs=[_col_spec(t, 1), _col_spec(t, 2), _col_spec(t, 3), _head_row_spec(), _head_row_spec()],
        out_specs=[_STATE_SPEC, _STATE_SPEC],
        out_shape=[shp, shp],
        compiler_params=_cparams("parallel", "parallel"),
        name="hgrn_ctx",
    )(pc, pc, pc, lbf, lbb)


def _hgrn_out(px, lbf, lbb, nw, s0f, s0b):
    b, t, _ = px.shape
    n_chunks = t // CHUNK
    return pl.pallas_call(
        functools.partial(_hgrn_out_kernel, n_chunks=n_chunks, group=_scan_group(n_chunks, HGRN_GROUP, True)),
        grid=(b, N_HEADS),
        in_specs=[_col_spec(t, 0), _col_spec(t, 1), _col_spec(t, 2), _col_spec(t, 3), _col_spec(t, 4),
                  _head_row_spec(), _head_row_spec(),
                  pl.BlockSpec((1, HEAD_DIM), lambda b_, h: (0, 0)),
                  _STATE_SPEC, _STATE_SPEC],
        out_specs=pl.BlockSpec((None, t, HEAD_DIM), lambda b_, h: (b_, 0, h)),
        out_shape=jax.ShapeDtypeStruct((b, t, N_HEADS * HEAD_DIM), BF16),
        scratch_shapes=[pltpu.VMEM((t, HEAD_DIM), F32),
                        pltpu.VMEM((HEAD_DIM, HEAD_DIM), F32),
                        pltpu.VMEM((HEAD_DIM, HEAD_DIM), F32)],
        compiler_params=_cparams("parallel", "parallel"),
        name="hgrn_x",
    )(px, px, px, px, px, lbf, lbb, nw, s0f, s0b)


def _conv_silu(a, w, period):
    rows = a.shape[0]
    pos = lax.broadcasted_iota(jnp.int32, a.shape, 0) % period
    prev = jnp.where(pos == 0, 0.0, pltpu.roll(a, 1, axis=0))
    nxt = jnp.where(pos == period - 1, 0.0, pltpu.roll(a, rows - 1, axis=0))
    y = prev * w[0:1] + a * w[1:2] + nxt * w[2:3]
    return y * _sigmoid(y)


def _l2norm(x):
    return x * lax.rsqrt(jnp.sum(x * x, axis=-1, keepdims=True) + EPS)


def _softplus(x):
    return jnp.maximum(x, 0.0) + jnp.log1p(jnp.exp(-jnp.abs(x)))


def _unit_tri_inverses(tris):
    row = lax.broadcasted_iota(jnp.int32, (CHUNK, 2 * CHUNK), 0)
    lane = lax.broadcasted_iota(jnp.int32, (CHUNK, 2 * CHUNK), 1)
    left = lane < CHUNK
    eye = jnp.where((lane & (CHUNK - 1)) == row, 1.0, 0.0)
    zeros = jnp.zeros((CHUNK, 2 * CHUNK), BF16)

    def lhs(hl):
        return jnp.where(left, hl[0], hl[1])

    def rhs(hl):
        top = jnp.concatenate([hl[0], hl[1]], axis=1)
        return jnp.concatenate([top, jnp.concatenate([hl[0], zeros], axis=1)], axis=0)

    def fold(o):
        return o[:, :2 * CHUNK] + o[:, 2 * CHUNK:]

    ps = [_split2(-a) for a in tris]
    invs = [eye - a for a in tris]
    ps = [_split2(fold(_dot(lhs(p), rhs(p)))) for p in ps]
    for _ in range(CHUNK.bit_length() - 3):
        outs = [fold(_dot(jnp.concatenate([lhs(p), lhs(_split2(inv))], axis=0), rhs(p)))
                for p, inv in zip(ps, invs)]
        invs = [inv + o[CHUNK:] for inv, o in zip(invs, outs)]
        ps = [_split2(o[:CHUNK]) for o in outs]
    invs = [inv + fold(_dot(lhs(_split2(inv)), rhs(p))) for inv, p in zip(invs, ps)]
    return [inv[:, :CHUNK] for inv in invs]


def _gdn_group(qs, ks, vs, a_cs, b_cs, a_rs, alogs, dtbs, revs):
    n = len(ks)
    left = lax.broadcasted_iota(jnp.int32, (CHUNK, 2 * CHUNK), 1) < CHUNK
    dmask, ecum, beta, eend, dec, strict = [], [], [], [], [], []
    for i in range(n):
        incl, st = _chunk_masks(revs[i], wide=True)
        incl_t = _chunk_masks(not revs[i], wide=True)[0]
        scale = -jnp.exp(alogs[i])
        g_c = scale * _softplus(a_cs[i] + dtbs[i])
        g_r = scale * _softplus(a_rs[i] + dtbs[i])
        cum_c = jnp.sum(jnp.where(incl & left, g_r, 0.0), axis=1, keepdims=True)
        cum_r = jnp.sum(jnp.where(incl_t, g_c, 0.0), axis=0, keepdims=True)
        dmask.append(jnp.exp(jnp.where(incl, cum_c - cum_r, -jnp.inf)))
        ecum.append(jnp.exp(cum_c))
        beta.append(_sigmoid(b_cs[i]))
        last = cum_c[0:1] if revs[i] else cum_c[CHUNK - 1:CHUNK]
        eend.append(jnp.exp(last - cum_c))
        dec.append(jnp.exp(last))
        strict.append(st)
    kbf = [k.astype(BF16) for k in ks]
    kb = [ks[i] * beta[i] for i in range(n)]
    kk = [_dot_nt(kb[i].astype(BF16), jnp.concatenate([kbf[i], kbf[i]], axis=0)) for i in range(n)]
    tris = [jnp.where(strict[i], kk[i] * dmask[i], 0.0) for i in range(n)]
    tinv = [t.astype(BF16) for t in _unit_tri_inverses(tris)]
    dmask = [m[:, :CHUNK] for m in dmask]
    u = [_dot(tinv[i], (vs[i] * beta[i]).astype(BF16)).astype(BF16) for i in range(n)]
    w = [_dot(tinv[i], (kb[i] * ecum[i]).astype(BF16)).astype(BF16) for i in range(n)]
    ke_t = [(ks[i] * eend[i]).T.astype(BF16) for i in range(n)]
    mp = [_dot(ke_t[i], w[i]) for i in range(n)]
    cc = [_dot(ke_t[i], u[i]) for i in range(n)]
    if qs is None:
        return mp, cc, dec, None, None
    attn = [(_dot_nt(qs[i].astype(BF16), kbf[i]) * dmask[i]).astype(BF16) for i in range(n)]
    qp = [qs[i] * ecum[i] - _dot(attn[i], w[i]) for i in range(n)]
    oi = [_dot(attn[i], u[i]) for i in range(n)]
    return mp, cc, dec, qp, oi


def _gdn_kernel(*refs, n_chunks, period, prep_rows, group, igroup, with_out):
    if with_out:
        (q_ref, k_ref, v_ref, z_ref, gc_ref, gr_ref, wq_ref, wk_ref, wv_ref, hp_ref, nw_ref,
         s0f_ref, s0b_ref, o_ref, qn, kn, vn, mp_s, cc_s, dc_s, qp_s, o_scr, sf, sb) = refs
    else:
        (k_ref, v_ref, gc_ref, gr_ref, wk_ref, wv_ref, hp_ref,
         sf, sb, kn, vn, mp_s, cc_s, dc_s) = refs
        q_ref = None

    t = n_chunks * CHUNK

    def prep(i, carry):
        sl = pl.ds(pl.multiple_of(i * prep_rows, prep_rows), prep_rows)
        kn[sl, :] = _l2norm(_conv_silu(k_ref[sl, :], wk_ref[...], period))
        vn[sl, :] = _conv_silu(v_ref[sl, :], wv_ref[...], period)
        if with_out:
            qn[sl, :] = _l2norm(_conv_silu(q_ref[sl, :], wq_ref[...], period)) * (HEAD_DIM ** -0.5)
        return carry

    lax.fori_loop(0, t // prep_rows, prep, 0)

    hp = hp_ref[...]

    def intra(g, carry):
        qs, ks, vs, a_cs, b_cs, a_rs, alogs, dtbs, revs, where = [], [], [], [], [], [], [], [], [], []
        for i in range(igroup):
            n = g * igroup + i
            sl = pl.ds(pl.multiple_of(n * CHUNK, CHUNK), CHUNK)
            gc = gc_ref[sl, :]
            gr = gr_ref[n]
            for d, rev in enumerate((False, True)):
                if with_out:
                    qs.append(qn[sl, :])
                ks.append(kn[sl, :])
                vs.append(vn[sl, :])
                a_cs.append(gc[:, d:d + 1])
                b_cs.append(gc[:, 2 + d:3 + d])
                a_rs.append(gr[d:d + 1, :])
                alogs.append(hp[2 * d:2 * d + 1, 0:1])
                dtbs.append(hp[2 * d + 1:2 * d + 2, 0:1])
                revs.append(rev)
                where.append((d, n, sl))
        mp, cc, dec, qp, oi = _gdn_group(qs if with_out else None, ks, vs, a_cs, b_cs, a_rs, alogs, dtbs, revs)
        for i, (d, n, sl) in enumerate(where):
            mp_s[d, n] = mp[i].astype(BF16)
            cc_s[d, n] = cc[i]
            dc_s[d, n] = jnp.broadcast_to(dec[i], (8, HEAD_DIM))
            if with_out:
                qp_s[d, sl, :] = qp[i].astype(BF16)
                if d == 1:
                    o_scr[sl, :] = oi[i - 1] + oi[i]
        return carry

    lax.fori_loop(0, n_chunks // igroup, intra, 0)

    if with_out:
        sf[...] = s0f_ref[...]
        sb[...] = s0b_ref[...]
        nw = nw_ref[...]
    else:
        sf[...] = jnp.zeros_like(sf)
        sb[...] = jnp.zeros_like(sb)

    def scan(g, carry, final):
        sls, idxs, revs = _group_slices(g, group, n_chunks)
        for sl, idx, rev in zip(sls, idxs, revs):
            d = int(rev)
            s_ref = sb if rev else sf
            s = s_ref[...]
            sbf = s.astype(BF16)
            s_ref[...] = s * dc_s[d, idx][0:1, :] - _dot(mp_s[d, idx], sbf) + cc_s[d, idx]
            if with_out:
                o = o_scr[sl, :] + _dot(qp_s[d, sl, :], sbf)
                if final:
                    o_ref[sl, :] = _gated_head_norm(o, z_ref[sl, :], nw).astype(o_ref.dtype)
                else:
                    o_scr[sl, :] = o
        return carry

    trips = n_chunks // group
    if with_out:
        lax.fori_loop(0, trips // 2, functools.partial(scan, final=False), 0)
        lax.fori_loop(trips // 2, trips, functools.partial(scan, final=True), 0)
    else:
        lax.fori_loop(0, trips, functools.partial(scan, final=False), 0)


def _gdn_call(p, gates, conv_w, hp, nw, s0f, s0b, period, with_out, sec0):
    b, t, _ = p.shape
    n_chunks = t // CHUNK
    prep_rows = max(period, min(t, 256))
    assert prep_rows % period == 0 and t % prep_rows == 0
    group = _scan_group(n_chunks, GDN_GROUP, with_out)
    g4 = gates[:, :, :4 * N_HEADS].reshape(b, t, 4, N_HEADS)
    g_col = g4.transpose(0, 3, 1, 2)
    g_row = jnp.pad(g4.transpose(0, 3, 2, 1), ((0, 0), (0, 0), (0, 4), (0, 0)))
    g_row = g_row.reshape(b, N_HEADS, 8, n_chunks, CHUNK).transpose(0, 1, 3, 2, 4)
    g_row = jnp.concatenate([g_row, g_row], axis=-1)

    gc_spec = pl.BlockSpec((None, None, t, 4), lambda b_, h: (b_, h, 0, 0))
    gr_spec = pl.BlockSpec((None, None, n_chunks, 8, 2 * CHUNK), lambda b_, h: (b_, h, 0, 0, 0))
    conv_spec = lambda sec: pl.BlockSpec((3, HEAD_DIM), lambda b_, h: (0, sec * N_HEADS + h))
    hp_spec = pl.BlockSpec((None, 8, HEAD_DIM), lambda b_, h: (h, 0, 0))
    seq = lambda: pltpu.VMEM((t, HEAD_DIM), F32)
    mats = lambda dt: pltpu.VMEM((2, n_chunks, HEAD_DIM, HEAD_DIM), dt)
    dc = pltpu.VMEM((2, n_chunks, 8, HEAD_DIM), F32)
    state = lambda: pltpu.VMEM((HEAD_DIM, HEAD_DIM), F32)
    kern = functools.partial(_gdn_kernel, n_chunks=n_chunks, period=period, prep_rows=prep_rows,
                             group=group, igroup=_scan_group(n_chunks, GDN_INTRA_GROUP, False),
                             with_out=with_out)
    if with_out:
        return pl.pallas_call(
            kern,
            grid=(b, N_HEADS),
            in_specs=[_col_spec(t, sec0), _col_spec(t, sec0 + 1), _col_spec(t, sec0 + 2), _col_spec(t, sec0 + 3),
                      gc_spec, gr_spec, conv_spec(0), conv_spec(1), conv_spec(2), hp_spec,
                      pl.BlockSpec((1, HEAD_DIM), lambda b_, h: (0, 0)), _STATE_SPEC, _STATE_SPEC],
            out_specs=pl.BlockSpec((None, t, HEAD_DIM), lambda b_, h: (b_, 0, h)),
            out_shape=jax.ShapeDtypeStruct((b, t, N_HEADS * HEAD_DIM), BF16),
            scratch_shapes=[seq(), seq(), seq(), mats(BF16), mats(F32), dc,
                            pltpu.VMEM((2, t, HEAD_DIM), BF16), seq(), state(), state()],
            compiler_params=_cparams("parallel", "parallel"),
            name="gdn_x",
        )(p, p, p, p, g_col, g_row, conv_w, conv_w, conv_w, hp, nw, s0f, s0b)
    shp = jax.ShapeDtypeStruct((b, N_HEADS, HEAD_DIM, HEAD_DIM), F32)
    return pl.pallas_call(
        kern,
        grid=(b, N_HEADS),
        in_specs=[_col_spec(t, sec0 + 1), _col_spec(t, sec0 + 2), gc_spec, gr_spec,
                  conv_spec(1), conv_spec(2), hp_spec],
        out_specs=[_STATE_SPEC, _STATE_SPEC],
        out_shape=[shp, shp],
        scratch_shapes=[seq(), seq(), mats(BF16), mats(F32), dc],
        compiler_params=_cparams("parallel", "parallel"),
        name="gdn_ctx",
    )(p, p, g_col, g_row, conv_w, conv_w, hp)


def _outproj_kernel(mh_ref, mg_ref, x_ref, gt_ref, sc_ref, sh_ref, nw_ref, woh_ref, wog_ref,
                    wrh_ref, wrl_ref, br_ref, x1_ref, h2_ref, idx_ref, gate_ref, cnt_ref):
    y = _dot(mh_ref[...], woh_ref[...]) + _dot(mg_ref[...], wog_ref[...])
    x1 = x_ref[...] + gt_ref[...] * y
    x1_ref[...] = x1
    h = _modulated_norm(x1, nw_ref[...], sc_ref[...], sh_ref[...])
    hh = h.astype(BF16)
    half = hh.shape[1] // 2
    hi_bits = lax.bitcast_convert_type(hh[:, :half].astype(F32), jnp.uint32)
    lo_bits = lax.bitcast_convert_type(hh[:, half:].astype(F32), jnp.uint32)
    h2_ref[...] = hi_bits | (lo_bits >> 16)
    hl = (h - hh.astype(F32)).astype(BF16)
    wrh = wrh_ref[...]
    logits = _dot(hh, wrh) + _dot(hl, wrh) + _dot(hh, wrl_ref[...]) + br_ref[...]
    lane = lax.broadcasted_iota(jnp.int32, logits.shape, 1).astype(F32)
    vals, idxs = [], []
    for _ in range(TOP_K):
        m = jnp.max(logits, axis=-1, keepdims=True)
        i = jnp.min(jnp.where(logits == m, lane, float(LANES)), axis=-1, keepdims=True)
        vals.append(m)
        idxs.append(i)
        logits = jnp.where(lane == i, -jnp.inf, logits)
    es = [jnp.exp(v - vals[0]) for v in vals]
    inv = 1.0 / functools.reduce(lambda a, b_: a + b_, es)

    @pl.when(pl.program_id(0) == 0)
    def _():
        cnt_ref[...] = jnp.zeros_like(cnt_ref)

    onehots = [lane == i for i in idxs]
    picked = functools.reduce(lambda a, b_: a | b_, onehots)
    tm = lane.shape[0]
    before = (lax.broadcasted_iota(jnp.int32, (tm, tm), 1) < lax.broadcasted_iota(jnp.int32, (tm, tm), 0))
    prior = cnt_ref[...] + _dot(before.astype(BF16), picked.astype(BF16))
    ranks = [jnp.sum(jnp.where(oh, prior, 0.0), axis=-1, keepdims=True) for oh in onehots]
    cnt_ref[...] += jnp.sum(picked.astype(F32), axis=0, keepdims=True)

    idx_out = jnp.zeros(lane.shape, F32)
    gate_out = jnp.zeros(lane.shape, F32)
    for k in range(TOP_K):
        idx_out = jnp.where(lane == k, idxs[k], idx_out)
        idx_out = jnp.where(lane == TOP_K + k, ranks[k], idx_out)
        gate_out = jnp.where(lane == k, es[k] * inv, gate_out)
    idx_ref[...] = idx_out.astype(jnp.int32)
    gate_ref[...] = gate_out


def _outproj(mix_h, mix_g, x, gt1, sc2, sh2, nw, wo_h, wo_g, wr_hi, wr_lo, br):
    b, t, d = x.shape
    w = mix_h.shape[-1]
    tm = min(t, 512)
    tpb = t // tm
    row = lambda width: pl.BlockSpec((None, tm, width), lambda i: (i // tpb, i % tpb, 0))
    per_b = pl.BlockSpec((None, 1, d), lambda i: (i // tpb, 0, 0))
    const = lambda r, c: pl.BlockSpec((r, c), lambda i: (0, 0))
    return pl.pallas_call(
        _outproj_kernel,
        grid=(b * tpb,),
        in_specs=[row(w), row(w), row(d), per_b, per_b, per_b, const(1, d), const(w, d), const(w, d),
                  const(d, LANES), const(d, LANES), const(1, LANES)],
        out_specs=[row(d), row(d // 2), row(LANES), row(LANES), const(1, LANES)],
        out_shape=[jax.ShapeDtypeStruct((b, t, d), F32), jax.ShapeDtypeStruct((b, t, d // 2), jnp.uint32),
                   jax.ShapeDtypeStruct((b, t, LANES), jnp.int32), jax.ShapeDtypeStruct((b, t, LANES), F32),
                   jax.ShapeDtypeStruct((1, LANES), F32)],
        compiler_params=_cparams("arbitrary"),
        name="outproj",
    )(mix_h, mix_g, x, gt1, sc2, sh2, nw.reshape(1, d), wo_h, wo_g, wr_hi, wr_lo, br)


def _dispatch_kernel(dest_ref, h_ref, xs_init_ref, xs_ref, sem, *, tt, k):
    del xs_init_ref
    base = pl.program_id(0) * (tt * k)

    def body(t, carry):
        for kk in range(k):
            row = dest_ref[base + t * k + kk]
            pltpu.make_async_copy(h_ref.at[pl.ds(t, 1), :], xs_ref.at[pl.ds(row, 1), :], sem).start()
        return carry

    lax.fori_loop(0, tt, body, 0)
    done = xs_ref.at[pl.ds(0, tt * k), :]
    pltpu.make_async_copy(done, done, sem).wait()


def _dispatch(h2, dest, rows, k):
    n_tok, w = h2.shape
    tt = min(n_tok, 512)
    return pl.pallas_call(
        functools.partial(_dispatch_kernel, tt=tt, k=k),
        grid_spec=pltpu.PrefetchScalarGridSpec(
            num_scalar_prefetch=1,
            grid=(n_tok // tt,),
            in_specs=[pl.BlockSpec((tt, w), lambda i, dest_: (i, 0)),
                      pl.BlockSpec(memory_space=pl.ANY)],
            out_specs=pl.BlockSpec(memory_space=pl.ANY),
            scratch_shapes=[pltpu.SemaphoreType.DMA(())]),
        out_shape=jax.ShapeDtypeStruct((rows, w), h2.dtype),
        input_output_aliases={2: 0},
        compiler_params=_cparams("arbitrary"),
        name="dispatch",
    )(dest, h2, jnp.zeros((rows, w), h2.dtype))


def _cast_tile(src, dst):
    rows = 256

    def body(i, carry):
        sl = pl.ds(pl.multiple_of(i * rows, rows), rows)
        dst[sl, :] = src[sl, :].astype(dst.dtype)
        return carry

    lax.fori_loop(0, src.shape[0] // rows, body, 0)


def _moe_up_kernel(e_ref, j_ref, r_ref, ro_ref, jo_ref, first_ref, valid_ref,
                   x_ref, wg_ref, wu_ref, bg_ref, bu_ref, act_ref, wg_b, wu_b):
    s = pl.program_id(0)

    @pl.when(first_ref[s] == 1)
    def _():
        _cast_tile(wg_ref, wg_b)
        _cast_tile(wu_ref, wu_b)

    @pl.when(valid_ref[s] == 1)
    def _():
        xp = x_ref[...]
        half = xp.shape[1]
        xa = lax.bitcast_convert_type(xp & jnp.uint32(0xFFFF0000), F32).astype(BF16)
        xb = lax.bitcast_convert_type(xp << 16, F32).astype(BF16)
        gate = _dot(xa, wg_b[:half, :]) + _dot(xb, wg_b[half:, :]) + bg_ref[...]
        up = _dot(xa, wu_b[:half, :]) + _dot(xb, wu_b[half:, :]) + bu_ref[...]
        gate = jnp.minimum(gate, SWIGLU_LIMIT)
        up = jnp.clip(up, -SWIGLU_LIMIT, SWIGLU_LIMIT)
        act_ref[...] = ((up + 1.0) * gate * _sigmoid(SWIGLU_ALPHA * gate)).astype(act_ref.dtype)

    @pl.when(valid_ref[s] == 0)
    def _():
        act_ref[...] = jnp.zeros_like(act_ref)


def _moe_down_kernel(e_ref, j_ref, r_ref, ro_ref, jo_ref, first_ref, valid_ref,
                     a_ref, wd_ref, bd_ref, y_ref, wd_b):
    s = pl.program_id(0)

    @pl.when(first_ref[s] == 1)
    def _():
        _cast_tile(wd_ref, wd_b)

    @pl.when(valid_ref[s] == 1)
    def _():
        y_ref[...] = _dot(a_ref[...], wd_b[...]) + bd_ref[...]

    @pl.when(valid_ref[s] == 0)
    def _():
        y_ref[...] = jnp.zeros_like(y_ref)


def _moe_experts(xs, sched, wg, wu, wd, bg, bu, bd, tm, tf):
    rows = xs.shape[0]
    n_exp, d, f = wg.shape
    assert xs.shape[1] * 2 == d and d // tf == f // tf
    steps = sched[0].shape[0]
    w_tile = lambda shape: pl.BlockSpec((None,) + shape, lambda s, e, j, r, ro, jo, fi, va: (e[s], 0, j[s]))
    act = pl.pallas_call(
        _moe_up_kernel,
        grid_spec=pltpu.PrefetchScalarGridSpec(
            num_scalar_prefetch=7,
            grid=(steps,),
            in_specs=[pl.BlockSpec((tm, d // 2), lambda s, e, j, r, ro, jo, fi, va: (r[s], 0)),
                      w_tile((d, tf)), w_tile((d, tf)), w_tile((1, tf)), w_tile((1, tf))],
            out_specs=pl.BlockSpec((tm, tf), lambda s, e, j, r, ro, jo, fi, va: (ro[s], jo[s])),
            scratch_shapes=[pltpu.VMEM((d, tf), BF16), pltpu.VMEM((d, tf), BF16)]),
        out_shape=jax.ShapeDtypeStruct((rows, f), BF16),
        compiler_params=_cparams("arbitrary"),
        name="moe_up",
    )(*sched, xs, wg, wu, bg.reshape(n_exp, 1, f), bu.reshape(n_exp, 1, f))
    return pl.pallas_call(
        _moe_down_kernel,
        grid_spec=pltpu.PrefetchScalarGridSpec(
            num_scalar_prefetch=7,
            grid=(steps,),
            in_specs=[pl.BlockSpec((tm, f), lambda s, e, j, r, ro, jo, fi, va: (r[s], 0)),
                      w_tile((f, tf)), w_tile((1, tf))],
            out_specs=pl.BlockSpec((tm, tf), lambda s, e, j, r, ro, jo, fi, va: (ro[s], jo[s])),
            scratch_shapes=[pltpu.VMEM((f, tf), BF16)]),
        out_shape=jax.ShapeDtypeStruct((rows, d), F32),
        compiler_params=_cparams("arbitrary"),
        name="moe_down",
    )(*sched, act, wd, bd.reshape(n_exp, 1, d))


def _route(top_idx, rank, counts, tm, n_tiles):
    n_tok, k = top_idx.shape
    n_exp = counts.shape[0]
    m = n_tok * k
    padded = (counts + tm - 1) // tm * tm
    pend = jnp.cumsum(padded)
    onehot = top_idx.reshape(m, 1) == jnp.arange(n_exp, dtype=jnp.int32)[None, :]
    dest = jnp.sum(jnp.where(onehot, (pend - padded)[None, :], 0), axis=1) + rank.reshape(m)
    n_blocks = -(-m // tm) + n_exp

    nb = padded // tm
    blk0 = (pend - padded) // tm
    cum = jnp.cumsum(nb * n_tiles)
    total = cum[-1]
    s = jnp.arange(n_tiles * n_blocks, dtype=jnp.int32)
    valid = s < total
    sc = jnp.minimum(s, total - 1)
    e = jnp.minimum(jnp.searchsorted(cum, sc, side='right'), n_exp - 1).astype(jnp.int32)
    local = sc - (cum[e] - nb[e] * n_tiles)
    nbe = jnp.maximum(nb[e], 1)
    j = local // nbe
    r = blk0[e] + local % nbe
    first = valid & (local % nbe == 0)
    extra = s - total
    r_out = jnp.where(valid, r, total // n_tiles + extra // n_tiles)
    j_out = jnp.where(valid, j, extra % n_tiles)
    sched = tuple(a.astype(jnp.int32) for a in (e, j, r, r_out, j_out, first, valid))
    return dest.astype(jnp.int32), n_blocks * tm, sched


def _final_kernel(x1_ref, yg_ref, gate_ref, gt_ref, nw_ref, o_ref):
    g = gate_ref[...]
    moe = yg_ref[0] * g[:, 0:1]
    for k in range(1, TOP_K):
        moe = moe + yg_ref[k] * g[:, k:k + 1]
    x = x1_ref[...] + gt_ref[...] * moe
    ms = jnp.mean(x * x, axis=-1, keepdims=True)
    o_ref[...] = x * lax.rsqrt(ms + EPS) * nw_ref[...]


def _final(x1, yg, gates, gt2, nw):
    b, t, d = x1.shape
    tm = min(t, 256)
    tpb = t // tm
    row = lambda width: pl.BlockSpec((None, tm, width), lambda i: (i // tpb, i % tpb, 0))
    return pl.pallas_call(
        _final_kernel,
        grid=(b * tpb,),
        in_specs=[row(d),
                  pl.BlockSpec((TOP_K, None, tm, d), lambda i: (0, i // tpb, i % tpb, 0)),
                  row(LANES),
                  pl.BlockSpec((None, 1, d), lambda i: (i // tpb, 0, 0)),
                  pl.BlockSpec((1, d), lambda i: (0, 0))],
        out_specs=row(d),
        out_shape=jax.ShapeDtypeStruct((b, t, d), F32),
        compiler_params=_cparams("parallel"),
        name="final",
    )(x1, yg, gates, gt2, nw.reshape(1, d))


def kernel(x, c, ctx, c_ctx, w_ada, b_ada, norm_mix_w, w_in, hg_lb_f, hg_lb_b, hg_norm_w, gd_conv_w,
           gd_a_log_f, gd_a_log_b, gd_dt_bias_f, gd_dt_bias_b, gd_norm_w, w_out, norm_ffn_w, w_router,
           b_router, w_gate, b_gate, w_up, b_up, w_down, b_down, norm_out_w):
    bsz, t, d = x.shape
    t_ctx = ctx.shape[1]
    n_exp = w_router.shape[-1]
    l = 0
    hg_w = N_HEADS * HEAD_DIM
    n_main = 9 * hg_w

    lb_f = jnp.cumsum(jax.nn.softmax(hg_lb_f.astype(F32), axis=0), axis=0)[l].reshape(1, hg_w)
    lb_b = jnp.cumsum(jax.nn.softmax(hg_lb_b.astype(F32), axis=0), axis=0)[l].reshape(1, hg_w)

    rows = -(-(bsz + 1) // 8) * 8
    cc = jnp.zeros((rows, d), F32).at[:bsz].set(c).at[bsz].set(c_ctx)
    mod = _ada(cc, w_ada[l], b_ada[l])
    sh1, sc1, gt1, sh2, sc2, gt2 = (mod[:bsz, i * d:(i + 1) * d].reshape(bsz, 1, d) for i in range(6))
    csh1 = jnp.broadcast_to(mod[bsz, 0:d].reshape(1, 1, d), (bsz, 1, d))
    csc1 = jnp.broadcast_to(mod[bsz, d:2 * d].reshape(1, 1, d), (bsz, 1, d))

    w_in_b = w_in[l].astype(BF16)
    w_main = w_in_b[:, :n_main]
    w_gates = jnp.pad(w_in_b[:, n_main:], ((0, 0), (0, LANES - (w_in_b.shape[1] - n_main))))
    px, gx = _inproj(x, norm_mix_w[l], sc1, sh1, w_main, w_gates)
    pc, gc = _inproj(ctx, norm_mix_w[l], csc1, csh1, w_main, w_gates)

    hg_nw = hg_norm_w[l].reshape(1, HEAD_DIM)
    hs_f, hs_b = _hgrn_states(pc, lb_f, lb_b)
    mix_h = _hgrn_out(px, lb_f, lb_b, hg_nw, hs_f, hs_b)

    hp = jnp.stack([gd_a_log_f[l], gd_dt_bias_f[l], gd_a_log_b[l], gd_dt_bias_b[l]], axis=1)
    hp = jnp.broadcast_to(jnp.pad(hp, ((0, 0), (0, 4)))[:, :, None], (N_HEADS, 8, HEAD_DIM)).astype(F32)
    gd_nw = gd_norm_w[l].reshape(1, HEAD_DIM)
    gs_f, gs_b = _gdn_call(pc, gc, gd_conv_w[l], hp, gd_nw, None, None, t_ctx, False, 5)
    mix_g = _gdn_call(px, gx, gd_conv_w[l], hp, gd_nw, gs_f, gs_b, GRID_W, True, 5)

    w_out_b = w_out[l].astype(BF16)
    wr = jnp.pad(w_router[l], ((0, 0), (0, LANES - n_exp)))
    wr_hi = wr.astype(BF16)
    wr_lo = (wr - wr_hi.astype(F32)).astype(BF16)
    br = jnp.full((1, LANES), NEG_BIG, F32).at[0, :n_exp].set(b_router[l])
    x1, h2, idx_pad, gate_pad, cnt = _outproj(mix_h, mix_g, x, gt1, sc2, sh2, norm_ffn_w[l],
                                              w_out_b[:hg_w], w_out_b[hg_w:], wr_hi, wr_lo, br)

    n_tok = bsz * t
    idx_pad = idx_pad.reshape(n_tok, LANES)
    dest, n_rows, sched = _route(idx_pad[:, :TOP_K], idx_pad[:, TOP_K:2 * TOP_K],
                                 cnt[0, :n_exp].astype(jnp.int32), MOE_TM, w_gate.shape[-1] // MOE_TF)
    xs = _dispatch(h2.reshape(n_tok, d // 2), dest, n_rows, TOP_K)
    ys = _moe_experts(xs, sched, w_gate[l], w_up[l], w_down[l], b_gate[l], b_up[l], b_down[l], MOE_TM, MOE_TF)

    yg = ys[dest.reshape(n_tok, TOP_K).T].reshape(TOP_K, bsz, t, d)
    return _final(x1, yg, gate_pad, gt2, norm_out_w)
```

```python
import functools

import jax
import jax.numpy as jnp
from jax import lax
from jax.experimental import pallas as pl
from jax.experimental.pallas import tpu as pltpu

F32 = jnp.float32
BF16 = jnp.bfloat16

EPS = 1e-6
CHUNK = 64
HEAD_DIM = 128
N_HEADS = 8
GRID_W = 64
TOP_K = 4
SWIGLU_LIMIT = 7.0
SWIGLU_ALPHA = 1.702
LANES = 128
NEG_BIG = -1e30

HGRN_GROUP = 4
GDN_GROUP = 4
GDN_INTRA_GROUP = 8
MOE_TM = 512
MOE_TF = 1024
VMEM_LIMIT = 56 * 1024 * 1024


def _cparams(*sem):
    return pltpu.CompilerParams(dimension_semantics=sem, vmem_limit_bytes=VMEM_LIMIT)


def _dot(a, b):
    return jnp.dot(a, b, preferred_element_type=F32)


def _dot_nt(a, b):
    return lax.dot_general(a, b, (((1,), (1,)), ((), ())), preferred_element_type=F32)


def _split2(x):
    hi = x.astype(BF16)
    return hi, (x - hi.astype(F32)).astype(BF16)


def _split3(x):
    hi = x.astype(BF16)
    rest = x - hi.astype(F32)
    mid = rest.astype(BF16)
    return hi, mid, (rest - mid.astype(F32)).astype(BF16)


def _dot_split(a, b):
    ah, al = a
    bh, bl = b
    return _dot(ah, bh) + (_dot(ah, bl) + _dot(al, bh))


def _sigmoid(x):
    return 1.0 / (1.0 + jnp.exp(-x))


def _ada_kernel(c_ref, w_ref, b_ref, o_ref):
    c = c_ref[...]
    s = (c * _sigmoid(c)).astype(BF16)
    o_ref[...] = _dot(s, w_ref[...].astype(BF16)) + b_ref[...]


def _ada(cc, w_ada, b_ada):
    rows, d = cc.shape
    n = w_ada.shape[1]
    tn = 1024
    return pl.pallas_call(
        _ada_kernel,
        grid=(n // tn,),
        in_specs=[pl.BlockSpec((rows, d), lambda j: (0, 0)),
                  pl.BlockSpec((d, tn), lambda j: (0, j)),
                  pl.BlockSpec((1, tn), lambda j: (0, j))],
        out_specs=pl.BlockSpec((rows, tn), lambda j: (0, j)),
        out_shape=jax.ShapeDtypeStruct((rows, n), F32),
        compiler_params=_cparams("arbitrary"),
        name="ada",
    )(cc, w_ada, b_ada.reshape(1, n))


def _modulated_norm(x, nw, sc, sh):
    ms = jnp.mean(x * x, axis=-1, keepdims=True)
    return (x * lax.rsqrt(ms + EPS) * nw) * (1.0 + sc) + sh


def _inproj_kernel(x_ref, nw_ref, sc_ref, sh_ref, w_ref, wg_ref, o_ref, og_ref, h_scr):
    @pl.when(pl.program_id(1) == 0)
    def _():
        h = _modulated_norm(x_ref[...], nw_ref[...], sc_ref[...], sh_ref[...]).astype(BF16)
        h_scr[...] = h
        og_ref[...] = _dot(h, wg_ref[...])

    o_ref[...] = _dot(h_scr[...], w_ref[...])


def _inproj(x, nw, sc, sh, w_all, n, w_gates):
    b, t, d = x.shape
    w_main = w_all
    tm = min(t, 1024)
    tn = 1024
    tpb = t // tm
    assert n % tn == 0
    return pl.pallas_call(
        _inproj_kernel,
        grid=(b * tpb, n // tn),
        in_specs=[pl.BlockSpec((None, tm, d), lambda i, j: (i // tpb, i % tpb, 0)),
                  pl.BlockSpec((1, d), lambda i, j: (0, 0)),
                  pl.BlockSpec((None, 1, d), lambda i, j: (i // tpb, 0, 0)),
                  pl.BlockSpec((None, 1, d), lambda i, j: (i // tpb, 0, 0)),
                  pl.BlockSpec((d, tn), lambda i, j: (0, j)),
                  pl.BlockSpec((d, LANES), lambda i, j: (0, 0))],
        out_specs=[pl.BlockSpec((None, tm, tn), lambda i, j: (i // tpb, i % tpb, j)),
                   pl.BlockSpec((None, tm, LANES), lambda i, j: (i // tpb, i % tpb, 0))],
        out_shape=[jax.ShapeDtypeStruct((b, t, n), F32),
                   jax.ShapeDtypeStruct((b, t, LANES), F32)],
        scratch_shapes=[pltpu.VMEM((tm, d), BF16)],
        compiler_params=_cparams("parallel", "arbitrary"),
        name="inproj",
    )(x, nw.reshape(1, d), sc, sh, w_main, w_gates)


def _chunk_masks(rev, wide=False):
    shape = (CHUNK, 2 * CHUNK if wide else CHUNK)
    r = lax.broadcasted_iota(jnp.int32, shape, 0)
    c = lax.broadcasted_iota(jnp.int32, shape, 1) & (CHUNK - 1)
    incl = (c >= r) if rev else (c <= r)
    strict = (c > r) if rev else (c < r)
    return incl, strict


def _gated_head_norm(o, z, nw):
    ms = jnp.mean(o * o, axis=-1, keepdims=True)
    return (o * lax.rsqrt(ms + EPS) * nw) * (z * _sigmoid(z))


def _group_slices(g, group, n_chunks):
    idxs, revs = [], []
    for i in range(group):
        n = g * group + i
        idxs += [n, n_chunks - 1 - n]
        revs += [False, True]
    return [pl.ds(pl.multiple_of(ix * CHUNK, CHUNK), CHUNK) for ix in idxs], idxs, revs


def _scan_group(n_chunks, want, two_phase):
    g = want
    while g > 1 and (n_chunks % g or (two_phase and (n_chunks // g) % 2)):
        g //= 2
    assert n_chunks % g == 0 and not (two_phase and (n_chunks // g) % 2)
    return g


def _gla_group(zfs, vs, qs, lbs, revs):
    n = len(zfs)
    incl = [_chunk_masks(r)[0] for r in revs]
    logf, k = [], []
    for i in range(n):
        one_m = 1.0 - lbs[i]
        logf.append(jnp.log(lbs[i] + one_m * _sigmoid(zfs[i])))
        k.append(one_m * _sigmoid(-zfs[i]))
    cum = []
    for i in range(n):
        tri = incl[i].astype(BF16)
        hi, mid, lo = _split3(logf[i])
        parts = _dot(tri, jnp.concatenate([hi, mid, lo], axis=1))
        cum.append(parts[:, :HEAD_DIM] + (parts[:, HEAD_DIM:2 * HEAD_DIM] + parts[:, 2 * HEAD_DIM:]))
    last = [cum[i][0:1] if revs[i] else cum[i][CHUNK - 1:CHUNK] for i in range(n)]
    u_t = [_dot(vs[i].T.astype(BF16), (k[i] * jnp.exp(last[i] - cum[i])).astype(BF16)) for i in range(n)]
    dec = [jnp.exp(x) for x in last]
    if qs is None:
        return u_t, dec, None, None
    scores = []
    for i in range(n):
        mid_row = CHUNK // 2 - 1 if revs[i] else CHUNK // 2
        ref = cum[i][mid_row:mid_row + 1]
        sc = _dot_nt((qs[i] * jnp.exp(cum[i] - ref)).astype(BF16), (k[i] * jnp.exp(ref - cum[i])).astype(BF16))
        scores.append(jnp.where(incl[i], sc, 0.0).astype(BF16))
    o_intra = [_dot(scores[i], vs[i].astype(BF16)) for i in range(n)]
    qe = [(qs[i] * jnp.exp(cum[i])).astype(BF16) for i in range(n)]
    return u_t, dec, o_intra, qe


def _hgrn_out_kernel(q_ref, ff_ref, fb_ref, v_ref, g_ref, lbf_ref, lbb_ref, nw_ref, s0f_ref, s0b_ref,
                     o_ref, o_scr, sf_scr, sb_scr, *, n_chunks, group):
    sf_scr[...] = s0f_ref[...]
    sb_scr[...] = s0b_ref[...]
    lbf, lbb, nw = lbf_ref[...], lbb_ref[...], nw_ref[...]

    def body(g, carry, final):
        sls, _, revs = _group_slices(g, group, n_chunks)
        zfs = [(fb_ref if r else ff_ref)[sl, :] for sl, r in zip(sls, revs)]
        vs = [v_ref[sl, :] for sl in sls]
        qs = [q_ref[sl, :] for sl in sls]
        u_t, dec, o_intra, qe = _gla_group(zfs, vs, qs, [lbb if r else lbf for r in revs], revs)
        for i, (sl, rev) in enumerate(zip(sls, revs)):
            s_scr = sb_scr if rev else sf_scr
            st = s_scr[...]
            o = o_intra[i] + _dot_nt(qe[i], st.astype(BF16))
            s_scr[...] = st * dec[i] + u_t[i]
            if final:
                o_ref[sl, :] = _gated_head_norm(o_scr[sl, :] + o, g_ref[sl, :], nw).astype(o_ref.dtype)
            else:
                o_scr[sl, :] = o
        return carry

    trips = n_chunks // group
    lax.fori_loop(0, trips // 2, functools.partial(body, final=False), 0)
    lax.fori_loop(trips // 2, trips, functools.partial(body, final=True), 0)


def _hgrn_state_kernel(ff_ref, fb_ref, v_ref, lbf_ref, lbb_ref, sf_ref, sb_ref, *, n_chunks, group):
    sf_ref[...] = jnp.zeros_like(sf_ref)
    sb_ref[...] = jnp.zeros_like(sb_ref)
    lbf, lbb = lbf_ref[...], lbb_ref[...]

    def body(g, carry):
        sls, _, revs = _group_slices(g, group, n_chunks)
        zfs = [(fb_ref if r else ff_ref)[sl, :] for sl, r in zip(sls, revs)]
        vs = [v_ref[sl, :] for sl in sls]
        u_t, dec, _, _ = _gla_group(zfs, vs, None, [lbb if r else lbf for r in revs], revs)
        for i, rev in enumerate(revs):
            s_ref = sb_ref if rev else sf_ref
            s_ref[...] = s_ref[...] * dec[i] + u_t[i]
        return carry

    lax.fori_loop(0, n_chunks // group, body, 0)


def _col_spec(t, section):
    return pl.BlockSpec((None, t, HEAD_DIM), lambda b, h: (b, 0, section * N_HEADS + h))


def _head_row_spec(offset=0):
    return pl.BlockSpec((1, HEAD_DIM), lambda b, h: (0, offset + h))


_STATE_SPEC = pl.BlockSpec((None, None, HEAD_DIM, HEAD_DIM), lambda b, h: (b, h, 0, 0))


def _hgrn_states(pc, lbf, lbb):
    b, t, _ = pc.shape
    n_chunks = t // CHUNK
    shp = jax.ShapeDtypeStruct((b, N_HEADS, HEAD_DIM, HEAD_DIM), F32)
    return pl.pallas_call(
        functools.partial(_hgrn_state_kernel, n_chunks=n_chunks, group=_scan_group(n_chunks, HGRN_GROUP, False)),
        grid=(b, N_HEADS),
        in_specs=[_col_spec(t, 1), _col_spec(t, 2), _col_spec(t, 3), _head_row_spec(), _head_row_spec()],
        out_specs=[_STATE_SPEC, _STATE_SPEC],
        out_shape=[shp, shp],
        compiler_params=_cparams("parallel", "parallel"),
        name="hgrn_ctx",
    )(pc, pc, pc, lbf, lbb)


def _hgrn_out(px, lbf, lbb, nw, s0f, s0b):
    b, t, _ = px.shape
    n_chunks = t // CHUNK
    return pl.pallas_call(
        functools.partial(_hgrn_out_kernel, n_chunks=n_chunks, group=_scan_group(n_chunks, HGRN_GROUP, True)),
        grid=(b, N_HEADS),
        in_specs=[_col_spec(t, 0), _col_spec(t, 1), _col_spec(t, 2), _col_spec(t, 3), _col_spec(t, 4),
                  _head_row_spec(), _head_row_spec(),
                  pl.BlockSpec((1, HEAD_DIM), lambda b_, h: (0, 0)),
                  _STATE_SPEC, _STATE_SPEC],
        out_specs=pl.BlockSpec((None, t, HEAD_DIM), lambda b_, h: (b_, 0, h)),
        out_shape=jax.ShapeDtypeStruct((b, t, N_HEADS * HEAD_DIM), BF16),
        scratch_shapes=[pltpu.VMEM((t, HEAD_DIM), F32),
                        pltpu.VMEM((HEAD_DIM, HEAD_DIM), F32),
                        pltpu.VMEM((HEAD_DIM, HEAD_DIM), F32)],
        compiler_params=_cparams("parallel", "parallel"),
        name="hgrn_x",
    )(px, px, px, px, px, lbf, lbb, nw, s0f, s0b)


def _conv_silu(a, w, period):
    rows = a.shape[0]
    pos = lax.broadcasted_iota(jnp.int32, a.shape, 0) % period
    prev = jnp.where(pos == 0, 0.0, pltpu.roll(a, 1, axis=0))
    nxt = jnp.where(pos == period - 1, 0.0, pltpu.roll(a, rows - 1, axis=0))
    y = prev * w[0:1] + a * w[1:2] + nxt * w[2:3]
    return y * _sigmoid(y)


def _l2norm(x):
    return x * lax.rsqrt(jnp.sum(x * x, axis=-1, keepdims=True) + EPS)


def _softplus(x):
    return jnp.maximum(x, 0.0) + jnp.log1p(jnp.exp(-jnp.abs(x)))


def _unit_tri_inverses(tris):
    row = lax.broadcasted_iota(jnp.int32, (CHUNK, 2 * CHUNK), 0)
    lane = lax.broadcasted_iota(jnp.int32, (CHUNK, 2 * CHUNK), 1)
    left = lane < CHUNK
    eye = jnp.where((lane & (CHUNK - 1)) == row, 1.0, 0.0)
    zeros = jnp.zeros((CHUNK, 2 * CHUNK), BF16)

    def lhs(hl):
        return jnp.where(left, hl[0], hl[1])

    def rhs(hl):
        top = jnp.concatenate([hl[0], hl[1]], axis=1)
        return jnp.concatenate([top, jnp.concatenate([hl[0], zeros], axis=1)], axis=0)

    def fold(o):
        return o[:, :2 * CHUNK] + o[:, 2 * CHUNK:]

    ps = [_split2(-a) for a in tris]
    invs = [eye - a for a in tris]
    ps = [_split2(fold(_dot(lhs(p), rhs(p)))) for p in ps]
    for _ in range(CHUNK.bit_length() - 3):
        outs = [fold(_dot(jnp.concatenate([lhs(p), lhs(_split2(inv))], axis=0), rhs(p)))
                for p, inv in zip(ps, invs)]
        invs = [inv + o[CHUNK:] for inv, o in zip(invs, outs)]
        ps = [_split2(o[:CHUNK]) for o in outs]
    invs = [inv + fold(_dot(lhs(_split2(inv)), rhs(p))) for inv, p in zip(invs, ps)]
    return [inv[:, :CHUNK] for inv in invs]


def _gdn_group(qs, ks, vs, a_cs, b_cs, a_rs, alogs, dtbs, revs):
    n = len(ks)
    left = lax.broadcasted_iota(jnp.int32, (CHUNK, 2 * CHUNK), 1) < CHUNK
    dmask, ecum, beta, eend, dec, strict = [], [], [], [], [], []
    for i in range(n):
        incl, st = _chunk_masks(revs[i], wide=True)
        incl_t = _chunk_masks(not revs[i], wide=True)[0]
        scale = -jnp.exp(alogs[i])
        g_c = scale * _softplus(a_cs[i] + dtbs[i])
        g_r = scale * _softplus(a_rs[i] + dtbs[i])
        cum_c = jnp.sum(jnp.where(incl & left, g_r, 0.0), axis=1, keepdims=True)
        cum_r = jnp.sum(jnp.where(incl_t, g_c, 0.0), axis=0, keepdims=True)
        dmask.append(jnp.exp(jnp.where(incl, cum_c - cum_r, -jnp.inf)))
        ecum.append(jnp.exp(cum_c))
        beta.append(_sigmoid(b_cs[i]))
        last = cum_c[0:1] if revs[i] else cum_c[CHUNK - 1:CHUNK]
        eend.append(jnp.exp(last - cum_c))
        dec.append(jnp.exp(last))
        strict.append(st)
    kbf = [k.astype(BF16) for k in ks]
    kb = [ks[i] * beta[i] for i in range(n)]
    kk = [_dot_nt(kb[i].astype(BF16), jnp.concatenate([kbf[i], kbf[i]], axis=0)) for i in range(n)]
    tris = [jnp.where(strict[i], kk[i] * dmask[i], 0.0) for i in range(n)]
    tinv = [t.astype(BF16) for t in _unit_tri_inverses(tris)]
    dmask = [m[:, :CHUNK] for m in dmask]
    u = [_dot(tinv[i], (vs[i] * beta[i]).astype(BF16)).astype(BF16) for i in range(n)]
    w = [_dot(tinv[i], (kb[i] * ecum[i]).astype(BF16)).astype(BF16) for i in range(n)]
    ke_t = [(ks[i] * eend[i]).T.astype(BF16) for i in range(n)]
    mp = [_dot(ke_t[i], w[i]) for i in range(n)]
    cc = [_dot(ke_t[i], u[i]) for i in range(n)]
    if qs is None:
        return mp, cc, dec, None, None
    attn = [(_dot_nt(qs[i].astype(BF16), kbf[i]) * dmask[i]).astype(BF16) for i in range(n)]
    qp = [qs[i] * ecum[i] - _dot(attn[i], w[i]) for i in range(n)]
    oi = [_dot(attn[i], u[i]) for i in range(n)]
    return mp, cc, dec, qp, oi


def _gdn_kernel(*refs, n_chunks, period, prep_rows, group, igroup, with_out):
    if with_out:
        (q_ref, k_ref, v_ref, z_ref, gc_ref, gr_ref, wq_ref, wk_ref, wv_ref, hp_ref, nw_ref,
         s0f_ref, s0b_ref, o_ref, qn, kn, vn, mp_s, cc_s, dc_s, qp_s, o_scr, sf, sb) = refs
    else:
        (k_ref, v_ref, gc_ref, gr_ref, wk_ref, wv_ref, hp_ref,
         sf, sb, kn, vn, mp_s, cc_s, dc_s) = refs
        q_ref = None

    t = n_chunks * CHUNK

    def prep(i, carry):
        sl = pl.ds(pl.multiple_of(i * prep_rows, prep_rows), prep_rows)
        kn[sl, :] = _l2norm(_conv_silu(k_ref[sl, :], wk_ref[...], period))
        vn[sl, :] = _conv_silu(v_ref[sl, :], wv_ref[...], period)
        if with_out:
            qn[sl, :] = _l2norm(_conv_silu(q_ref[sl, :], wq_ref[...], period)) * (HEAD_DIM ** -0.5)
        return carry

    lax.fori_loop(0, t // prep_rows, prep, 0)

    hp = hp_ref[...]

    def intra(g, carry):
        qs, ks, vs, a_cs, b_cs, a_rs, alogs, dtbs, revs, where = [], [], [], [], [], [], [], [], [], []
        for i in range(igroup):
            n = g * igroup + i
            sl = pl.ds(pl.multiple_of(n * CHUNK, CHUNK), CHUNK)
            gc = gc_ref[sl, :]
            gr = gr_ref[n]
            for d, rev in enumerate((False, True)):
                if with_out:
                    qs.append(qn[sl, :])
                ks.append(kn[sl, :])
                vs.append(vn[sl, :])
                a_cs.append(gc[:, d:d + 1])
                b_cs.append(gc[:, 2 + d:3 + d])
                a_rs.append(gr[d:d + 1, :])
                alogs.append(hp[2 * d:2 * d + 1, 0:1])
                dtbs.append(hp[2 * d + 1:2 * d + 2, 0:1])
                revs.append(rev)
                where.append((d, n, sl))
        mp, cc, dec, qp, oi = _gdn_group(qs if with_out else None, ks, vs, a_cs, b_cs, a_rs, alogs, dtbs, revs)
        for i, (d, n, sl) in enumerate(where):
            mp_s[d, n] = mp[i].astype(BF16)
            cc_s[d, n] = cc[i]
            dc_s[d, n] = jnp.broadcast_to(dec[i], (8, HEAD_DIM))
            if with_out:
                qp_s[d, sl, :] = qp[i].astype(BF16)
                if d == 1:
                    o_scr[sl, :] = oi[i - 1] + oi[i]
        return carry

    lax.fori_loop(0, n_chunks // igroup, intra, 0)

    if with_out:
        sf[...] = s0f_ref[...]
        sb[...] = s0b_ref[...]
        nw = nw_ref[...]
    else:
        sf[...] = jnp.zeros_like(sf)
        sb[...] = jnp.zeros_like(sb)

    def scan(g, carry, final):
        sls, idxs, revs = _group_slices(g, group, n_chunks)
        for sl, idx, rev in zip(sls, idxs, revs):
            d = int(rev)
            s_ref = sb if rev else sf
            s = s_ref[...]
            sbf = s.astype(BF16)
            s_ref[...] = s * dc_s[d, idx][0:1, :] - _dot(mp_s[d, idx], sbf) + cc_s[d, idx]
            if with_out:
                o = o_scr[sl, :] + _dot(qp_s[d, sl, :], sbf)
                if final:
                    o_ref[sl, :] = _gated_head_norm(o, z_ref[sl, :], nw).astype(o_ref.dtype)
                else:
                    o_scr[sl, :] = o
        return carry

    trips = n_chunks // group
    if with_out:
        lax.fori_loop(0, trips // 2, functools.partial(scan, final=False), 0)
        lax.fori_loop(trips // 2, trips, functools.partial(scan, final=True), 0)
    else:
        lax.fori_loop(0, trips, functools.partial(scan, final=False), 0)


def _gdn_call(p, gates, conv_w, hp, nw, s0f, s0b, period, with_out, sec0):
    b, t, _ = p.shape
    n_chunks = t // CHUNK
    prep_rows = max(period, min(t, 256))
    assert prep_rows % period == 0 and t % prep_rows == 0
    group = _scan_group(n_chunks, GDN_GROUP, with_out)
    g4 = gates[:, :, :4 * N_HEADS].reshape(b, t, 4, N_HEADS)
    g_col = g4.transpose(0, 3, 1, 2)
    g_row = jnp.pad(g4.transpose(0, 3, 2, 1), ((0, 0), (0, 0), (0, 4), (0, 0)))
    g_row = g_row.reshape(b, N_HEADS, 8, n_chunks, CHUNK).transpose(0, 1, 3, 2, 4)
    g_row = jnp.concatenate([g_row, g_row], axis=-1)

    gc_spec = pl.BlockSpec((None, None, t, 4), lambda b_, h: (b_, h, 0, 0))
    gr_spec = pl.BlockSpec((None, None, n_chunks, 8, 2 * CHUNK), lambda b_, h: (b_, h, 0, 0, 0))
    conv_spec = lambda sec: pl.BlockSpec((3, HEAD_DIM), lambda b_, h: (0, sec * N_HEADS + h))
    hp_spec = pl.BlockSpec((None, 8, HEAD_DIM), lambda b_, h: (h, 0, 0))
    seq = lambda: pltpu.VMEM((t, HEAD_DIM), F32)
    mats = lambda dt: pltpu.VMEM((2, n_chunks, HEAD_DIM, HEAD_DIM), dt)
    dc = pltpu.VMEM((2, n_chunks, 8, HEAD_DIM), F32)
    state = lambda: pltpu.VMEM((HEAD_DIM, HEAD_DIM), F32)
    kern = functools.partial(_gdn_kernel, n_chunks=n_chunks, period=period, prep_rows=prep_rows,
                             group=group, igroup=_scan_group(n_chunks, GDN_INTRA_GROUP, False),
                             with_out=with_out)
    if with_out:
        return pl.pallas_call(
            kern,
            grid=(b, N_HEADS),
            in_specs=[_col_spec(t, sec0), _col_spec(t, sec0 + 1), _col_spec(t, sec0 + 2), _col_spec(t, sec0 + 3),
                      gc_spec, gr_spec, conv_spec(0), conv_spec(1), conv_spec(2), hp_spec,
                      pl.BlockSpec((1, HEAD_DIM), lambda b_, h: (0, 0)), _STATE_SPEC, _STATE_SPEC],
            out_specs=pl.BlockSpec((None, t, HEAD_DIM), lambda b_, h: (b_, 0, h)),
            out_shape=jax.ShapeDtypeStruct((b, t, N_HEADS * HEAD_DIM), BF16),
            scratch_shapes=[seq(), seq(), seq(), mats(BF16), mats(F32), dc,
                            pltpu.VMEM((2, t, HEAD_DIM), BF16), seq(), state(), state()],
            compiler_params=_cparams("parallel", "parallel"),
            name="gdn_x",
        )(p, p, p, p, g_col, g_row, conv_w, conv_w, conv_w, hp, nw, s0f, s0b)
    shp = jax.ShapeDtypeStruct((b, N_HEADS, HEAD_DIM, HEAD_DIM), F32)
    return pl.pallas_call(
        kern,
        grid=(b, N_HEADS),
        in_specs=[_col_spec(t, sec0 + 1), _col_spec(t, sec0 + 2), gc_spec, gr_spec,
                  conv_spec(1), conv_spec(2), hp_spec],
        out_specs=[_STATE_SPEC, _STATE_SPEC],
        out_shape=[shp, shp],
        scratch_shapes=[seq(), seq(), mats(BF16), mats(F32), dc],
        compiler_params=_cparams("parallel", "parallel"),
        name="gdn_ctx",
    )(p, p, g_col, g_row, conv_w, conv_w, hp)


def _outproj_kernel(mh_ref, mg_ref, x_ref, gt_ref, sc_ref, sh_ref, nw_ref, woh_ref, wog_ref,
                    wrh_ref, wrl_ref, br_ref, x1_ref, h2_ref, idx_ref, gate_ref, cnt_ref):
    y = _dot(mh_ref[...], woh_ref[...]) + _dot(mg_ref[...], wog_ref[...])
    x1 = x_ref[...] + gt_ref[...] * y
    x1_ref[...] = x1
    h = _modulated_norm(x1, nw_ref[...], sc_ref[...], sh_ref[...])
    hh = h.astype(BF16)
    half = hh.shape[1] // 2
    hi_bits = lax.bitcast_convert_type(hh[:, :half].astype(F32), jnp.uint32)
    lo_bits = lax.bitcast_convert_type(hh[:, half:].astype(F32), jnp.uint32)
    h2_ref[...] = hi_bits | (lo_bits >> 16)
    hl = (h - hh.astype(F32)).astype(BF16)
    wrh = wrh_ref[...]
    logits = _dot(hh, wrh) + _dot(hl, wrh) + _dot(hh, wrl_ref[...]) + br_ref[...]
    lane = lax.broadcasted_iota(jnp.int32, logits.shape, 1).astype(F32)
    vals, idxs = [], []
    for _ in range(TOP_K):
        m = jnp.max(logits, axis=-1, keepdims=True)
        i = jnp.min(jnp.where(logits == m, lane, float(LANES)), axis=-1, keepdims=True)
        vals.append(m)
        idxs.append(i)
        logits = jnp.where(lane == i, -jnp.inf, logits)
    es = [jnp.exp(v - vals[0]) for v in vals]
    inv = 1.0 / functools.reduce(lambda a, b_: a + b_, es)

    @pl.when(pl.program_id(0) == 0)
    def _():
        cnt_ref[...] = jnp.zeros_like(cnt_ref)

    onehots = [lane == i for i in idxs]
    picked = functools.reduce(lambda a, b_: a | b_, onehots)
    tm = lane.shape[0]
    before = (lax.broadcasted_iota(jnp.int32, (tm, tm), 1) < lax.broadcasted_iota(jnp.int32, (tm, tm), 0))
    prior = cnt_ref[...] + _dot(before.astype(BF16), picked.astype(BF16))
    ranks = [jnp.sum(jnp.where(oh, prior, 0.0), axis=-1, keepdims=True) for oh in onehots]
    cnt_ref[...] += jnp.sum(picked.astype(F32), axis=0, keepdims=True)

    idx_out = jnp.zeros(lane.shape, F32)
    gate_out = jnp.zeros(lane.shape, F32)
    for k in range(TOP_K):
        idx_out = jnp.where(lane == k, idxs[k], idx_out)
        idx_out = jnp.where(lane == TOP_K + k, ranks[k], idx_out)
        gate_out = jnp.where(lane == k, es[k] * inv, gate_out)
    idx_ref[...] = idx_out.astype(jnp.int32)
    gate_ref[...] = gate_out


def _outproj(mix_h, mix_g, x, gt1, sc2, sh2, nw, w_out2, wr_hi, wr_lo, br):
    b, t, d = x.shape
    w = mix_h.shape[-1]
    tm = min(t, 512)
    tpb = t // tm
    row = lambda width: pl.BlockSpec((None, tm, width), lambda i: (i // tpb, i % tpb, 0))
    per_b = pl.BlockSpec((None, 1, d), lambda i: (i // tpb, 0, 0))
    const = lambda r, c: pl.BlockSpec((r, c), lambda i: (0, 0))
    return pl.pallas_call(
        _outproj_kernel,
        grid=(b * tpb,),
        in_specs=[row(w), row(w), row(d), per_b, per_b, per_b, const(1, d),
                  pl.BlockSpec((None, w, d), lambda i: (0, 0, 0)), pl.BlockSpec((None, w, d), lambda i: (1, 0, 0)),
                  const(d, LANES), const(d, LANES), const(1, LANES)],
        out_specs=[row(d), row(d // 2), row(LANES), row(LANES), const(1, LANES)],
        out_shape=[jax.ShapeDtypeStruct((b, t, d), F32), jax.ShapeDtypeStruct((b, t, d // 2), jnp.uint32),
                   jax.ShapeDtypeStruct((b, t, LANES), jnp.int32), jax.ShapeDtypeStruct((b, t, LANES), F32),
                   jax.ShapeDtypeStruct((1, LANES), F32)],
        compiler_params=_cparams("arbitrary"),
        name="outproj",
    )(mix_h, mix_g, x, gt1, sc2, sh2, nw.reshape(1, d), w_out2, w_out2, wr_hi, wr_lo, br)


def _dispatch_kernel(dest_ref, h_ref, xs_init_ref, xs_ref, sem, *, tt, k):
    del xs_init_ref
    base = pl.program_id(0) * (tt * k)

    def body(tb, carry):
        t0 = pl.multiple_of(tb * 8, 8)
        for u in range(8):
            for kk in range(k):
                row = dest_ref[base + (t0 + u) * k + kk]
                pltpu.make_async_copy(h_ref.at[pl.ds(t0 + u, 1), :], xs_ref.at[pl.ds(row, 1), :], sem).start()
        return carry

    lax.fori_loop(0, tt // 8, body, 0)
    done = xs_ref.at[pl.ds(0, tt * k), :]
    pltpu.make_async_copy(done, done, sem).wait()


def _dispatch(h2, dest, rows, k):
    n_tok, w = h2.shape
    tt = min(n_tok, 512)
    return pl.pallas_call(
        functools.partial(_dispatch_kernel, tt=tt, k=k),
        grid_spec=pltpu.PrefetchScalarGridSpec(
            num_scalar_prefetch=1,
            grid=(n_tok // tt,),
            in_specs=[pl.BlockSpec((tt, w), lambda i, dest_: (i, 0)),
                      pl.BlockSpec(memory_space=pl.ANY)],
            out_specs=pl.BlockSpec(memory_space=pl.ANY),
            scratch_shapes=[pltpu.SemaphoreType.DMA(())]),
        out_shape=jax.ShapeDtypeStruct((rows, w), h2.dtype),
        input_output_aliases={2: 0},
        compiler_params=_cparams("arbitrary"),
        name="dispatch",
    )(dest, h2, jnp.zeros((rows, w), h2.dtype))


def _cast_tile(src, dst):
    rows = 256

    def body(i, carry):
        sl = pl.ds(pl.multiple_of(i * rows, rows), rows)
        dst[sl, :] = src[sl, :].astype(dst.dtype)
        return carry

    lax.fori_loop(0, src.shape[0] // rows, body, 0)


def _moe_up_kernel(e_ref, j_ref, r_ref, ro_ref, jo_ref, first_ref, valid_ref,
                   x_ref, wg_ref, wu_ref, bg_ref, bu_ref, act_ref, wg_b, wu_b):
    s = pl.program_id(0)

    @pl.when(first_ref[s] == 1)
    def _():
        _cast_tile(wg_ref, wg_b)
        _cast_tile(wu_ref, wu_b)

    @pl.when(valid_ref[s] == 1)
    def _():
        xp = x_ref[...]
        half = xp.shape[1]
        xa = lax.bitcast_convert_type(xp & jnp.uint32(0xFFFF0000), F32).astype(BF16)
        xb = lax.bitcast_convert_type(xp << 16, F32).astype(BF16)
        gate = _dot(xa, wg_b[:half, :]) + _dot(xb, wg_b[half:, :]) + bg_ref[...]
        up = _dot(xa, wu_b[:half, :]) + _dot(xb, wu_b[half:, :]) + bu_ref[...]
        gate = jnp.minimum(gate, SWIGLU_LIMIT)
        up = jnp.clip(up, -SWIGLU_LIMIT, SWIGLU_LIMIT)
        act_ref[...] = ((up + 1.0) * gate * _sigmoid(SWIGLU_ALPHA * gate)).astype(act_ref.dtype)

    @pl.when(valid_ref[s] == 0)
    def _():
        act_ref[...] = jnp.zeros_like(act_ref)


def _moe_down_kernel(e_ref, j_ref, r_ref, ro_ref, jo_ref, first_ref, valid_ref,
                     a_ref, wd_ref, bd_ref, y_ref, wd_b):
    s = pl.program_id(0)

    @pl.when(first_ref[s] == 1)
    def _():
        _cast_tile(wd_ref, wd_b)

    @pl.when(valid_ref[s] == 1)
    def _():
        y_ref[...] = _dot(a_ref[...], wd_b[...]) + bd_ref[...]

    @pl.when(valid_ref[s] == 0)
    def _():
        y_ref[...] = jnp.zeros_like(y_ref)


def _moe_experts(xs, sched, wg, wu, wd, bg, bu, bd, tm, tf):
    rows = xs.shape[0]
    n_exp, d, f = wg.shape
    assert xs.shape[1] * 2 == d and d // tf == f // tf
    steps = sched[0].shape[0]
    w_tile = lambda shape: pl.BlockSpec((None,) + shape, lambda s, e, j, r, ro, jo, fi, va: (e[s], 0, j[s]))
    act = pl.pallas_call(
        _moe_up_kernel,
        grid_spec=pltpu.PrefetchScalarGridSpec(
            num_scalar_prefetch=7,
            grid=(steps,),
            in_specs=[pl.BlockSpec((tm, d // 2), lambda s, e, j, r, ro, jo, fi, va: (r[s], 0)),
                      w_tile((d, tf)), w_tile((d, tf)), w_tile((1, tf)), w_tile((1, tf))],
            out_specs=pl.BlockSpec((tm, tf), lambda s, e, j, r, ro, jo, fi, va: (ro[s], jo[s])),
            scratch_shapes=[pltpu.VMEM((d, tf), BF16), pltpu.VMEM((d, tf), BF16)]),
        out_shape=jax.ShapeDtypeStruct((rows, f), BF16),
        compiler_params=_cparams("arbitrary"),
        name="moe_up",
    )(*sched, xs, wg, wu, bg.reshape(n_exp, 1, f), bu.reshape(n_exp, 1, f))
    return pl.pallas_call(
        _moe_down_kernel,
        grid_spec=pltpu.PrefetchScalarGridSpec(
            num_scalar_prefetch=7,
            grid=(steps,),
            in_specs=[pl.BlockSpec((tm, f), lambda s, e, j, r, ro, jo, fi, va: (r[s], 0)),
                      w_tile((f, tf)), w_tile((1, tf))],
            out_specs=pl.BlockSpec((tm, tf), lambda s, e, j, r, ro, jo, fi, va: (ro[s], jo[s])),
            scratch_shapes=[pltpu.VMEM((f, tf), BF16)]),
        out_shape=jax.ShapeDtypeStruct((rows, d), F32),
        compiler_params=_cparams("arbitrary"),
        name="moe_down",
    )(*sched, act, wd, bd.reshape(n_exp, 1, d))


def _route(top_idx, rank, counts, tm, n_tiles):
    n_tok, k = top_idx.shape
    n_exp = counts.shape[0]
    m = n_tok * k
    padded = (counts + tm - 1) // tm * tm
    pend = jnp.cumsum(padded)
    onehot = top_idx.reshape(m, 1) == jnp.arange(n_exp, dtype=jnp.int32)[None, :]
    dest = jnp.sum(jnp.where(onehot, (pend - padded)[None, :], 0), axis=1) + rank.reshape(m)
    n_blocks = -(-m // tm) + n_exp

    nb = padded // tm
    blk0 = (pend - padded) // tm
    cum = jnp.cumsum(nb * n_tiles)
    total = cum[-1]
    s = jnp.arange(n_tiles * n_blocks, dtype=jnp.int32)
    valid = s < total
    sc = jnp.minimum(s, total - 1)
    e = jnp.minimum(jnp.sum((cum[None, :] <= sc[:, None]).astype(jnp.int32), axis=1), n_exp - 1)
    of_e = e[:, None] == jnp.arange(n_exp, dtype=jnp.int32)[None, :]
    pick = lambda v: jnp.sum(jnp.where(of_e, v[None, :], 0), axis=1)
    nb_e = pick(nb)
    local = sc - (pick(cum) - nb_e * n_tiles)
    nbe = jnp.maximum(nb_e, 1)
    j = local // nbe
    r = pick(blk0) + local % nbe
    first = valid & (local % nbe == 0)
    extra = s - total
    r_out = jnp.where(valid, r, total // n_tiles + extra // n_tiles)
    j_out = jnp.where(valid, j, extra % n_tiles)
    sched = tuple(a.astype(jnp.int32) for a in (e, j, r, r_out, j_out, first, valid))
    return dest.astype(jnp.int32), n_blocks * tm, sched


def _combine_kernel(dest_ref, x1_ref, gate_ref, gt_ref, nw_ref, ys_ref, o_ref, ybuf, sems, *, tt, k):
    i = pl.program_id(0)
    n = pl.num_programs(0)

    def start_gather(tile, slot):
        base = tile * (tt * k)

        def body(tb, carry):
            for u in range(8):
                for j in range(k):
                    row = dest_ref[base + (tb * 8 + u) * k + j]
                    pltpu.make_async_copy(ys_ref.at[row >> 3, pl.ds(row & 7, 1), :],
                                          ybuf.at[slot, j * (tt // 8) + tb, pl.ds(u, 1), :], sems.at[slot]).start()
            return carry

        lax.fori_loop(0, tt // 8, body, 0)

    @pl.when(i == 0)
    def _():
        start_gather(0, 0)

    for nxt in (0, 1):
        @pl.when((i + 1 < n) & ((i + 1) % 2 == nxt))
        def _():
            start_gather(i + 1, nxt)

    slot = i % 2
    pltpu.make_async_copy(ybuf.at[slot], ybuf.at[slot], sems.at[slot]).wait()
    g = gate_ref[...]
    d = o_ref.shape[-1]
    picked = lambda j: ybuf[slot, pl.ds(j * (tt // 8), tt // 8), :, :].reshape(tt, d)
    moe = picked(0) * g[:, 0:1]
    for j in range(1, k):
        moe = moe + picked(j) * g[:, j:j + 1]
    x = x1_ref[...] + gt_ref[...] * moe
    ms = jnp.mean(x * x, axis=-1, keepdims=True)
    o_ref[...] = x * lax.rsqrt(ms + EPS) * nw_ref[...]


def _combine(x1, ys, dest, gates, gt2, nw, k):
    b, t, d = x1.shape
    tt = min(t, 256)
    tpb = t // tt
    row = lambda width: pl.BlockSpec((None, tt, width), lambda i, dest_: (i // tpb, i % tpb, 0))
    return pl.pallas_call(
        functools.partial(_combine_kernel, tt=tt, k=k),
        grid_spec=pltpu.PrefetchScalarGridSpec(
            num_scalar_prefetch=1,
            grid=(b * tpb,),
            in_specs=[row(d), row(LANES),
                      pl.BlockSpec((None, 1, d), lambda i, dest_: (i // tpb, 0, 0)),
                      pl.BlockSpec((1, d), lambda i, dest_: (0, 0)),
                      pl.BlockSpec(memory_space=pl.ANY)],
            out_specs=row(d),
            scratch_shapes=[pltpu.VMEM((2, k * tt // 8, 8, d), F32), pltpu.SemaphoreType.DMA((2,))]),
        out_shape=jax.ShapeDtypeStruct((b, t, d), F32),
        compiler_params=_cparams("arbitrary"),
        name="combine",
    )(dest, x1, gates, gt2, nw.reshape(1, d), ys.reshape(ys.shape[0] // 8, 8, d))


def kernel(x, c, ctx, c_ctx, w_ada, b_ada, norm_mix_w, w_in, hg_lb_f, hg_lb_b, hg_norm_w, gd_conv_w,
           gd_a_log_f, gd_a_log_b, gd_dt_bias_f, gd_dt_bias_b, gd_norm_w, w_out, norm_ffn_w, w_router,
           b_router, w_gate, b_gate, w_up, b_up, w_down, b_down, norm_out_w):
    bsz, t, d = x.shape
    t_ctx = ctx.shape[1]
    n_exp = w_router.shape[-1]
    l = 0
    hg_w = N_HEADS * HEAD_DIM
    n_main = 9 * hg_w

    lb_f = jnp.cumsum(jax.nn.softmax(hg_lb_f.astype(F32), axis=0), axis=0)[l].reshape(1, hg_w)
    lb_b = jnp.cumsum(jax.nn.softmax(hg_lb_b.astype(F32), axis=0), axis=0)[l].reshape(1, hg_w)

    rows = -(-(bsz + 1) // 8) * 8
    cc = jnp.zeros((rows, d), F32).at[:bsz].set(c).at[bsz].set(c_ctx)
    mod = _ada(cc, w_ada[l], b_ada[l])
    sh1, sc1, gt1, sh2, sc2, gt2 = (mod[:bsz, i * d:(i + 1) * d].reshape(bsz, 1, d) for i in range(6))
    csh1 = jnp.broadcast_to(mod[bsz, 0:d].reshape(1, 1, d), (bsz, 1, d))
    csc1 = jnp.broadcast_to(mod[bsz, d:2 * d].reshape(1, 1, d), (bsz, 1, d))

    w_in_b = w_in[l].astype(BF16)
    w_gates = jnp.pad(w_in_b[:, n_main:], ((0, 0), (0, LANES - (w_in_b.shape[1] - n_main))))
    px, gx = _inproj(x, norm_mix_w[l], sc1, sh1, w_in_b, n_main, w_gates)
    pc, gc = _inproj(ctx, norm_mix_w[l], csc1, csh1, w_in_b, n_main, w_gates)

    hg_nw = hg_norm_w[l].reshape(1, HEAD_DIM)
    hs_f, hs_b = _hgrn_states(pc, lb_f, lb_b)
    mix_h = _hgrn_out(px, lb_f, lb_b, hg_nw, hs_f, hs_b)

    hp = jnp.stack([gd_a_log_f[l], gd_dt_bias_f[l], gd_a_log_b[l], gd_dt_bias_b[l]], axis=1)
    hp = jnp.broadcast_to(jnp.pad(hp, ((0, 0), (0, 4)))[:, :, None], (N_HEADS, 8, HEAD_DIM)).astype(F32)
    gd_nw = gd_norm_w[l].reshape(1, HEAD_DIM)
    gs_f, gs_b = _gdn_call(pc, gc, gd_conv_w[l], hp, gd_nw, None, None, t_ctx, False, 5)
    mix_g = _gdn_call(px, gx, gd_conv_w[l], hp, gd_nw, gs_f, gs_b, GRID_W, True, 5)

    w_out_b = w_out[l].astype(BF16)
    wr = jnp.pad(w_router[l], ((0, 0), (0, LANES - n_exp)))
    wr_hi = wr.astype(BF16)
    wr_lo = (wr - wr_hi.astype(F32)).astype(BF16)
    br = jnp.full((1, LANES), NEG_BIG, F32).at[0, :n_exp].set(b_router[l])
    x1, h2, idx_pad, gate_pad, cnt = _outproj(mix_h, mix_g, x, gt1, sc2, sh2, norm_ffn_w[l],
                                              w_out_b.reshape(2, hg_w, d), wr_hi, wr_lo, br)

    n_tok = bsz * t
    idx_pad = idx_pad.reshape(n_tok, LANES)
    dest, n_rows, sched = _route(idx_pad[:, :TOP_K], idx_pad[:, TOP_K:2 * TOP_K],
                                 cnt[0, :n_exp].astype(jnp.int32), MOE_TM, w_gate.shape[-1] // MOE_TF)
    xs = _dispatch(h2.reshape(n_tok, d // 2), dest, n_rows, TOP_K)
    ys = _moe_experts(xs, sched, w_gate[l], w_up[l], w_down[l], b_gate[l], b_up[l], b_down[l], MOE_TM, MOE_TF)

    return _combine(x1, ys, dest, gate_pad, gt2, norm_out_w, TOP_K)
```

```python
import functools

import jax
import jax.numpy as jnp
from jax import lax
from jax.experimental import pallas as pl
from jax.experimental.pallas import tpu as pltpu

F32 = jnp.float32
BF16 = jnp.bfloat16

EPS = 1e-6
CHUNK = 64
HEAD_DIM = 128
N_HEADS = 8
GRID_W = 64
TOP_K = 4
SWIGLU_LIMIT = 7.0
SWIGLU_ALPHA = 1.702
LANES = 128
NEG_BIG = -1e30

HGRN_GROUP = 4
GDN_GROUP = 4
GDN_INTRA_GROUP = 8
MOE_TM = 512
MOE_TF = 1024
VMEM_LIMIT = 56 * 1024 * 1024


def _cparams(*sem):
    return pltpu.CompilerParams(dimension_semantics=sem, vmem_limit_bytes=VMEM_LIMIT)


def _dot(a, b):
    return jnp.dot(a, b, preferred_element_type=F32)


def _dot_nt(a, b):
    return lax.dot_general(a, b, (((1,), (1,)), ((), ())), preferred_element_type=F32)


def _split2(x):
    hi = x.astype(BF16)
    return hi, (x - hi.astype(F32)).astype(BF16)


def _split3(x):
    hi = x.astype(BF16)
    rest = x - hi.astype(F32)
    mid = rest.astype(BF16)
    return hi, mid, (rest - mid.astype(F32)).astype(BF16)


def _dot_split(a, b):
    ah, al = a
    bh, bl = b
    return _dot(ah, bh) + (_dot(ah, bl) + _dot(al, bh))


def _sigmoid(x):
    return 1.0 / (1.0 + jnp.exp(-x))


def _ada_kernel(c_ref, w_ref, b_ref, o_ref):
    c = c_ref[...]
    s = (c * _sigmoid(c)).astype(BF16)
    o_ref[...] = _dot(s, w_ref[...].astype(BF16)) + b_ref[...]


def _ada(cc, w_ada, b_ada):
    rows, d = cc.shape
    n = w_ada.shape[1]
    tn = 1024
    return pl.pallas_call(
        _ada_kernel,
        grid=(n // tn,),
        in_specs=[pl.BlockSpec((rows, d), lambda j: (0, 0)),
                  pl.BlockSpec((d, tn), lambda j: (0, j)),
                  pl.BlockSpec((1, tn), lambda j: (0, j))],
        out_specs=pl.BlockSpec((rows, tn), lambda j: (0, j)),
        out_shape=jax.ShapeDtypeStruct((rows, n), F32),
        compiler_params=_cparams("arbitrary"),
        name="ada",
    )(cc, w_ada, b_ada.reshape(1, n))


def _modulated_norm(x, nw, sc, sh):
    ms = jnp.mean(x * x, axis=-1, keepdims=True)
    return (x * lax.rsqrt(ms + EPS) * nw) * (1.0 + sc) + sh


def _inproj_kernel(x_ref, nw_ref, sc_ref, sh_ref, w_ref, wg_ref, o_ref, og_ref, h_scr):
    @pl.when(pl.program_id(1) == 0)
    def _():
        h = _modulated_norm(x_ref[...], nw_ref[...], sc_ref[...], sh_ref[...]).astype(BF16)
        h_scr[...] = h
        og_ref[...] = _dot(h, wg_ref[...])

    o_ref[...] = _dot(h_scr[...], w_ref[...])


def _inproj(x, nw, sc, sh, w_all, n, w_gates):
    b, t, d = x.shape
    w_main = w_all
    tm = min(t, 1024)
    tn = 1024
    tpb = t // tm
    assert n % tn == 0
    return pl.pallas_call(
        _inproj_kernel,
        grid=(b * tpb, n // tn),
        in_specs=[pl.BlockSpec((None, tm, d), lambda i, j: (i // tpb, i % tpb, 0)),
                  pl.BlockSpec((1, d), lambda i, j: (0, 0)),
                  pl.BlockSpec((None, 1, d), lambda i, j: (i // tpb, 0, 0)),
                  pl.BlockSpec((None, 1, d), lambda i, j: (i // tpb, 0, 0)),
                  pl.BlockSpec((d, tn), lambda i, j: (0, j)),
                  pl.BlockSpec((d, LANES), lambda i, j: (0, 0))],
        out_specs=[pl.BlockSpec((None, tm, tn), lambda i, j: (i // tpb, i % tpb, j)),
                   pl.BlockSpec((None, tm, LANES), lambda i, j: (i // tpb, i % tpb, 0))],
        out_shape=[jax.ShapeDtypeStruct((b, t, n), F32),
                   jax.ShapeDtypeStruct((b, t, LANES), F32)],
        scratch_shapes=[pltpu.VMEM((tm, d), BF16)],
        compiler_params=_cparams("parallel", "arbitrary"),
        name="inproj",
    )(x, nw.reshape(1, d), sc, sh, w_main, w_gates)


def _chunk_masks(rev, wide=False):
    shape = (CHUNK, 2 * CHUNK if wide else CHUNK)
    r = lax.broadcasted_iota(jnp.int32, shape, 0)
    c = lax.broadcasted_iota(jnp.int32, shape, 1) & (CHUNK - 1)
    incl = (c >= r) if rev else (c <= r)
    strict = (c > r) if rev else (c < r)
    return incl, strict


def _gated_head_norm(o, z, nw):
    ms = jnp.mean(o * o, axis=-1, keepdims=True)
    return (o * lax.rsqrt(ms + EPS) * nw) * (z * _sigmoid(z))


def _group_slices(g, group, n_chunks):
    idxs, revs = [], []
    for i in range(group):
        n = g * group + i
        idxs += [n, n_chunks - 1 - n]
        revs += [False, True]
    return [pl.ds(pl.multiple_of(ix * CHUNK, CHUNK), CHUNK) for ix in idxs], idxs, revs


def _scan_group(n_chunks, want, two_phase):
    g = want
    while g > 1 and (n_chunks % g or (two_phase and (n_chunks // g) % 2)):
        g //= 2
    assert n_chunks % g == 0 and not (two_phase and (n_chunks // g) % 2)
    return g


def _gla_group(zfs, vs, qs, lbs, revs):
    n = len(zfs)
    incl = [_chunk_masks(r)[0] for r in revs]
    logf, k = [], []
    for i in range(n):
        one_m = 1.0 - lbs[i]
        logf.append(jnp.log(lbs[i] + one_m * _sigmoid(zfs[i])))
        k.append(one_m * _sigmoid(-zfs[i]))
    cum = []
    for i in range(n):
        tri = incl[i].astype(BF16)
        hi, mid, lo = _split3(logf[i])
        parts = _dot(tri, jnp.concatenate([hi, mid, lo], axis=1))
        cum.append(parts[:, :HEAD_DIM] + (parts[:, HEAD_DIM:2 * HEAD_DIM] + parts[:, 2 * HEAD_DIM:]))
    last = [cum[i][0:1] if revs[i] else cum[i][CHUNK - 1:CHUNK] for i in range(n)]
    u_t = [_dot(vs[i].T.astype(BF16), (k[i] * jnp.exp(last[i] - cum[i])).astype(BF16)) for i in range(n)]
    dec = [jnp.exp(x) for x in last]
    if qs is None:
        return u_t, dec, None, None
    scores = []
    for i in range(n):
        mid_row = CHUNK // 2 - 1 if revs[i] else CHUNK // 2
        ref = cum[i][mid_row:mid_row + 1]
        sc = _dot_nt((qs[i] * jnp.exp(cum[i] - ref)).astype(BF16), (k[i] * jnp.exp(ref - cum[i])).astype(BF16))
        scores.append(jnp.where(incl[i], sc, 0.0).astype(BF16))
    o_intra = [_dot(scores[i], vs[i].astype(BF16)) for i in range(n)]
    qe = [(qs[i] * jnp.exp(cum[i])).astype(BF16) for i in range(n)]
    return u_t, dec, o_intra, qe


def _hgrn_out_kernel(q_ref, ff_ref, fb_ref, v_ref, g_ref, lbf_ref, lbb_ref, nw_ref, s0f_ref, s0b_ref,
                     o_ref, o_scr, sf_scr, sb_scr, *, n_chunks, group):
    sf_scr[...] = s0f_ref[...]
    sb_scr[...] = s0b_ref[...]
    lbf, lbb, nw = lbf_ref[...], lbb_ref[...], nw_ref[...]

    def body(g, carry, final):
        sls, _, revs = _group_slices(g, group, n_chunks)
        zfs = [(fb_ref if r else ff_ref)[sl, :] for sl, r in zip(sls, revs)]
        vs = [v_ref[sl, :] for sl in sls]
        qs = [q_ref[sl, :] for sl in sls]
        u_t, dec, o_intra, qe = _gla_group(zfs, vs, qs, [lbb if r else lbf for r in revs], revs)
        for i, (sl, rev) in enumerate(zip(sls, revs)):
            s_scr = sb_scr if rev else sf_scr
            st = s_scr[...]
            o = o_intra[i] + _dot_nt(qe[i], st.astype(BF16))
            s_scr[...] = st * dec[i] + u_t[i]
            if final:
                o_ref[sl, :] = _gated_head_norm(o_scr[sl, :] + o, g_ref[sl, :], nw).astype(o_ref.dtype)
            else:
                o_scr[sl, :] = o
        return carry

    trips = n_chunks // group
    lax.fori_loop(0, trips // 2, functools.partial(body, final=False), 0)
    lax.fori_loop(trips // 2, trips, functools.partial(body, final=True), 0)


def _hgrn_state_kernel(ff_ref, fb_ref, v_ref, lbf_ref, lbb_ref, sf_ref, sb_ref, *, n_chunks, group):
    sf_ref[...] = jnp.zeros_like(sf_ref)
    sb_ref[...] = jnp.zeros_like(sb_ref)
    lbf, lbb = lbf_ref[...], lbb_ref[...]

    def body(g, carry):
        sls, _, revs = _group_slices(g, group, n_chunks)
        zfs = [(fb_ref if r else ff_ref)[sl, :] for sl, r in zip(sls, revs)]
        vs = [v_ref[sl, :] for sl in sls]
        u_t, dec, _, _ = _gla_group(zfs, vs, None, [lbb if r else lbf for r in revs], revs)
        for i, rev in enumerate(revs):
            s_ref = sb_ref if rev else sf_ref
            s_ref[...] = s_ref[...] * dec[i] + u_t[i]
        return carry

    lax.fori_loop(0, n_chunks // group, body, 0)


def _col_spec(t, section):
    return pl.BlockSpec((None, t, HEAD_DIM), lambda b, h: (b, 0, section * N_HEADS + h))


def _head_row_spec(offset=0):
    return pl.BlockSpec((1, HEAD_DIM), lambda b, h: (0, offset + h))


_STATE_SPEC = pl.BlockSpec((None, None, HEAD_DIM, HEAD_DIM), lambda b, h: (b, h, 0, 0))


def _hgrn_states(pc, lbf, lbb):
    b, t, _ = pc.shape
    n_chunks = t // CHUNK
    shp = jax.ShapeDtypeStruct((b, N_HEADS, HEAD_DIM, HEAD_DIM), F32)
    return pl.pallas_call(
        functools.partial(_hgrn_state_kernel, n_chunks=n_chunks, group=_scan_group(n_chunks, HGRN_GROUP, False)),
        grid=(b, N_HEADS),
        in_specs=[_col_spec(t, 1), _col_spec(t, 2), _col_spec(t, 3), _head_row_spec(), _head_row_spec()],
        out_specs=[_STATE_SPEC, _STATE_SPEC],
        out_shape=[shp, shp],
        compiler_params=_cparams("parallel", "parallel"),
        name="hgrn_ctx",
    )(pc, pc, pc, lbf, lbb)


def _hgrn_out(px, lbf, lbb, nw, s0f, s0b):
    b, t, _ = px.shape
    n_chunks = t // CHUNK
    return pl.pallas_call(
        functools.partial(_hgrn_out_kernel, n_chunks=n_chunks, group=_scan_group(n_chunks, HGRN_GROUP, True)),
        grid=(b, N_HEADS),
        in_specs=[_col_spec(t, 0), _col_spec(t, 1), _col_spec(t, 2), _col_spec(t, 3), _col_spec(t, 4),
                  _head_row_spec(), _head_row_spec(),
                  pl.BlockSpec((1, HEAD_DIM), lambda b_, h: (0, 0)),
                  _STATE_SPEC, _STATE_SPEC],
        out_specs=pl.BlockSpec((None, t, HEAD_DIM), lambda b_, h: (b_, 0, h)),
        out_shape=jax.ShapeDtypeStruct((b, t, N_HEADS * HEAD_DIM), BF16),
        scratch_shapes=[pltpu.VMEM((t, HEAD_DIM), F32),
                        pltpu.VMEM((HEAD_DIM, HEAD_DIM), F32),
                        pltpu.VMEM((HEAD_DIM, HEAD_DIM), F32)],
        compiler_params=_cparams("parallel", "parallel"),
        name="hgrn_x",
    )(px, px, px, px, px, lbf, lbb, nw, s0f, s0b)


def _conv_silu(a, w, period):
    rows = a.shape[0]
    pos = lax.broadcasted_iota(jnp.int32, a.shape, 0) % period
    prev = jnp.where(pos == 0, 0.0, pltpu.roll(a, 1, axis=0))
    nxt = jnp.where(pos == period - 1, 0.0, pltpu.roll(a, rows - 1, axis=0))
    y = prev * w[0:1] + a * w[1:2] + nxt * w[2:3]
    return y * _sigmoid(y)


def _l2norm(x):
    return x * lax.rsqrt(jnp.sum(x * x, axis=-1, keepdims=True) + EPS)


def _softplus(x):
    return jnp.maximum(x, 0.0) + jnp.log1p(jnp.exp(-jnp.abs(x)))


def _unit_tri_inverses(tris):
    row = lax.broadcasted_iota(jnp.int32, (CHUNK, 2 * CHUNK), 0)
    lane = lax.broadcasted_iota(jnp.int32, (CHUNK, 2 * CHUNK), 1)
    left = lane < CHUNK
    eye = jnp.where((lane & (CHUNK - 1)) == row, 1.0, 0.0)
    zeros = jnp.zeros((CHUNK, 2 * CHUNK), BF16)

    def lhs(hl):
        return jnp.where(left, hl[0], hl[1])

    def rhs(hl):
        top = jnp.concatenate([hl[0], hl[1]], axis=1)
        return jnp.concatenate([top, jnp.concatenate([hl[0], zeros], axis=1)], axis=0)

    def fold(o):
        return o[:, :2 * CHUNK] + o[:, 2 * CHUNK:]

    ps = [_split2(-a) for a in tris]
    invs = [eye - a for a in tris]
    ps = [_split2(fold(_dot(lhs(p), rhs(p)))) for p in ps]
    for _ in range(CHUNK.bit_length() - 3):
        outs = [fold(_dot(jnp.concatenate([lhs(p), lhs(_split2(inv))], axis=0), rhs(p)))
                for p, inv in zip(ps, invs)]
        invs = [inv + o[CHUNK:] for inv, o in zip(invs, outs)]
        ps = [_split2(o[:CHUNK]) for o in outs]
    invs = [inv + fold(_dot(lhs(_split2(inv)), rhs(p))) for inv, p in zip(invs, ps)]
    return [inv[:, :CHUNK] for inv in invs]


def _gdn_group(qs, ks, vs, a_cs, b_cs, a_rs, alogs, dtbs, revs):
    n = len(ks)
    left = lax.broadcasted_iota(jnp.int32, (CHUNK, 2 * CHUNK), 1) < CHUNK
    dmask, ecum, beta, eend, dec, strict = [], [], [], [], [], []
    for i in range(n):
        incl, st = _chunk_masks(revs[i], wide=True)
        incl_t = _chunk_masks(not revs[i], wide=True)[0]
        scale = -jnp.exp(alogs[i])
        g_c = scale * _softplus(a_cs[i] + dtbs[i])
        g_r = scale * _softplus(a_rs[i] + dtbs[i])
        cum_c = jnp.sum(jnp.where(incl & left, g_r, 0.0), axis=1, keepdims=True)
        cum_r = jnp.sum(jnp.where(incl_t, g_c, 0.0), axis=0, keepdims=True)
        dmask.append(jnp.exp(jnp.where(incl, cum_c - cum_r, -jnp.inf)))
        ecum.append(jnp.exp(cum_c))
        beta.append(_sigmoid(b_cs[i]))
        last = cum_c[0:1] if revs[i] else cum_c[CHUNK - 1:CHUNK]
        eend.append(jnp.exp(last - cum_c))
        dec.append(jnp.exp(last))
        strict.append(st)
    kbf = [k.astype(BF16) for k in ks]
    kb = [ks[i] * beta[i] for i in range(n)]
    kk = [_dot_nt(kb[i].astype(BF16), jnp.concatenate([kbf[i], kbf[i]], axis=0)) for i in range(n)]
    tris = [jnp.where(strict[i], kk[i] * dmask[i], 0.0) for i in range(n)]
    tinv = [t.astype(BF16) for t in _unit_tri_inverses(tris)]
    dmask = [m[:, :CHUNK] for m in dmask]
    u = [_dot(tinv[i], (vs[i] * beta[i]).astype(BF16)).astype(BF16) for i in range(n)]
    w = [_dot(tinv[i], (kb[i] * ecum[i]).astype(BF16)).astype(BF16) for i in range(n)]
    ke_t = [(ks[i] * eend[i]).T.astype(BF16) for i in range(n)]
    mp = [_dot(ke_t[i], w[i]) for i in range(n)]
    cc = [_dot(ke_t[i], u[i]) for i in range(n)]
    if qs is None:
        return mp, cc, dec, None, None
    attn = [(_dot_nt(qs[i].astype(BF16), kbf[i]) * dmask[i]).astype(BF16) for i in range(n)]
    qp = [qs[i] * ecum[i] - _dot(attn[i], w[i]) for i in range(n)]
    oi = [_dot(attn[i], u[i]) for i in range(n)]
    return mp, cc, dec, qp, oi


def _gdn_kernel(*refs, n_chunks, period, prep_rows, group, igroup, with_out):
    if with_out:
        (q_ref, k_ref, v_ref, z_ref, gc_ref, gr_ref, wq_ref, wk_ref, wv_ref, hp_ref, nw_ref,
         s0f_ref, s0b_ref, o_ref, qn, kn, vn, mp_s, cc_s, dc_s, qp_s, o_scr, sf, sb) = refs
    else:
        (k_ref, v_ref, gc_ref, gr_ref, wk_ref, wv_ref, hp_ref,
         sf, sb, kn, vn, mp_s, cc_s, dc_s) = refs
        q_ref = None

    t = n_chunks * CHUNK

    def prep(i, carry):
        sl = pl.ds(pl.multiple_of(i * prep_rows, prep_rows), prep_rows)
        kn[sl, :] = _l2norm(_conv_silu(k_ref[sl, :], wk_ref[...], period))
        vn[sl, :] = _conv_silu(v_ref[sl, :], wv_ref[...], period)
        if with_out:
            qn[sl, :] = _l2norm(_conv_silu(q_ref[sl, :], wq_ref[...], period)) * (HEAD_DIM ** -0.5)
        return carry

    lax.fori_loop(0, t // prep_rows, prep, 0)

    hp = hp_ref[...]

    def intra(g, carry):
        qs, ks, vs, a_cs, b_cs, a_rs, alogs, dtbs, revs, where = [], [], [], [], [], [], [], [], [], []
        for i in range(igroup):
            n = g * igroup + i
            sl = pl.ds(pl.multiple_of(n * CHUNK, CHUNK), CHUNK)
            gc = gc_ref[sl, :]
            gr = gr_ref[n]
            for d, rev in enumerate((False, True)):
                if with_out:
                    qs.append(qn[sl, :])
                ks.append(kn[sl, :])
                vs.append(vn[sl, :])
                a_cs.append(gc[:, d:d + 1])
                b_cs.append(gc[:, 2 + d:3 + d])
                a_rs.append(gr[d:d + 1, :])
                alogs.append(hp[2 * d:2 * d + 1, 0:1])
                dtbs.append(hp[2 * d + 1:2 * d + 2, 0:1])
                revs.append(rev)
                where.append((d, n, sl))
        mp, cc, dec, qp, oi = _gdn_group(qs if with_out else None, ks, vs, a_cs, b_cs, a_rs, alogs, dtbs, revs)
        for i, (d, n, sl) in enumerate(where):
            mp_s[d, n] = mp[i].astype(BF16)
            cc_s[d, n] = cc[i]
            dc_s[d, n] = jnp.broadcast_to(dec[i], (8, HEAD_DIM))
            if with_out:
                qp_s[d, sl, :] = qp[i].astype(BF16)
                if d == 1:
                    o_scr[sl, :] = oi[i - 1] + oi[i]
        return carry

    lax.fori_loop(0, n_chunks // igroup, intra, 0)

    if with_out:
        sf[...] = s0f_ref[...]
        sb[...] = s0b_ref[...]
        nw = nw_ref[...]
    else:
        sf[...] = jnp.zeros_like(sf)
        sb[...] = jnp.zeros_like(sb)

    def scan(g, carry, final):
        sls, idxs, revs = _group_slices(g, group, n_chunks)
        for sl, idx, rev in zip(sls, idxs, revs):
            d = int(rev)
            s_ref = sb if rev else sf
            s = s_ref[...]
            sbf = s.astype(BF16)
            s_ref[...] = s * dc_s[d, idx][0:1, :] - _dot(mp_s[d, idx], sbf) + cc_s[d, idx]
            if with_out:
                o = o_scr[sl, :] + _dot(qp_s[d, sl, :], sbf)
                if final:
                    o_ref[sl, :] = _gated_head_norm(o, z_ref[sl, :], nw).astype(o_ref.dtype)
                else:
                    o_scr[sl, :] = o
        return carry

    trips = n_chunks // group
    if with_out:
        lax.fori_loop(0, trips // 2, functools.partial(scan, final=False), 0)
        lax.fori_loop(trips // 2, trips, functools.partial(scan, final=True), 0)
    else:
        lax.fori_loop(0, trips, functools.partial(scan, final=False), 0)


def _gdn_call(p, gates, conv_w, hp, nw, s0f, s0b, period, with_out, sec0):
    b, t, _ = p.shape
    n_chunks = t // CHUNK
    prep_rows = max(period, min(t, 256))
    assert prep_rows % period == 0 and t % prep_rows == 0
    group = _scan_group(n_chunks, GDN_GROUP, with_out)
    g4 = gates[:, :, :4 * N_HEADS].reshape(b, t, 4, N_HEADS)
    g_col = g4.transpose(0, 3, 1, 2)
    g_row = jnp.pad(g4.transpose(0, 3, 2, 1), ((0, 0), (0, 0), (0, 4), (0, 0)))
    g_row = g_row.reshape(b, N_HEADS, 8, n_chunks, CHUNK).transpose(0, 1, 3, 2, 4)
    g_row = jnp.concatenate([g_row, g_row], axis=-1)

    gc_spec = pl.BlockSpec((None, None, t, 4), lambda b_, h: (b_, h, 0, 0))
    gr_spec = pl.BlockSpec((None, None, n_chunks, 8, 2 * CHUNK), lambda b_, h: (b_, h, 0, 0, 0))
    conv_spec = lambda sec: pl.BlockSpec((3, HEAD_DIM), lambda b_, h: (0, sec * N_HEADS + h))
    hp_spec = pl.BlockSpec((None, 8, HEAD_DIM), lambda b_, h: (h, 0, 0))
    seq = lambda: pltpu.VMEM((t, HEAD_DIM), F32)
    mats = lambda dt: pltpu.VMEM((2, n_chunks, HEAD_DIM, HEAD_DIM), dt)
    dc = pltpu.VMEM((2, n_chunks, 8, HEAD_DIM), F32)
    state = lambda: pltpu.VMEM((HEAD_DIM, HEAD_DIM), F32)
    kern = functools.partial(_gdn_kernel, n_chunks=n_chunks, period=period, prep_rows=prep_rows,
                             group=group, igroup=_scan_group(n_chunks, GDN_INTRA_GROUP, False),
                             with_out=with_out)
    if with_out:
        return pl.pallas_call(
            kern,
            grid=(b, N_HEADS),
            in_specs=[_col_spec(t, sec0), _col_spec(t, sec0 + 1), _col_spec(t, sec0 + 2), _col_spec(t, sec0 + 3),
                      gc_spec, gr_spec, conv_spec(0), conv_spec(1), conv_spec(2), hp_spec,
                      pl.BlockSpec((1, HEAD_DIM), lambda b_, h: (0, 0)), _STATE_SPEC, _STATE_SPEC],
            out_specs=pl.BlockSpec((None, t, HEAD_DIM), lambda b_, h: (b_, 0, h)),
            out_shape=jax.ShapeDtypeStruct((b, t, N_HEADS * HEAD_DIM), BF16),
            scratch_shapes=[seq(), seq(), seq(), mats(BF16), mats(F32), dc,
                            pltpu.VMEM((2, t, HEAD_DIM), BF16), seq(), state(), state()],
            compiler_params=_cparams("parallel", "parallel"),
            name="gdn_x",
        )(p, p, p, p, g_col, g_row, conv_w, conv_w, conv_w, hp, nw, s0f, s0b)
    shp = jax.ShapeDtypeStruct((b, N_HEADS, HEAD_DIM, HEAD_DIM), F32)
    return pl.pallas_call(
        kern,
        grid=(b, N_HEADS),
        in_specs=[_col_spec(t, sec0 + 1), _col_spec(t, sec0 + 2), gc_spec, gr_spec,
                  conv_spec(1), conv_spec(2), hp_spec],
        out_specs=[_STATE_SPEC, _STATE_SPEC],
        out_shape=[shp, shp],
        scratch_shapes=[seq(), seq(), mats(BF16), mats(F32), dc],
        compiler_params=_cparams("parallel", "parallel"),
        name="gdn_ctx",
    )(p, p, g_col, g_row, conv_w, conv_w, hp)


def _outproj_kernel(mh_ref, mg_ref, x_ref, gt_ref, sc_ref, sh_ref, nw_ref, woh_ref, wog_ref,
                    wrh_ref, wrl_ref, br_ref, x1_ref, h2_ref, idx_ref, gate_ref, cnt_ref):
    y = _dot(mh_ref[...], woh_ref[...]) + _dot(mg_ref[...], wog_ref[...])
    x1 = x_ref[...] + gt_ref[...] * y
    x1_ref[...] = x1
    h = _modulated_norm(x1, nw_ref[...], sc_ref[...], sh_ref[...])
    hh = h.astype(BF16)
    half = hh.shape[1] // 2
    hi_bits = lax.bitcast_convert_type(hh[:, :half].astype(F32), jnp.uint32)
    lo_bits = lax.bitcast_convert_type(hh[:, half:].astype(F32), jnp.uint32)
    h2_ref[...] = hi_bits | (lo_bits >> 16)
    hl = (h - hh.astype(F32)).astype(BF16)
    wrh = wrh_ref[...]
    logits = _dot(hh, wrh) + _dot(hl, wrh) + _dot(hh, wrl_ref[...]) + br_ref[...]
    lane = lax.broadcasted_iota(jnp.int32, logits.shape, 1).astype(F32)
    vals, idxs = [], []
    for _ in range(TOP_K):
        m = jnp.max(logits, axis=-1, keepdims=True)
        i = jnp.min(jnp.where(logits == m, lane, float(LANES)), axis=-1, keepdims=True)
        vals.append(m)
        idxs.append(i)
        logits = jnp.where(lane == i, -jnp.inf, logits)
    es = [jnp.exp(v - vals[0]) for v in vals]
    inv = 1.0 / functools.reduce(lambda a, b_: a + b_, es)

    @pl.when(pl.program_id(0) == 0)
    def _():
        cnt_ref[...] = jnp.zeros_like(cnt_ref)

    onehots = [lane == i for i in idxs]
    picked = functools.reduce(lambda a, b_: a | b_, onehots)
    tm = lane.shape[0]
    before = (lax.broadcasted_iota(jnp.int32, (tm, tm), 1) < lax.broadcasted_iota(jnp.int32, (tm, tm), 0))
    prior = cnt_ref[...] + _dot(before.astype(BF16), picked.astype(BF16))
    ranks = [jnp.sum(jnp.where(oh, prior, 0.0), axis=-1, keepdims=True) for oh in onehots]
    cnt_ref[...] += jnp.sum(picked.astype(F32), axis=0, keepdims=True)

    idx_out = jnp.zeros(lane.shape, F32)
    gate_out = jnp.zeros(lane.shape, F32)
    for k in range(TOP_K):
        idx_out = jnp.where(lane == k, idxs[k], idx_out)
        idx_out = jnp.where(lane == TOP_K + k, ranks[k], idx_out)
        gate_out = jnp.where(lane == k, es[k] * inv, gate_out)
    idx_ref[...] = idx_out.astype(jnp.int32)
    gate_ref[...] = gate_out


def _outproj(mix_h, mix_g, x, gt1, sc2, sh2, nw, w_out2, wr_hi, wr_lo, br):
    b, t, d = x.shape
    w = mix_h.shape[-1]
    tm = min(t, 512)
    tpb = t // tm
    row = lambda width: pl.BlockSpec((None, tm, width), lambda i: (i // tpb, i % tpb, 0))
    per_b = pl.BlockSpec((None, 1, d), lambda i: (i // tpb, 0, 0))
    const = lambda r, c: pl.BlockSpec((r, c), lambda i: (0, 0))
    return pl.pallas_call(
        _outproj_kernel,
        grid=(b * tpb,),
        in_specs=[row(w), row(w), row(d), per_b, per_b, per_b, const(1, d),
                  pl.BlockSpec((None, w, d), lambda i: (0, 0, 0)), pl.BlockSpec((None, w, d), lambda i: (1, 0, 0)),
                  const(d, LANES), const(d, LANES), const(1, LANES)],
        out_specs=[row(d), row(d // 2), row(LANES), row(LANES), const(1, LANES)],
        out_shape=[jax.ShapeDtypeStruct((b, t, d), F32), jax.ShapeDtypeStruct((b, t, d // 2), jnp.uint32),
                   jax.ShapeDtypeStruct((b, t, LANES), jnp.int32), jax.ShapeDtypeStruct((b, t, LANES), F32),
                   jax.ShapeDtypeStruct((1, LANES), F32)],
        compiler_params=_cparams("arbitrary"),
        name="outproj",
    )(mix_h, mix_g, x, gt1, sc2, sh2, nw.reshape(1, d), w_out2, w_out2, wr_hi, wr_lo, br)


def _dispatch_kernel(dest_ref, zrow_ref, h_ref, xs_ref, zbuf, sem, zsem, *, tt, k, tm, n_exp):
    @pl.when(pl.program_id(0) == 0)
    def _():
        zbuf[...] = jnp.zeros_like(zbuf)
        zero_block = lambda e: pltpu.make_async_copy(
            zbuf, xs_ref.at[pl.ds(pl.multiple_of(zrow_ref[e], tm), tm), :], zsem)
        for e in range(n_exp):
            @pl.when(zrow_ref[e] >= 0)
            def _():
                zero_block(e).start()
        for e in range(n_exp):
            @pl.when(zrow_ref[e] >= 0)
            def _():
                zero_block(e).wait()

        def unused_block(b):
            return pltpu.make_async_copy(zbuf, xs_ref.at[pl.ds(pl.multiple_of(b * tm, tm), tm), :], zsem)

        def start_unused(b, carry):
            unused_block(b).start()
            return carry

        def wait_unused(b, carry):
            unused_block(b).wait()
            return carry

        lax.fori_loop(zrow_ref[n_exp], xs_ref.shape[0] // tm, start_unused, 0)
        lax.fori_loop(zrow_ref[n_exp], xs_ref.shape[0] // tm, wait_unused, 0)

    base = pl.program_id(0) * (tt * k)

    def body(tb, carry):
        t0 = pl.multiple_of(tb * 8, 8)
        for u in range(8):
            for kk in range(k):
                row = dest_ref[base + (t0 + u) * k + kk]
                pltpu.make_async_copy(h_ref.at[pl.ds(t0 + u, 1), :], xs_ref.at[pl.ds(row, 1), :], sem).start()
        return carry

    lax.fori_loop(0, tt // 8, body, 0)
    done = xs_ref.at[pl.ds(0, tt * k), :]
    pltpu.make_async_copy(done, done, sem).wait()


def _dispatch(h2, dest, zrow, rows, k, tm):
    n_tok, w = h2.shape
    tt = min(n_tok, 512)
    n_exp = zrow.shape[0] - 1
    return pl.pallas_call(
        functools.partial(_dispatch_kernel, tt=tt, k=k, tm=tm, n_exp=n_exp),
        grid_spec=pltpu.PrefetchScalarGridSpec(
            num_scalar_prefetch=2,
            grid=(n_tok // tt,),
            in_specs=[pl.BlockSpec((tt, w), lambda i, dest_, zrow_: (i, 0))],
            out_specs=pl.BlockSpec(memory_space=pl.ANY),
            scratch_shapes=[pltpu.VMEM((tm, w), h2.dtype), pltpu.SemaphoreType.DMA(()),
                            pltpu.SemaphoreType.DMA(())]),
        out_shape=jax.ShapeDtypeStruct((rows, w), h2.dtype),
        compiler_params=_cparams("arbitrary"),
        name="dispatch",
    )(dest, zrow, h2)


def _cast_tile(src, dst):
    rows = 256

    def body(i, carry):
        sl = pl.ds(pl.multiple_of(i * rows, rows), rows)
        dst[sl, :] = src[sl, :].astype(dst.dtype)
        return carry

    lax.fori_loop(0, src.shape[0] // rows, body, 0)


S_E, S_J, S_R, S_RO, S_JO, S_FIRST, S_VALID, S_NE, S_NJ, S_SLOT = range(10)


def _weight_group_prefetch(sc_ref, copies, on_ready):
    s = pl.program_id(0)

    @pl.when(s == 0)
    def _():
        for c in copies(sc_ref[S_E, 0], sc_ref[S_J, 0], 0):
            c.start()

    @pl.when(sc_ref[S_FIRST, s] == 1)
    def _():
        slot = sc_ref[S_SLOT, s]

        @pl.when(sc_ref[S_NE, s] >= 0)
        def _():
            for c in copies(sc_ref[S_NE, s], sc_ref[S_NJ, s], 1 - slot):
                c.start()

        for c in copies(sc_ref[S_E, s], sc_ref[S_J, s], slot):
            c.wait()
        on_ready(slot)


def _moe_up_kernel(sc_ref, x_ref, bg_ref, bu_ref, wg_hbm, wu_hbm, act_ref, stage, wg_b, wu_b, sems, *, tf):
    s = pl.program_id(0)

    def copies(e, j, slot):
        cols = pl.ds(pl.multiple_of(j * tf, tf), tf)
        return [pltpu.make_async_copy(wg_hbm.at[e, :, cols], stage.at[slot, 0], sems.at[slot, 0]),
                pltpu.make_async_copy(wu_hbm.at[e, :, cols], stage.at[slot, 1], sems.at[slot, 1])]

    def on_ready(slot):
        _cast_tile(stage.at[slot, 0], wg_b)
        _cast_tile(stage.at[slot, 1], wu_b)

    _weight_group_prefetch(sc_ref, copies, on_ready)

    @pl.when(sc_ref[S_VALID, s] == 1)
    def _():
        xp = x_ref[...]
        half = xp.shape[1]
        xa = lax.bitcast_convert_type(xp & jnp.uint32(0xFFFF0000), F32).astype(BF16)
        xb = lax.bitcast_convert_type(xp << 16, F32).astype(BF16)
        gate = _dot(xa, wg_b[:half, :]) + _dot(xb, wg_b[half:, :]) + bg_ref[...]
        up = _dot(xa, wu_b[:half, :]) + _dot(xb, wu_b[half:, :]) + bu_ref[...]
        gate = jnp.minimum(gate, SWIGLU_LIMIT)
        up = jnp.clip(up, -SWIGLU_LIMIT, SWIGLU_LIMIT)
        act_ref[...] = ((up + 1.0) * gate * _sigmoid(SWIGLU_ALPHA * gate)).astype(act_ref.dtype)

    @pl.when(sc_ref[S_VALID, s] == 0)
    def _():
        act_ref[...] = jnp.zeros_like(act_ref)


def _moe_down_kernel(sc_ref, a_ref, bd_ref, wd_hbm, y_ref, stage, wd_b, sems, *, tf):
    s = pl.program_id(0)

    def copies(e, j, slot):
        del j
        return [pltpu.make_async_copy(wd_hbm.at[e], stage.at[slot], sems.at[slot])]

    _weight_group_prefetch(sc_ref, copies, lambda slot: _cast_tile(stage.at[slot], wd_b))

    @pl.when(sc_ref[S_VALID, s] == 1)
    def _():
        cols = pl.ds(pl.multiple_of(sc_ref[S_J, s] * tf, tf), tf)
        y_ref[...] = _dot(a_ref[...], wd_b[:, cols]) + bd_ref[...]

    @pl.when(sc_ref[S_VALID, s] == 0)
    def _():
        y_ref[...] = jnp.zeros_like(y_ref)


def _moe_experts(xs, sched_up, sched_down, wg, wu, wd, bg, bu, bd, tm, tf):
    rows = xs.shape[0]
    n_exp, d, f = wg.shape
    assert xs.shape[1] * 2 == d and d // tf == f // tf
    steps = sched_up.shape[1]
    blk = lambda shape, at: pl.BlockSpec(shape, lambda s, sc: at(sc, s))
    hbm = pl.BlockSpec(memory_space=pl.ANY)
    act = pl.pallas_call(
        functools.partial(_moe_up_kernel, tf=tf),
        grid_spec=pltpu.PrefetchScalarGridSpec(
            num_scalar_prefetch=1,
            grid=(steps,),
            in_specs=[blk((tm, d // 2), lambda sc, s: (sc[S_R, s], 0)),
                      blk((None, 1, tf), lambda sc, s: (sc[S_E, s], 0, sc[S_J, s])),
                      blk((None, 1, tf), lambda sc, s: (sc[S_E, s], 0, sc[S_J, s])),
                      hbm, hbm],
            out_specs=blk((tm, tf), lambda sc, s: (sc[S_RO, s], sc[S_JO, s])),
            scratch_shapes=[pltpu.VMEM((2, 2, d, tf), F32), pltpu.VMEM((d, tf), BF16), pltpu.VMEM((d, tf), BF16),
                            pltpu.SemaphoreType.DMA((2, 2))]),
        out_shape=jax.ShapeDtypeStruct((rows, f), BF16),
        compiler_params=_cparams("arbitrary"),
        name="moe_up",
    )(sched_up, xs, bg.reshape(n_exp, 1, f), bu.reshape(n_exp, 1, f), wg, wu)
    return pl.pallas_call(
        functools.partial(_moe_down_kernel, tf=tf),
        grid_spec=pltpu.PrefetchScalarGridSpec(
            num_scalar_prefetch=1,
            grid=(steps,),
            in_specs=[blk((tm, f), lambda sc, s: (sc[S_R, s], 0)),
                      blk((None, 1, tf), lambda sc, s: (sc[S_E, s], 0, sc[S_J, s])),
                      hbm],
            out_specs=blk((tm, tf), lambda sc, s: (sc[S_RO, s], sc[S_JO, s])),
            scratch_shapes=[pltpu.VMEM((2, f, d), F32), pltpu.VMEM((f, d), BF16), pltpu.SemaphoreType.DMA((2,))]),
        out_shape=jax.ShapeDtypeStruct((rows, d), F32),
        compiler_params=_cparams("arbitrary"),
        name="moe_down",
    )(sched_down, act, bd.reshape(n_exp, 1, d), wd)


def _route(top_idx, rank, counts, tm, n_tiles):
    n_tok, k = top_idx.shape
    n_exp = counts.shape[0]
    m = n_tok * k
    padded = (counts + tm - 1) // tm * tm
    pend = jnp.cumsum(padded)
    onehot = top_idx.reshape(m, 1) == jnp.arange(n_exp, dtype=jnp.int32)[None, :]
    dest = jnp.sum(jnp.where(onehot, (pend - padded)[None, :], 0), axis=1) + rank.reshape(m)
    n_blocks = -(-m // tm) + n_exp

    nb = padded // tm
    blk0 = (pend - padded) // tm
    cum = jnp.cumsum(nb * n_tiles)
    total = cum[-1]
    s = jnp.arange(n_tiles * n_blocks, dtype=jnp.int32)
    valid = s < total
    sc = jnp.minimum(s, total - 1)
    e = jnp.minimum(jnp.sum((cum[None, :] <= sc[:, None]).astype(jnp.int32), axis=1), n_exp - 1)
    of_e = e[:, None] == jnp.arange(n_exp, dtype=jnp.int32)[None, :]
    pick = lambda v: jnp.sum(jnp.where(of_e, v[None, :], 0), axis=1)
    nb_e = pick(nb)
    local = sc - (pick(cum) - nb_e * n_tiles)
    nbe = jnp.maximum(nb_e, 1)
    blk0_e = pick(blk0)
    extra = s - total
    ids = jnp.arange(n_exp, dtype=jnp.int32)
    later = jnp.where((ids[None, :] > ids[:, None]) & (nb[None, :] > 0), ids[None, :], n_exp)
    next_e = jnp.min(later, axis=1)
    next_e = pick(jnp.where(next_e == n_exp, -1, next_e))
    erank = pick(jnp.cumsum((nb > 0).astype(jnp.int32)) - 1)

    def rows_of(j, r, first, ne, nj, group):
        r_out = jnp.where(valid, r, total // n_tiles + extra // n_tiles)
        j_out = jnp.where(valid, j, extra % n_tiles)
        fields = (e, j, r, r_out, j_out, valid & first, valid, ne, nj, group % 2)
        return jnp.stack([a.astype(jnp.int32) for a in fields])

    j_up = local // nbe
    last_tile = j_up == n_tiles - 1
    sched_up = rows_of(j_up, blk0_e + local % nbe, local % nbe == 0,
                       jnp.where(last_tile, next_e, e), jnp.where(last_tile, 0, j_up + 1),
                       erank * n_tiles + j_up)
    sched_down = rows_of(local % n_tiles, blk0_e + local // n_tiles, local == 0, next_e, jnp.zeros_like(e), erank)
    zrow = jnp.concatenate([jnp.where(padded > 0, pend - tm, -1), pend[-1:] // tm]).astype(jnp.int32)
    return dest.astype(jnp.int32), zrow, n_blocks * tm, sched_up, sched_down


def _combine_kernel(dest_ref, x1_ref, gate_ref, gt_ref, nw_ref, ys_ref, o_ref, ybuf, sems, *, tt, k):
    i = pl.program_id(0)
    n = pl.num_programs(0)

    def start_gather(tile, slot):
        base = tile * (tt * k)

        def body(tb, carry):
            for u in range(8):
                for j in range(k):
                    row = dest_ref[base + (tb * 8 + u) * k + j]
                    pltpu.make_async_copy(ys_ref.at[row >> 3, pl.ds(row & 7, 1), :],
                                          ybuf.at[slot, j * (tt // 8) + tb, pl.ds(u, 1), :], sems.at[slot]).start()
            return carry

        lax.fori_loop(0, tt // 8, body, 0)

    @pl.when(i == 0)
    def _():
        start_gather(0, 0)

    for nxt in (0, 1):
        @pl.when((i + 1 < n) & ((i + 1) % 2 == nxt))
        def _():
            start_gather(i + 1, nxt)

    slot = i % 2
    pltpu.make_async_copy(ybuf.at[slot], ybuf.at[slot], sems.at[slot]).wait()
    g = gate_ref[...]
    d = o_ref.shape[-1]
    picked = lambda j: ybuf[slot, pl.ds(j * (tt // 8), tt // 8), :, :].reshape(tt, d)
    moe = picked(0) * g[:, 0:1]
    for j in range(1, k):
        moe = moe + picked(j) * g[:, j:j + 1]
    x = x1_ref[...] + gt_ref[...] * moe
    ms = jnp.mean(x * x, axis=-1, keepdims=True)
    o_ref[...] = x * lax.rsqrt(ms + EPS) * nw_ref[...]


def _combine(x1, ys, dest, gates, gt2, nw, k):
    b, t, d = x1.shape
    tt = min(t, 256)
    tpb = t // tt
    row = lambda width: pl.BlockSpec((None, tt, width), lambda i, dest_: (i // tpb, i % tpb, 0))
    return pl.pallas_call(
        functools.partial(_combine_kernel, tt=tt, k=k),
        grid_spec=pltpu.PrefetchScalarGridSpec(
            num_scalar_prefetch=1,
            grid=(b * tpb,),
            in_specs=[row(d), row(LANES),
                      pl.BlockSpec((None, 1, d), lambda i, dest_: (i // tpb, 0, 0)),
                      pl.BlockSpec((1, d), lambda i, dest_: (0, 0)),
                      pl.BlockSpec(memory_space=pl.ANY)],
            out_specs=row(d),
            scratch_shapes=[pltpu.VMEM((2, k * tt // 8, 8, d), F32), pltpu.SemaphoreType.DMA((2,))]),
        out_shape=jax.ShapeDtypeStruct((b, t, d), F32),
        compiler_params=_cparams("arbitrary"),
        name="combine",
    )(dest, x1, gates, gt2, nw.reshape(1, d), ys.reshape(ys.shape[0] // 8, 8, d))


def kernel(x, c, ctx, c_ctx, w_ada, b_ada, norm_mix_w, w_in, hg_lb_f, hg_lb_b, hg_norm_w, gd_conv_w,
           gd_a_log_f, gd_a_log_b, gd_dt_bias_f, gd_dt_bias_b, gd_norm_w, w_out, norm_ffn_w, w_router,
           b_router, w_gate, b_gate, w_up, b_up, w_down, b_down, norm_out_w):
    bsz, t, d = x.shape
    t_ctx = ctx.shape[1]
    n_exp = w_router.shape[-1]
    l = 0
    hg_w = N_HEADS * HEAD_DIM
    n_main = 9 * hg_w

    lb_f = jnp.cumsum(jax.nn.softmax(hg_lb_f.astype(F32), axis=0), axis=0)[l].reshape(1, hg_w)
    lb_b = jnp.cumsum(jax.nn.softmax(hg_lb_b.astype(F32), axis=0), axis=0)[l].reshape(1, hg_w)

    rows = -(-(bsz + 1) // 8) * 8
    cc = jnp.zeros((rows, d), F32).at[:bsz].set(c).at[bsz].set(c_ctx)
    mod = _ada(cc, w_ada[l], b_ada[l])
    sh1, sc1, gt1, sh2, sc2, gt2 = (mod[:bsz, i * d:(i + 1) * d].reshape(bsz, 1, d) for i in range(6))
    csh1 = jnp.broadcast_to(mod[bsz, 0:d].reshape(1, 1, d), (bsz, 1, d))
    csc1 = jnp.broadcast_to(mod[bsz, d:2 * d].reshape(1, 1, d), (bsz, 1, d))

    w_in_b = w_in[l].astype(BF16)
    w_gates = jnp.pad(w_in_b[:, n_main:], ((0, 0), (0, LANES - (w_in_b.shape[1] - n_main))))
    px, gx = _inproj(x, norm_mix_w[l], sc1, sh1, w_in_b, n_main, w_gates)
    pc, gc = _inproj(ctx, norm_mix_w[l], csc1, csh1, w_in_b, n_main, w_gates)

    hg_nw = hg_norm_w[l].reshape(1, HEAD_DIM)
    hs_f, hs_b = _hgrn_states(pc, lb_f, lb_b)
    mix_h = _hgrn_out(px, lb_f, lb_b, hg_nw, hs_f, hs_b)

    hp = jnp.stack([gd_a_log_f[l], gd_dt_bias_f[l], gd_a_log_b[l], gd_dt_bias_b[l]], axis=1)
    hp = jnp.broadcast_to(jnp.pad(hp, ((0, 0), (0, 4)))[:, :, None], (N_HEADS, 8, HEAD_DIM)).astype(F32)
    gd_nw = gd_norm_w[l].reshape(1, HEAD_DIM)
    gs_f, gs_b = _gdn_call(pc, gc, gd_conv_w[l], hp, gd_nw, None, None, t_ctx, False, 5)
    mix_g = _gdn_call(px, gx, gd_conv_w[l], hp, gd_nw, gs_f, gs_b, GRID_W, True, 5)

    w_out_b = w_out[l].astype(BF16)
    wr = jnp.pad(w_router[l], ((0, 0), (0, LANES - n_exp)))
    wr_hi = wr.astype(BF16)
    wr_lo = (wr - wr_hi.astype(F32)).astype(BF16)
    br = jnp.full((1, LANES), NEG_BIG, F32).at[0, :n_exp].set(b_router[l])
    x1, h2, idx_pad, gate_pad, cnt = _outproj(mix_h, mix_g, x, gt1, sc2, sh2, norm_ffn_w[l],
                                              w_out_b.reshape(2, hg_w, d), wr_hi, wr_lo, br)

    n_tok = bsz * t
    idx_pad = idx_pad.reshape(n_tok, LANES)
    dest, zrow, n_rows, sched_up, sched_down = _route(idx_pad[:, :TOP_K], idx_pad[:, TOP_K:2 * TOP_K],
                                                      cnt[0, :n_exp].astype(jnp.int32), MOE_TM,
                                                      w_gate.shape[-1] // MOE_TF)
    xs = _dispatch(h2.reshape(n_tok, d // 2), dest, zrow, n_rows, TOP_K, MOE_TM)
    ys = _moe_experts(xs, sched_up, sched_down, w_gate[l], w_up[l], w_down[l], b_gate[l], b_up[l], b_down[l],
                      MOE_TM, MOE_TF)

    return _combine(x1, ys, dest, gate_pad, gt2, norm_out_w, TOP_K)
```

```python
import functools

import jax
import jax.numpy as jnp
from jax import lax
from jax.experimental import pallas as pl
from jax.experimental.pallas import tpu as pltpu

F32 = jnp.float32
BF16 = jnp.bfloat16

EPS = 1e-6
CHUNK = 64
HEAD_DIM = 128
N_HEADS = 8
GRID_W = 64
TOP_K = 4
SWIGLU_LIMIT = 7.0
SWIGLU_ALPHA = 1.702
LANES = 128
NEG_BIG = -1e30

HGRN_GROUP = 8
GDN_GROUP = 4
GDN_INTRA_GROUP = 16
MOE_TM = 512
MOE_TF = 1024
VMEM_LIMIT = 56 * 1024 * 1024


def _cparams(*sem):
    return pltpu.CompilerParams(dimension_semantics=sem, vmem_limit_bytes=VMEM_LIMIT)


def _dot(a, b):
    return jnp.dot(a, b, preferred_element_type=F32)


def _dot_nt(a, b):
    return lax.dot_general(a, b, (((1,), (1,)), ((), ())), preferred_element_type=F32)


def _split2(x):
    hi = x.astype(BF16)
    return hi, (x - hi.astype(F32)).astype(BF16)


def _split3(x):
    hi = x.astype(BF16)
    rest = x - hi.astype(F32)
    mid = rest.astype(BF16)
    return hi, mid, (rest - mid.astype(F32)).astype(BF16)


def _dot_split(a, b):
    ah, al = a
    bh, bl = b
    return _dot(ah, bh) + (_dot(ah, bl) + _dot(al, bh))


def _sigmoid(x):
    return 1.0 / (1.0 + jnp.exp(-x))


def _ada_kernel(c_ref, w_ref, b_ref, o_ref):
    c = c_ref[...]
    s = (c * _sigmoid(c)).astype(BF16)
    o_ref[...] = _dot(s, w_ref[...].astype(BF16)) + b_ref[...]


def _ada(cc, w_ada, b_ada):
    rows, d = cc.shape
    n = w_ada.shape[1]
    tn = 1024
    return pl.pallas_call(
        _ada_kernel,
        grid=(n // tn,),
        in_specs=[pl.BlockSpec((rows, d), lambda j: (0, 0)),
                  pl.BlockSpec((d, tn), lambda j: (0, j)),
                  pl.BlockSpec((1, tn), lambda j: (0, j))],
        out_specs=pl.BlockSpec((rows, tn), lambda j: (0, j)),
        out_shape=jax.ShapeDtypeStruct((rows, n), F32),
        compiler_params=_cparams("arbitrary"),
        name="ada",
    )(cc, w_ada, b_ada.reshape(1, n))


def _modulated_norm(x, nw, sc, sh):
    ms = jnp.mean(x * x, axis=-1, keepdims=True)
    return (x * lax.rsqrt(ms + EPS) * nw) * (1.0 + sc) + sh


def _inproj_kernel(x_ref, nw_ref, sc_ref, sh_ref, w_ref, wg_ref, o_ref, og_ref, h_scr):
    @pl.when(pl.program_id(1) == 0)
    def _():
        h = _modulated_norm(x_ref[...], nw_ref[...], sc_ref[...], sh_ref[...]).astype(BF16)
        h_scr[...] = h
        og_ref[...] = _dot(h, wg_ref[...])

    o_ref[...] = _dot(h_scr[...], w_ref[...])


def _inproj(x, nw, sc, sh, w_all, n, w_gates):
    b, t, d = x.shape
    w_main = w_all
    tm = min(t, 1024)
    tn = 1024
    tpb = t // tm
    assert n % tn == 0
    return pl.pallas_call(
        _inproj_kernel,
        grid=(b * tpb, n // tn),
        in_specs=[pl.BlockSpec((None, tm, d), lambda i, j: (i // tpb, i % tpb, 0)),
                  pl.BlockSpec((1, d), lambda i, j: (0, 0)),
                  pl.BlockSpec((None, 1, d), lambda i, j: (i // tpb, 0, 0)),
                  pl.BlockSpec((None, 1, d), lambda i, j: (i // tpb, 0, 0)),
                  pl.BlockSpec((d, tn), lambda i, j: (0, j)),
                  pl.BlockSpec((d, LANES), lambda i, j: (0, 0))],
        out_specs=[pl.BlockSpec((None, tm, tn), lambda i, j: (i // tpb, i % tpb, j)),
                   pl.BlockSpec((None, tm, LANES), lambda i, j: (i // tpb, i % tpb, 0))],
        out_shape=[jax.ShapeDtypeStruct((b, t, n), F32),
                   jax.ShapeDtypeStruct((b, t, LANES), F32)],
        scratch_shapes=[pltpu.VMEM((tm, d), BF16)],
        compiler_params=_cparams("parallel", "arbitrary"),
        name="inproj",
    )(x, nw.reshape(1, d), sc, sh, w_main, w_gates)


def _chunk_masks(rev, wide=False):
    shape = (CHUNK, 2 * CHUNK if wide else CHUNK)
    r = lax.broadcasted_iota(jnp.int32, shape, 0)
    c = lax.broadcasted_iota(jnp.int32, shape, 1) & (CHUNK - 1)
    incl = (c >= r) if rev else (c <= r)
    strict = (c > r) if rev else (c < r)
    return incl, strict


def _gated_head_norm(o, z, nw):
    ms = jnp.mean(o * o, axis=-1, keepdims=True)
    return (o * lax.rsqrt(ms + EPS) * nw) * (z * _sigmoid(z))


def _group_slices(g, group, n_chunks):
    idxs, revs = [], []
    for i in range(group):
        n = g * group + i
        idxs += [n, n_chunks - 1 - n]
        revs += [False, True]
    return [pl.ds(pl.multiple_of(ix * CHUNK, CHUNK), CHUNK) for ix in idxs], idxs, revs


def _scan_group(n_chunks, want, two_phase):
    g = want
    while g > 1 and (n_chunks % g or (two_phase and (n_chunks // g) % 2)):
        g //= 2
    assert n_chunks % g == 0 and not (two_phase and (n_chunks // g) % 2)
    return g


def _gla_group(zfs, vs, qs, lbs, revs):
    n = len(zfs)
    incl = [_chunk_masks(r)[0] for r in revs]
    logf, k = [], []
    for i in range(n):
        one_m = 1.0 - lbs[i]
        logf.append(jnp.log(lbs[i] + one_m * _sigmoid(zfs[i])))
        k.append(one_m * _sigmoid(-zfs[i]))
    cum = []
    for i in range(n):
        tri = incl[i].astype(BF16)
        hi, mid, lo = _split3(logf[i])
        parts = _dot(tri, jnp.concatenate([hi, mid, lo], axis=1))
        cum.append(parts[:, :HEAD_DIM] + (parts[:, HEAD_DIM:2 * HEAD_DIM] + parts[:, 2 * HEAD_DIM:]))
    last = [cum[i][0:1] if revs[i] else cum[i][CHUNK - 1:CHUNK] for i in range(n)]
    u_t = [_dot(vs[i].T.astype(BF16), (k[i] * jnp.exp(last[i] - cum[i])).astype(BF16)) for i in range(n)]
    dec = [jnp.exp(x) for x in last]
    if qs is None:
        return u_t, dec, None, None
    scores = []
    for i in range(n):
        mid_row = CHUNK // 2 - 1 if revs[i] else CHUNK // 2
        ref = cum[i][mid_row:mid_row + 1]
        sc = _dot_nt((qs[i] * jnp.exp(cum[i] - ref)).astype(BF16), (k[i] * jnp.exp(ref - cum[i])).astype(BF16))
        scores.append(jnp.where(incl[i], sc, 0.0).astype(BF16))
    o_intra = [_dot(scores[i], vs[i].astype(BF16)) for i in range(n)]
    qe = [(qs[i] * jnp.exp(cum[i])).astype(BF16) for i in range(n)]
    return u_t, dec, o_intra, qe


def _hgrn_out_kernel(q_ref, ff_ref, fb_ref, v_ref, g_ref, lbf_ref, lbb_ref, nw_ref, s0f_ref, s0b_ref,
                     o_ref, o_scr, sf_scr, sb_scr, *, n_chunks, group):
    sf_scr[...] = s0f_ref[...]
    sb_scr[...] = s0b_ref[...]
    lbf, lbb, nw = lbf_ref[...], lbb_ref[...], nw_ref[...]

    def body(g, carry, final):
        sls, _, revs = _group_slices(g, group, n_chunks)
        zfs = [(fb_ref if r else ff_ref)[sl, :] for sl, r in zip(sls, revs)]
        vs = [v_ref[sl, :] for sl in sls]
        qs = [q_ref[sl, :] for sl in sls]
        u_t, dec, o_intra, qe = _gla_group(zfs, vs, qs, [lbb if r else lbf for r in revs], revs)
        for i, (sl, rev) in enumerate(zip(sls, revs)):
            s_scr = sb_scr if rev else sf_scr
            st = s_scr[...]
            o = o_intra[i] + _dot_nt(qe[i], st.astype(BF16))
            s_scr[...] = st * dec[i] + u_t[i]
            if final:
                o_ref[sl, :] = _gated_head_norm(o_scr[sl, :] + o, g_ref[sl, :], nw).astype(o_ref.dtype)
            else:
                o_scr[sl, :] = o
        return carry

    trips = n_chunks // group
    lax.fori_loop(0, trips // 2, functools.partial(body, final=False), 0)
    lax.fori_loop(trips // 2, trips, functools.partial(body, final=True), 0)


def _hgrn_state_kernel(ff_ref, fb_ref, v_ref, lbf_ref, lbb_ref, sf_ref, sb_ref, *, n_chunks, group):
    sf_ref[...] = jnp.zeros_like(sf_ref)
    sb_ref[...] = jnp.zeros_like(sb_ref)
    lbf, lbb = lbf_ref[...], lbb_ref[...]

    def body(g, carry):
        sls, _, revs = _group_slices(g, group, n_chunks)
        zfs = [(fb_ref if r else ff_ref)[sl, :] for sl, r in zip(sls, revs)]
        vs = [v_ref[sl, :] for sl in sls]
        u_t, dec, _, _ = _gla_group(zfs, vs, None, [lbb if r else lbf for r in revs], revs)
        for i, rev in enumerate(revs):
            s_ref = sb_ref if rev else sf_ref
            s_ref[...] = s_ref[...] * dec[i] + u_t[i]
        return carry

    lax.fori_loop(0, n_chunks // group, body, 0)


def _col_spec(t, section):
    return pl.BlockSpec((None, t, HEAD_DIM), lambda b, h: (b, 0, section * N_HEADS + h))


def _head_row_spec(offset=0):
    return pl.BlockSpec((1, HEAD_DIM), lambda b, h: (0, offset + h))


_STATE_SPEC = pl.BlockSpec((None, None, HEAD_DIM, HEAD_DIM), lambda b, h: (b, h, 0, 0))


def _hgrn_states(pc, lbf, lbb):
    b, t, _ = pc.shape
    n_chunks = t // CHUNK
    shp = jax.ShapeDtypeStruct((b, N_HEADS, HEAD_DIM, HEAD_DIM), F32)
    return pl.pallas_call(
        functools.partial(_hgrn_state_kernel, n_chunks=n_chunks, group=_scan_group(n_chunks, HGRN_GROUP, False)),
        grid=(b, N_HEADS),
        in_specs=[_col_spec(t, 1), _col_spec(t, 2), _col_spec(t, 3), _head_row_spec(), _head_row_spec()],
        out_specs=[_STATE_SPEC, _STATE_SPEC],
        out_shape=[shp, shp],
        compiler_params=_cparams("parallel", "parallel"),
        name="hgrn_ctx",
    )(pc, pc, pc, lbf, lbb)


def _hgrn_out(px, lbf, lbb, nw, s0f, s0b):
    b, t, _ = px.shape
    n_chunks = t // CHUNK
    return pl.pallas_call(
        functools.partial(_hgrn_out_kernel, n_chunks=n_chunks, group=_scan_group(n_chunks, HGRN_GROUP, True)),
        grid=(b, N_HEADS),
        in_specs=[_col_spec(t, 0), _col_spec(t, 1), _col_spec(t, 2), _col_spec(t, 3), _col_spec(t, 4),
                  _head_row_spec(), _head_row_spec(),
                  pl.BlockSpec((1, HEAD_DIM), lambda b_, h: (0, 0)),
                  _STATE_SPEC, _STATE_SPEC],
        out_specs=pl.BlockSpec((None, t, HEAD_DIM), lambda b_, h: (b_, 0, h)),
        out_shape=jax.ShapeDtypeStruct((b, t, N_HEADS * HEAD_DIM), BF16),
        scratch_shapes=[pltpu.VMEM((t, HEAD_DIM), F32),
                        pltpu.VMEM((HEAD_DIM, HEAD_DIM), F32),
                        pltpu.VMEM((HEAD_DIM, HEAD_DIM), F32)],
        compiler_params=_cparams("parallel", "parallel"),
        name="hgrn_x",
    )(px, px, px, px, px, lbf, lbb, nw, s0f, s0b)


def _conv_silu(a, w, period):
    rows = a.shape[0]
    pos = lax.broadcasted_iota(jnp.int32, a.shape, 0) % period
    prev = jnp.where(pos == 0, 0.0, pltpu.roll(a, 1, axis=0))
    nxt = jnp.where(pos == period - 1, 0.0, pltpu.roll(a, rows - 1, axis=0))
    y = prev * w[0:1] + a * w[1:2] + nxt * w[2:3]
    return y * _sigmoid(y)


def _l2norm(x):
    return x * lax.rsqrt(jnp.sum(x * x, axis=-1, keepdims=True) + EPS)


def _softplus(x):
    return jnp.maximum(x, 0.0) + jnp.log1p(jnp.exp(-jnp.abs(x)))


def _unit_tri_inverses(tris):
    row = lax.broadcasted_iota(jnp.int32, (CHUNK, 2 * CHUNK), 0)
    lane = lax.broadcasted_iota(jnp.int32, (CHUNK, 2 * CHUNK), 1)
    left = lane < CHUNK
    eye = jnp.where((lane & (CHUNK - 1)) == row, 1.0, 0.0)
    zeros = jnp.zeros((CHUNK, 2 * CHUNK), BF16)

    def lhs(hl):
        return jnp.where(left, hl[0], hl[1])

    def rhs(hl):
        top = jnp.concatenate([hl[0], hl[1]], axis=1)
        return jnp.concatenate([top, jnp.concatenate([hl[0], zeros], axis=1)], axis=0)

    def fold(o):
        return o[:, :2 * CHUNK] + o[:, 2 * CHUNK:]

    ps = [_split2(-a) for a in tris]
    invs = [eye - a for a in tris]
    ps = [_split2(fold(_dot(lhs(p), rhs(p)))) for p in ps]
    for _ in range(CHUNK.bit_length() - 3):
        outs = [fold(_dot(jnp.concatenate([lhs(p), lhs(_split2(inv))], axis=0), rhs(p)))
                for p, inv in zip(ps, invs)]
        invs = [inv + o[CHUNK:] for inv, o in zip(invs, outs)]
        ps = [_split2(o[:CHUNK]) for o in outs]
    invs = [inv + fold(_dot(lhs(_split2(inv)), rhs(p))) for inv, p in zip(invs, ps)]
    return [inv[:, :CHUNK] for inv in invs]


def _gdn_group(qs, ks, vs, a_cs, b_cs, a_rs, alogs, dtbs, revs):
    n = len(ks)
    left = lax.broadcasted_iota(jnp.int32, (CHUNK, 2 * CHUNK), 1) < CHUNK
    dmask, ecum, beta, eend, dec, strict = [], [], [], [], [], []
    for i in range(n):
        incl, st = _chunk_masks(revs[i], wide=True)
        incl_t = _chunk_masks(not revs[i], wide=True)[0]
        scale = -jnp.exp(alogs[i])
        g_c = scale * _softplus(a_cs[i] + dtbs[i])
        g_r = scale * _softplus(a_rs[i] + dtbs[i])
        cum_c = jnp.sum(jnp.where(incl & left, g_r, 0.0), axis=1, keepdims=True)
        cum_r = jnp.sum(jnp.where(incl_t, g_c, 0.0), axis=0, keepdims=True)
        dmask.append(jnp.exp(jnp.where(incl, cum_c - cum_r, -jnp.inf)))
        ecum.append(jnp.exp(cum_c))
        beta.append(_sigmoid(b_cs[i]))
        last = cum_c[0:1] if revs[i] else cum_c[CHUNK - 1:CHUNK]
        eend.append(jnp.exp(last - cum_c))
        dec.append(jnp.exp(last))
        strict.append(st)
    kbf = [k.astype(BF16) for k in ks]
    kb = [ks[i] * beta[i] for i in range(n)]
    kk = [_dot_nt(kb[i].astype(BF16), jnp.concatenate([kbf[i], kbf[i]], axis=0)) for i in range(n)]
    tris = [jnp.where(strict[i], kk[i] * dmask[i], 0.0) for i in range(n)]
    tinv = [t.astype(BF16) for t in _unit_tri_inverses(tris)]
    dmask = [m[:, :CHUNK] for m in dmask]
    u = [_dot(tinv[i], (vs[i] * beta[i]).astype(BF16)).astype(BF16) for i in range(n)]
    w = [_dot(tinv[i], (kb[i] * ecum[i]).astype(BF16)).astype(BF16) for i in range(n)]
    ke_t = [(ks[i] * eend[i]).T.astype(BF16) for i in range(n)]
    mp = [_dot(ke_t[i], w[i]) for i in range(n)]
    cc = [_dot(ke_t[i], u[i]) for i in range(n)]
    if qs is None:
        return mp, cc, dec, None, None
    attn = [(_dot_nt(qs[i].astype(BF16), kbf[i]) * dmask[i]).astype(BF16) for i in range(n)]
    qp = [qs[i] * ecum[i] - _dot(attn[i], w[i]) for i in range(n)]
    oi = [_dot(attn[i], u[i]) for i in range(n)]
    return mp, cc, dec, qp, oi


def _gdn_kernel(*refs, n_chunks, period, prep_rows, group, igroup, with_out):
    if with_out:
        (q_ref, k_ref, v_ref, z_ref, gc_ref, gr_ref, wq_ref, wk_ref, wv_ref, hp_ref, nw_ref,
         s0f_ref, s0b_ref, o_ref, qn, kn, vn, mp_s, cc_s, dc_s, qp_s, o_scr, sf, sb) = refs
    else:
        (k_ref, v_ref, gc_ref, gr_ref, wk_ref, wv_ref, hp_ref,
         sf, sb, kn, vn, mp_s, cc_s, dc_s) = refs
        q_ref = None

    t = n_chunks * CHUNK

    def prep(i, carry):
        sl = pl.ds(pl.multiple_of(i * prep_rows, prep_rows), prep_rows)
        kn[sl, :] = _l2norm(_conv_silu(k_ref[sl, :], wk_ref[...], period))
        vn[sl, :] = _conv_silu(v_ref[sl, :], wv_ref[...], period)
        if with_out:
            qn[sl, :] = _l2norm(_conv_silu(q_ref[sl, :], wq_ref[...], period)) * (HEAD_DIM ** -0.5)
        return carry

    lax.fori_loop(0, t // prep_rows, prep, 0)

    hp = hp_ref[...]

    def intra(g, carry):
        qs, ks, vs, a_cs, b_cs, a_rs, alogs, dtbs, revs, where = [], [], [], [], [], [], [], [], [], []
        for i in range(igroup):
            n = g * igroup + i
            sl = pl.ds(pl.multiple_of(n * CHUNK, CHUNK), CHUNK)
            gc = gc_ref[sl, :]
            gr = gr_ref[n]
            for d, rev in enumerate((False, True)):
                if with_out:
                    qs.append(qn[sl, :])
                ks.append(kn[sl, :])
                vs.append(vn[sl, :])
                a_cs.append(gc[:, d:d + 1])
                b_cs.append(gc[:, 2 + d:3 + d])
                a_rs.append(gr[d:d + 1, :])
                alogs.append(hp[2 * d:2 * d + 1, 0:1])
                dtbs.append(hp[2 * d + 1:2 * d + 2, 0:1])
                revs.append(rev)
                where.append((d, n, sl))
        mp, cc, dec, qp, oi = _gdn_group(qs if with_out else None, ks, vs, a_cs, b_cs, a_rs, alogs, dtbs, revs)
        for i, (d, n, sl) in enumerate(where):
            mp_s[d, n] = mp[i].astype(BF16)
            cc_s[d, n] = cc[i]
            dc_s[d, n] = jnp.broadcast_to(dec[i], (8, HEAD_DIM))
            if with_out:
                qp_s[d, sl, :] = qp[i].astype(BF16)
                if d == 1:
                    o_scr[sl, :] = oi[i - 1] + oi[i]
        return carry

    lax.fori_loop(0, n_chunks // igroup, intra, 0)

    if with_out:
        sf[...] = s0f_ref[...]
        sb[...] = s0b_ref[...]
        nw = nw_ref[...]
    else:
        sf[...] = jnp.zeros_like(sf)
        sb[...] = jnp.zeros_like(sb)

    def scan(g, carry, final):
        sls, idxs, revs = _group_slices(g, group, n_chunks)
        for sl, idx, rev in zip(sls, idxs, revs):
            d = int(rev)
            s_ref = sb if rev else sf
            s = s_ref[...]
            sbf = s.astype(BF16)
            s_ref[...] = s * dc_s[d, idx][0:1, :] - _dot(mp_s[d, idx], sbf) + cc_s[d, idx]
            if with_out:
                o = o_scr[sl, :] + _dot(qp_s[d, sl, :], sbf)
                if final:
                    o_ref[sl, :] = _gated_head_norm(o, z_ref[sl, :], nw).astype(o_ref.dtype)
                else:
                    o_scr[sl, :] = o
        return carry

    trips = n_chunks // group
    if with_out:
        lax.fori_loop(0, trips // 2, functools.partial(scan, final=False), 0)
        lax.fori_loop(trips // 2, trips, functools.partial(scan, final=True), 0)
    else:
        lax.fori_loop(0, trips, functools.partial(scan, final=False), 0)


def _gdn_call(p, gates, conv_w, hp, nw, s0f, s0b, period, with_out, sec0):
    b, t, _ = p.shape
    n_chunks = t // CHUNK
    prep_rows = max(period, min(t, 256))
    assert prep_rows % period == 0 and t % prep_rows == 0
    group = _scan_group(n_chunks, GDN_GROUP, with_out)
    g4 = gates[:, :, :4 * N_HEADS].reshape(b, t, 4, N_HEADS)
    g_col = g4.transpose(0, 3, 1, 2)
    g_row = jnp.pad(g4.transpose(0, 3, 2, 1), ((0, 0), (0, 0), (0, 4), (0, 0)))
    g_row = g_row.reshape(b, N_HEADS, 8, n_chunks, CHUNK).transpose(0, 1, 3, 2, 4)
    g_row = jnp.concatenate([g_row, g_row], axis=-1)

    gc_spec = pl.BlockSpec((None, None, t, 4), lambda b_, h: (b_, h, 0, 0))
    gr_spec = pl.BlockSpec((None, None, n_chunks, 8, 2 * CHUNK), lambda b_, h: (b_, h, 0, 0, 0))
    conv_spec = lambda sec: pl.BlockSpec((3, HEAD_DIM), lambda b_, h: (0, sec * N_HEADS + h))
    hp_spec = pl.BlockSpec((None, 8, HEAD_DIM), lambda b_, h: (h, 0, 0))
    seq = lambda: pltpu.VMEM((t, HEAD_DIM), F32)
    mats = lambda dt: pltpu.VMEM((2, n_chunks, HEAD_DIM, HEAD_DIM), dt)
    dc = pltpu.VMEM((2, n_chunks, 8, HEAD_DIM), F32)
    state = lambda: pltpu.VMEM((HEAD_DIM, HEAD_DIM), F32)
    kern = functools.partial(_gdn_kernel, n_chunks=n_chunks, period=period, prep_rows=prep_rows,
                             group=group, igroup=_scan_group(n_chunks, GDN_INTRA_GROUP, False),
                             with_out=with_out)
    if with_out:
        return pl.pallas_call(
            kern,
            grid=(b, N_HEADS),
            in_specs=[_col_spec(t, sec0), _col_spec(t, sec0 + 1), _col_spec(t, sec0 + 2), _col_spec(t, sec0 + 3),
                      gc_spec, gr_spec, conv_spec(0), conv_spec(1), conv_spec(2), hp_spec,
                      pl.BlockSpec((1, HEAD_DIM), lambda b_, h: (0, 0)), _STATE_SPEC, _STATE_SPEC],
            out_specs=pl.BlockSpec((None, t, HEAD_DIM), lambda b_, h: (b_, 0, h)),
            out_shape=jax.ShapeDtypeStruct((b, t, N_HEADS * HEAD_DIM), BF16),
            scratch_shapes=[seq(), seq(), seq(), mats(BF16), mats(F32), dc,
                            pltpu.VMEM((2, t, HEAD_DIM), BF16), seq(), state(), state()],
            compiler_params=_cparams("parallel", "parallel"),
            name="gdn_x",
        )(p, p, p, p, g_col, g_row, conv_w, conv_w, conv_w, hp, nw, s0f, s0b)
    shp = jax.ShapeDtypeStruct((b, N_HEADS, HEAD_DIM, HEAD_DIM), F32)
    return pl.pallas_call(
        kern,
        grid=(b, N_HEADS),
        in_specs=[_col_spec(t, sec0 + 1), _col_spec(t, sec0 + 2), gc_spec, gr_spec,
                  conv_spec(1), conv_spec(2), hp_spec],
        out_specs=[_STATE_SPEC, _STATE_SPEC],
        out_shape=[shp, shp],
        scratch_shapes=[seq(), seq(), mats(BF16), mats(F32), dc],
        compiler_params=_cparams("parallel", "parallel"),
        name="gdn_ctx",
    )(p, p, g_col, g_row, conv_w, conv_w, hp)


def _outproj_kernel(mh_ref, mg_ref, x_ref, gt_ref, sc_ref, sh_ref, nw_ref, woh_ref, wog_ref,
                    wrh_ref, wrl_ref, br_ref, x1_ref, h2_ref, idx_ref, gate_ref, cnt_ref):
    y = _dot(mh_ref[...], woh_ref[...]) + _dot(mg_ref[...], wog_ref[...])
    x1 = x_ref[...] + gt_ref[...] * y
    x1_ref[...] = x1
    h = _modulated_norm(x1, nw_ref[...], sc_ref[...], sh_ref[...])
    hh = h.astype(BF16)
    half = hh.shape[1] // 2
    hi_bits = lax.bitcast_convert_type(hh[:, :half].astype(F32), jnp.uint32)
    lo_bits = lax.bitcast_convert_type(hh[:, half:].astype(F32), jnp.uint32)
    h2_ref[...] = hi_bits | (lo_bits >> 16)
    hl = (h - hh.astype(F32)).astype(BF16)
    wrh = wrh_ref[...]
    logits = _dot(hh, wrh) + _dot(hl, wrh) + _dot(hh, wrl_ref[...]) + br_ref[...]
    lane = lax.broadcasted_iota(jnp.int32, logits.shape, 1).astype(F32)
    vals, idxs = [], []
    for _ in range(TOP_K):
        m = jnp.max(logits, axis=-1, keepdims=True)
        i = jnp.min(jnp.where(logits == m, lane, float(LANES)), axis=-1, keepdims=True)
        vals.append(m)
        idxs.append(i)
        logits = jnp.where(lane == i, -jnp.inf, logits)
    es = [jnp.exp(v - vals[0]) for v in vals]
    inv = 1.0 / functools.reduce(lambda a, b_: a + b_, es)

    @pl.when(pl.program_id(0) == 0)
    def _():
        cnt_ref[...] = jnp.zeros_like(cnt_ref)

    onehots = [lane == i for i in idxs]
    picked = functools.reduce(lambda a, b_: a | b_, onehots)
    tm = lane.shape[0]
    before = (lax.broadcasted_iota(jnp.int32, (tm, tm), 1) < lax.broadcasted_iota(jnp.int32, (tm, tm), 0))
    prior = cnt_ref[...] + _dot(before.astype(BF16), picked.astype(BF16))
    ranks = [jnp.sum(jnp.where(oh, prior, 0.0), axis=-1, keepdims=True) for oh in onehots]
    cnt_ref[...] += jnp.sum(picked.astype(F32), axis=0, keepdims=True)

    idx_out = jnp.zeros(lane.shape, F32)
    gate_out = jnp.zeros(lane.shape, F32)
    for k in range(TOP_K):
        idx_out = jnp.where(lane == k, idxs[k], idx_out)
        idx_out = jnp.where(lane == TOP_K + k, ranks[k], idx_out)
        gate_out = jnp.where(lane == k, es[k] * inv, gate_out)
    idx_ref[...] = idx_out.astype(jnp.int32)
    gate_ref[...] = gate_out


def _outproj(mix_h, mix_g, x, gt1, sc2, sh2, nw, w_out2, wr_hi, wr_lo, br):
    b, t, d = x.shape
    w = mix_h.shape[-1]
    tm = min(t, 512)
    tpb = t // tm
    row = lambda width: pl.BlockSpec((None, tm, width), lambda i: (i // tpb, i % tpb, 0))
    per_b = pl.BlockSpec((None, 1, d), lambda i: (i // tpb, 0, 0))
    const = lambda r, c: pl.BlockSpec((r, c), lambda i: (0, 0))
    return pl.pallas_call(
        _outproj_kernel,
        grid=(b * tpb,),
        in_specs=[row(w), row(w), row(d), per_b, per_b, per_b, const(1, d),
                  pl.BlockSpec((None, w, d), lambda i: (0, 0, 0)), pl.BlockSpec((None, w, d), lambda i: (1, 0, 0)),
                  const(d, LANES), const(d, LANES), const(1, LANES)],
        out_specs=[row(d), row(d // 2), row(LANES), row(LANES), const(1, LANES)],
        out_shape=[jax.ShapeDtypeStruct((b, t, d), F32), jax.ShapeDtypeStruct((b, t, d // 2), jnp.uint32),
                   jax.ShapeDtypeStruct((b, t, LANES), jnp.int32), jax.ShapeDtypeStruct((b, t, LANES), F32),
                   jax.ShapeDtypeStruct((1, LANES), F32)],
        compiler_params=_cparams("arbitrary"),
        name="outproj",
    )(mix_h, mix_g, x, gt1, sc2, sh2, nw.reshape(1, d), w_out2, w_out2, wr_hi, wr_lo, br)


def _dispatch_kernel(dest_ref, zrow_ref, h_ref, xs_ref, zbuf, sem, zsem, *, tt, k, tm, n_exp):
    @pl.when(pl.program_id(0) == 0)
    def _():
        zbuf[...] = jnp.zeros_like(zbuf)
        zero_block = lambda e: pltpu.make_async_copy(
            zbuf, xs_ref.at[pl.ds(pl.multiple_of(zrow_ref[e], tm), tm), :], zsem)
        for e in range(n_exp):
            @pl.when(zrow_ref[e] >= 0)
            def _():
                zero_block(e).start()
        for e in range(n_exp):
            @pl.when(zrow_ref[e] >= 0)
            def _():
                zero_block(e).wait()

        def unused_block(b):
            return pltpu.make_async_copy(zbuf, xs_ref.at[pl.ds(pl.multiple_of(b * tm, tm), tm), :], zsem)

        def start_unused(b, carry):
            unused_block(b).start()
            return carry

        def wait_unused(b, carry):
            unused_block(b).wait()
            return carry

        lax.fori_loop(zrow_ref[n_exp], xs_ref.shape[0] // tm, start_unused, 0)
        lax.fori_loop(zrow_ref[n_exp], xs_ref.shape[0] // tm, wait_unused, 0)

    base = pl.program_id(0) * (tt * k)

    def body(tb, carry):
        t0 = pl.multiple_of(tb * 8, 8)
        for u in range(8):
            for kk in range(k):
                row = dest_ref[base + (t0 + u) * k + kk]
                pltpu.make_async_copy(h_ref.at[pl.ds(t0 + u, 1), :], xs_ref.at[pl.ds(row, 1), :], sem).start()
        return carry

    lax.fori_loop(0, tt // 8, body, 0)
    done = xs_ref.at[pl.ds(0, tt * k), :]
    pltpu.make_async_copy(done, done, sem).wait()


def _dispatch(h2, dest, zrow, rows, k, tm):
    n_tok, w = h2.shape
    tt = min(n_tok, 512)
    n_exp = zrow.shape[0] - 1
    return pl.pallas_call(
        functools.partial(_dispatch_kernel, tt=tt, k=k, tm=tm, n_exp=n_exp),
        grid_spec=pltpu.PrefetchScalarGridSpec(
            num_scalar_prefetch=2,
            grid=(n_tok // tt,),
            in_specs=[pl.BlockSpec((tt, w), lambda i, dest_, zrow_: (i, 0))],
            out_specs=pl.BlockSpec(memory_space=pl.ANY),
            scratch_shapes=[pltpu.VMEM((tm, w), h2.dtype), pltpu.SemaphoreType.DMA(()),
                            pltpu.SemaphoreType.DMA(())]),
        out_shape=jax.ShapeDtypeStruct((rows, w), h2.dtype),
        compiler_params=_cparams("arbitrary"),
        name="dispatch",
    )(dest, zrow, h2)


def _cast_tile(src, dst):
    rows = 256

    def body(i, carry):
        sl = pl.ds(pl.multiple_of(i * rows, rows), rows)
        dst[sl, :] = src[sl, :].astype(dst.dtype)
        return carry

    lax.fori_loop(0, src.shape[0] // rows, body, 0)


S_E, S_J, S_R, S_RO, S_JO, S_FIRST, S_VALID, S_NE, S_NJ, S_SLOT = range(10)


def _weight_group_prefetch(sc_ref, copies, on_ready):
    s = pl.program_id(0)

    @pl.when(s == 0)
    def _():
        for c in copies(sc_ref[S_E, 0], sc_ref[S_J, 0], 0):
            c.start()

    @pl.when(sc_ref[S_FIRST, s] == 1)
    def _():
        slot = sc_ref[S_SLOT, s]

        @pl.when(sc_ref[S_NE, s] >= 0)
        def _():
            for c in copies(sc_ref[S_NE, s], sc_ref[S_NJ, s], 1 - slot):
                c.start()

        for c in copies(sc_ref[S_E, s], sc_ref[S_J, s], slot):
            c.wait()
        on_ready(slot)


def _moe_up_kernel(sc_ref, x_ref, bg_ref, bu_ref, wg_hbm, wu_hbm, act_ref, stage, wg_b, wu_b, sems, *, tf):
    s = pl.program_id(0)

    def copies(e, j, slot):
        cols = pl.ds(pl.multiple_of(j * tf, tf), tf)
        return [pltpu.make_async_copy(wg_hbm.at[e, :, cols], stage.at[slot, 0], sems.at[slot, 0]),
                pltpu.make_async_copy(wu_hbm.at[e, :, cols], stage.at[slot, 1], sems.at[slot, 1])]

    def on_ready(slot):
        _cast_tile(stage.at[slot, 0], wg_b)
        _cast_tile(stage.at[slot, 1], wu_b)

    _weight_group_prefetch(sc_ref, copies, on_ready)

    @pl.when(sc_ref[S_VALID, s] == 1)
    def _():
        xp = x_ref[...]
        half = xp.shape[1]
        xa = lax.bitcast_convert_type(xp & jnp.uint32(0xFFFF0000), F32).astype(BF16)
        xb = lax.bitcast_convert_type(xp << 16, F32).astype(BF16)
        gate = _dot(xa, wg_b[:half, :]) + _dot(xb, wg_b[half:, :]) + bg_ref[...]
        up = _dot(xa, wu_b[:half, :]) + _dot(xb, wu_b[half:, :]) + bu_ref[...]
        gate = jnp.minimum(gate, SWIGLU_LIMIT)
        up = jnp.clip(up, -SWIGLU_LIMIT, SWIGLU_LIMIT)
        act_ref[...] = ((up + 1.0) * gate * _sigmoid(SWIGLU_ALPHA * gate)).astype(act_ref.dtype)

    @pl.when(sc_ref[S_VALID, s] == 0)
    def _():
        act_ref[...] = jnp.zeros_like(act_ref)


def _moe_down_kernel(sc_ref, a_ref, bd_ref, wd_hbm, y_ref, stage, wd_b, sems, *, tf):
    s = pl.program_id(0)

    def copies(e, j, slot):
        del j
        return [pltpu.make_async_copy(wd_hbm.at[e], stage.at[slot], sems.at[slot])]

    _weight_group_prefetch(sc_ref, copies, lambda slot: _cast_tile(stage.at[slot], wd_b))

    @pl.when(sc_ref[S_VALID, s] == 1)
    def _():
        cols = pl.ds(pl.multiple_of(sc_ref[S_J, s] * tf, tf), tf)
        y_ref[...] = _dot(a_ref[...], wd_b[:, cols]) + bd_ref[...]

    @pl.when(sc_ref[S_VALID, s] == 0)
    def _():
        y_ref[...] = jnp.zeros_like(y_ref)


def _moe_experts(xs, sched_up, sched_down, wg, wu, wd, bg, bu, bd, tm, tf):
    rows = xs.shape[0]
    n_exp, d, f = wg.shape
    assert xs.shape[1] * 2 == d and d // tf == f // tf
    steps = sched_up.shape[1]
    blk = lambda shape, at: pl.BlockSpec(shape, lambda s, sc: at(sc, s))
    hbm = pl.BlockSpec(memory_space=pl.ANY)
    act = pl.pallas_call(
        functools.partial(_moe_up_kernel, tf=tf),
        grid_spec=pltpu.PrefetchScalarGridSpec(
            num_scalar_prefetch=1,
            grid=(steps,),
            in_specs=[blk((tm, d // 2), lambda sc, s: (sc[S_R, s], 0)),
                      blk((None, 1, tf), lambda sc, s: (sc[S_E, s], 0, sc[S_J, s])),
                      blk((None, 1, tf), lambda sc, s: (sc[S_E, s], 0, sc[S_J, s])),
                      hbm, hbm],
            out_specs=blk((tm, tf), lambda sc, s: (sc[S_RO, s], sc[S_JO, s])),
            scratch_shapes=[pltpu.VMEM((2, 2, d, tf), F32), pltpu.VMEM((d, tf), BF16), pltpu.VMEM((d, tf), BF16),
                            pltpu.SemaphoreType.DMA((2, 2))]),
        out_shape=jax.ShapeDtypeStruct((rows, f), BF16),
        compiler_params=_cparams("arbitrary"),
        name="moe_up",
    )(sched_up, xs, bg.reshape(n_exp, 1, f), bu.reshape(n_exp, 1, f), wg, wu)
    return pl.pallas_call(
        functools.partial(_moe_down_kernel, tf=tf),
        grid_spec=pltpu.PrefetchScalarGridSpec(
            num_scalar_prefetch=1,
            grid=(steps,),
            in_specs=[blk((tm, f), lambda sc, s: (sc[S_R, s], 0)),
                      blk((None, 1, tf), lambda sc, s: (sc[S_E, s], 0, sc[S_J, s])),
                      hbm],
            out_specs=blk((tm, tf), lambda sc, s: (sc[S_RO, s], sc[S_JO, s])),
            scratch_shapes=[pltpu.VMEM((2, f, d), F32), pltpu.VMEM((f, d), BF16), pltpu.SemaphoreType.DMA((2,))]),
        out_shape=jax.ShapeDtypeStruct((rows, d), F32),
        compiler_params=_cparams("arbitrary"),
        name="moe_down",
    )(sched_down, act, bd.reshape(n_exp, 1, d), wd)


def _route(top_idx, rank, counts, tm, n_tiles):
    n_tok, k = top_idx.shape
    n_exp = counts.shape[0]
    m = n_tok * k
    padded = (counts + tm - 1) // tm * tm
    pend = jnp.cumsum(padded)
    onehot = top_idx.reshape(m, 1) == jnp.arange(n_exp, dtype=jnp.int32)[None, :]
    dest = jnp.sum(jnp.where(onehot, (pend - padded)[None, :], 0), axis=1) + rank.reshape(m)
    n_blocks = -(-m // tm) + n_exp

    nb = padded // tm
    blk0 = (pend - padded) // tm
    cum = jnp.cumsum(nb * n_tiles)
    total = cum[-1]
    s = jnp.arange(n_tiles * n_blocks, dtype=jnp.int32)
    valid = s < total
    sc = jnp.minimum(s, total - 1)
    e = jnp.minimum(jnp.sum((cum[None, :] <= sc[:, None]).astype(jnp.int32), axis=1), n_exp - 1)
    of_e = e[:, None] == jnp.arange(n_exp, dtype=jnp.int32)[None, :]
    pick = lambda v: jnp.sum(jnp.where(of_e, v[None, :], 0), axis=1)
    nb_e = pick(nb)
    local = sc - (pick(cum) - nb_e * n_tiles)
    nbe = jnp.maximum(nb_e, 1)
    blk0_e = pick(blk0)
    extra = s - total
    ids = jnp.arange(n_exp, dtype=jnp.int32)
    later = jnp.where((ids[None, :] > ids[:, None]) & (nb[None, :] > 0), ids[None, :], n_exp)
    next_e = jnp.min(later, axis=1)
    next_e = pick(jnp.where(next_e == n_exp, -1, next_e))
    erank = pick(jnp.cumsum((nb > 0).astype(jnp.int32)) - 1)

    def rows_of(j, r, first, ne, nj, group):
        r_out = jnp.where(valid, r, total // n_tiles + extra // n_tiles)
        j_out = jnp.where(valid, j, extra % n_tiles)
        fields = (e, j, r, r_out, j_out, valid & first, valid, ne, nj, group % 2)
        return jnp.stack([a.astype(jnp.int32) for a in fields])

    j_up = local // nbe
    last_tile = j_up == n_tiles - 1
    sched_up = rows_of(j_up, blk0_e + local % nbe, local % nbe == 0,
                       jnp.where(last_tile, next_e, e), jnp.where(last_tile, 0, j_up + 1),
                       erank * n_tiles + j_up)
    sched_down = rows_of(local % n_tiles, blk0_e + local // n_tiles, local == 0, next_e, jnp.zeros_like(e), erank)
    zrow = jnp.concatenate([jnp.where(padded > 0, pend - tm, -1), pend[-1:] // tm]).astype(jnp.int32)
    return dest.astype(jnp.int32), zrow, n_blocks * tm, sched_up, sched_down


def _combine_kernel(dest_ref, x1_ref, gate_ref, gt_ref, nw_ref, ys_ref, o_ref, ybuf, sems, *, tt, k):
    i = pl.program_id(0)
    n = pl.num_programs(0)

    def start_gather(tile, slot):
        base = tile * (tt * k)

        def body(tb, carry):
            for u in range(8):
                for j in range(k):
                    row = dest_ref[base + (tb * 8 + u) * k + j]
                    pltpu.make_async_copy(ys_ref.at[row >> 3, pl.ds(row & 7, 1), :],
                                          ybuf.at[slot, j * (tt // 8) + tb, pl.ds(u, 1), :], sems.at[slot]).start()
            return carry

        lax.fori_loop(0, tt // 8, body, 0)

    @pl.when(i == 0)
    def _():
        start_gather(0, 0)

    for nxt in (0, 1):
        @pl.when((i + 1 < n) & ((i + 1) % 2 == nxt))
        def _():
            start_gather(i + 1, nxt)

    slot = i % 2
    pltpu.make_async_copy(ybuf.at[slot], ybuf.at[slot], sems.at[slot]).wait()
    g = gate_ref[...]
    d = o_ref.shape[-1]
    picked = lambda j: ybuf[slot, pl.ds(j * (tt // 8), tt // 8), :, :].reshape(tt, d)
    moe = picked(0) * g[:, 0:1]
    for j in range(1, k):
        moe = moe + picked(j) * g[:, j:j + 1]
    x = x1_ref[...] + gt_ref[...] * moe
    ms = jnp.mean(x * x, axis=-1, keepdims=True)
    o_ref[...] = x * lax.rsqrt(ms + EPS) * nw_ref[...]


def _combine(x1, ys, dest, gates, gt2, nw, k):
    b, t, d = x1.shape
    tt = min(t, 256)
    tpb = t // tt
    row = lambda width: pl.BlockSpec((None, tt, width), lambda i, dest_: (i // tpb, i % tpb, 0))
    return pl.pallas_call(
        functools.partial(_combine_kernel, tt=tt, k=k),
        grid_spec=pltpu.PrefetchScalarGridSpec(
            num_scalar_prefetch=1,
            grid=(b * tpb,),
            in_specs=[row(d), row(LANES),
                      pl.BlockSpec((None, 1, d), lambda i, dest_: (i // tpb, 0, 0)),
                      pl.BlockSpec((1, d), lambda i, dest_: (0, 0)),
                      pl.BlockSpec(memory_space=pl.ANY)],
            out_specs=row(d),
            scratch_shapes=[pltpu.VMEM((2, k * tt // 8, 8, d), F32), pltpu.SemaphoreType.DMA((2,))]),
        out_shape=jax.ShapeDtypeStruct((b, t, d), F32),
        compiler_params=_cparams("arbitrary"),
        name="combine",
    )(dest, x1, gates, gt2, nw.reshape(1, d), ys.reshape(ys.shape[0] // 8, 8, d))


def kernel(x, c, ctx, c_ctx, w_ada, b_ada, norm_mix_w, w_in, hg_lb_f, hg_lb_b, hg_norm_w, gd_conv_w,
           gd_a_log_f, gd_a_log_b, gd_dt_bias_f, gd_dt_bias_b, gd_norm_w, w_out, norm_ffn_w, w_router,
           b_router, w_gate, b_gate, w_up, b_up, w_down, b_down, norm_out_w):
    bsz, t, d = x.shape
    t_ctx = ctx.shape[1]
    n_exp = w_router.shape[-1]
    l = 0
    hg_w = N_HEADS * HEAD_DIM
    n_main = 9 * hg_w

    lb_f = jnp.cumsum(jax.nn.softmax(hg_lb_f.astype(F32), axis=0), axis=0)[l].reshape(1, hg_w)
    lb_b = jnp.cumsum(jax.nn.softmax(hg_lb_b.astype(F32), axis=0), axis=0)[l].reshape(1, hg_w)

    rows = -(-(bsz + 1) // 8) * 8
    cc = jnp.zeros((rows, d), F32).at[:bsz].set(c).at[bsz].set(c_ctx)
    mod = _ada(cc, w_ada[l], b_ada[l])
    sh1, sc1, gt1, sh2, sc2, gt2 = (mod[:bsz, i * d:(i + 1) * d].reshape(bsz, 1, d) for i in range(6))
    csh1 = jnp.broadcast_to(mod[bsz, 0:d].reshape(1, 1, d), (bsz, 1, d))
    csc1 = jnp.broadcast_to(mod[bsz, d:2 * d].reshape(1, 1, d), (bsz, 1, d))

    w_in_b = w_in[l].astype(BF16)
    w_gates = jnp.pad(w_in_b[:, n_main:], ((0, 0), (0, LANES - (w_in_b.shape[1] - n_main))))
    px, gx = _inproj(x, norm_mix_w[l], sc1, sh1, w_in_b, n_main, w_gates)
    pc, gc = _inproj(ctx, norm_mix_w[l], csc1, csh1, w_in_b, n_main, w_gates)

    hg_nw = hg_norm_w[l].reshape(1, HEAD_DIM)
    hs_f, hs_b = _hgrn_states(pc, lb_f, lb_b)
    mix_h = _hgrn_out(px, lb_f, lb_b, hg_nw, hs_f, hs_b)

    hp = jnp.stack([gd_a_log_f[l], gd_dt_bias_f[l], gd_a_log_b[l], gd_dt_bias_b[l]], axis=1)
    hp = jnp.broadcast_to(jnp.pad(hp, ((0, 0), (0, 4)))[:, :, None], (N_HEADS, 8, HEAD_DIM)).astype(F32)
    gd_nw = gd_norm_w[l].reshape(1, HEAD_DIM)
    gs_f, gs_b = _gdn_call(pc, gc, gd_conv_w[l], hp, gd_nw, None, None, t_ctx, False, 5)
    mix_g = _gdn_call(px, gx, gd_conv_w[l], hp, gd_nw, gs_f, gs_b, GRID_W, True, 5)

    w_out_b = w_out[l].astype(BF16)
    wr = jnp.pad(w_router[l], ((0, 0), (0, LANES - n_exp)))
    wr_hi = wr.astype(BF16)
    wr_lo = (wr - wr_hi.astype(F32)).astype(BF16)
    br = jnp.full((1, LANES), NEG_BIG, F32).at[0, :n_exp].set(b_router[l])
    x1, h2, idx_pad, gate_pad, cnt = _outproj(mix_h, mix_g, x, gt1, sc2, sh2, norm_ffn_w[l],
                                              w_out_b.reshape(2, hg_w, d), wr_hi, wr_lo, br)

    n_tok = bsz * t
    idx_pad = idx_pad.reshape(n_tok, LANES)
    dest, zrow, n_rows, sched_up, sched_down = _route(idx_pad[:, :TOP_K], idx_pad[:, TOP_K:2 * TOP_K],
                                                      cnt[0, :n_exp].astype(jnp.int32), MOE_TM,
                                                      w_gate.shape[-1] // MOE_TF)
    xs = _dispatch(h2.reshape(n_tok, d // 2), dest, zrow, n_rows, TOP_K, MOE_TM)
    ys = _moe_experts(xs, sched_up, sched_down, w_gate[l], w_up[l], w_down[l], b_gate[l], b_up[l], b_down[l],
                      MOE_TM, MOE_TF)

    return _combine(x1, ys, dest, gate_pad, gt2, norm_out_w, TOP_K)
```

```python
import functools

import jax
import jax.numpy as jnp
from jax import lax
from jax.experimental import pallas as pl
from jax.experimental.pallas import tpu as pltpu

F32 = jnp.float32
BF16 = jnp.bfloat16

EPS = 1e-6
CHUNK = 64
HEAD_DIM = 128
N_HEADS = 8
GRID_W = 64
TOP_K = 4
SWIGLU_LIMIT = 7.0
SWIGLU_ALPHA = 1.702
LANES = 128
NEG_BIG = -1e30

HGRN_GROUP = 8
GDN_GROUP = 4
GDN_INTRA_GROUP = 16
GDN_HEADS = 2
MOE_TM = 512
MOE_TF = 1024
VMEM_LIMIT = 56 * 1024 * 1024


def _cparams(*sem):
    return pltpu.CompilerParams(dimension_semantics=sem, vmem_limit_bytes=VMEM_LIMIT)


def _dot(a, b):
    return jnp.dot(a, b, preferred_element_type=F32)


def _dot_nt(a, b):
    return lax.dot_general(a, b, (((1,), (1,)), ((), ())), preferred_element_type=F32)


def _split2(x):
    hi = x.astype(BF16)
    return hi, (x - hi.astype(F32)).astype(BF16)


def _split3(x):
    hi = x.astype(BF16)
    rest = x - hi.astype(F32)
    mid = rest.astype(BF16)
    return hi, mid, (rest - mid.astype(F32)).astype(BF16)


def _dot_split(a, b):
    ah, al = a
    bh, bl = b
    return _dot(ah, bh) + (_dot(ah, bl) + _dot(al, bh))


def _sigmoid(x):
    return 1.0 / (1.0 + jnp.exp(-x))


def _ada_kernel(c_ref, w_ref, b_ref, o_ref):
    c = c_ref[...]
    s = (c * _sigmoid(c)).astype(BF16)
    o_ref[...] = _dot(s, w_ref[...].astype(BF16)) + b_ref[...]


def _ada(cc, w_ada, b_ada):
    rows, d = cc.shape
    n = w_ada.shape[1]
    tn = 1024
    return pl.pallas_call(
        _ada_kernel,
        grid=(n // tn,),
        in_specs=[pl.BlockSpec((rows, d), lambda j: (0, 0)),
                  pl.BlockSpec((d, tn), lambda j: (0, j)),
                  pl.BlockSpec((1, tn), lambda j: (0, j))],
        out_specs=pl.BlockSpec((rows, tn), lambda j: (0, j)),
        out_shape=jax.ShapeDtypeStruct((rows, n), F32),
        compiler_params=_cparams("arbitrary"),
        name="ada",
    )(cc, w_ada, b_ada.reshape(1, n))


def _modulated_norm(x, nw, sc, sh):
    ms = jnp.mean(x * x, axis=-1, keepdims=True)
    return (x * lax.rsqrt(ms + EPS) * nw) * (1.0 + sc) + sh


def _inproj_kernel(x_ref, nw_ref, sc_ref, sh_ref, w_ref, wg_ref, o_ref, og_ref, h_scr):
    @pl.when(pl.program_id(1) == 0)
    def _():
        h = _modulated_norm(x_ref[...], nw_ref[...], sc_ref[...], sh_ref[...]).astype(BF16)
        h_scr[...] = h
        og_ref[...] = _dot(h, wg_ref[...])

    o_ref[...] = _dot(h_scr[...], w_ref[...])


def _inproj(x, nw, sc, sh, w_all, n, w_gates):
    b, t, d = x.shape
    w_main = w_all
    tm = min(t, 1024)
    tn = 1024
    tpb = t // tm
    assert n % tn == 0
    return pl.pallas_call(
        _inproj_kernel,
        grid=(b * tpb, n // tn),
        in_specs=[pl.BlockSpec((None, tm, d), lambda i, j: (i // tpb, i % tpb, 0)),
                  pl.BlockSpec((1, d), lambda i, j: (0, 0)),
                  pl.BlockSpec((None, 1, d), lambda i, j: (i // tpb, 0, 0)),
                  pl.BlockSpec((None, 1, d), lambda i, j: (i // tpb, 0, 0)),
                  pl.BlockSpec((d, tn), lambda i, j: (0, j)),
                  pl.BlockSpec((d, LANES), lambda i, j: (0, 0))],
        out_specs=[pl.BlockSpec((None, tm, tn), lambda i, j: (i // tpb, i % tpb, j)),
                   pl.BlockSpec((None, tm, LANES), lambda i, j: (i // tpb, i % tpb, 0))],
        out_shape=[jax.ShapeDtypeStruct((b, t, n), F32),
                   jax.ShapeDtypeStruct((b, t, LANES), F32)],
        scratch_shapes=[pltpu.VMEM((tm, d), BF16)],
        compiler_params=_cparams("parallel", "arbitrary"),
        name="inproj",
    )(x, nw.reshape(1, d), sc, sh, w_main, w_gates)


def _chunk_masks(rev, wide=False):
    shape = (CHUNK, 2 * CHUNK if wide else CHUNK)
    r = lax.broadcasted_iota(jnp.int32, shape, 0)
    c = lax.broadcasted_iota(jnp.int32, shape, 1) & (CHUNK - 1)
    incl = (c >= r) if rev else (c <= r)
    strict = (c > r) if rev else (c < r)
    return incl, strict


def _gated_head_norm(o, z, nw):
    ms = jnp.mean(o * o, axis=-1, keepdims=True)
    return (o * lax.rsqrt(ms + EPS) * nw) * (z * _sigmoid(z))


def _group_slices(g, group, n_chunks):
    idxs, revs = [], []
    for i in range(group):
        n = g * group + i
        idxs += [n, n_chunks - 1 - n]
        revs += [False, True]
    return [pl.ds(pl.multiple_of(ix * CHUNK, CHUNK), CHUNK) for ix in idxs], idxs, revs


def _scan_group(n_chunks, want, two_phase):
    g = want
    while g > 1 and (n_chunks % g or (two_phase and (n_chunks // g) % 2)):
        g //= 2
    assert n_chunks % g == 0 and not (two_phase and (n_chunks // g) % 2)
    return g


def _gla_group(zfs, vs, qs, lbs, revs):
    n = len(zfs)
    incl = [_chunk_masks(r)[0] for r in revs]
    logf, k = [], []
    for i in range(n):
        one_m = 1.0 - lbs[i]
        logf.append(jnp.log(lbs[i] + one_m * _sigmoid(zfs[i])))
        k.append(one_m * _sigmoid(-zfs[i]))
    cum = []
    for i in range(n):
        tri = incl[i].astype(BF16)
        hi, mid, lo = _split3(logf[i])
        parts = _dot(tri, jnp.concatenate([hi, mid, lo], axis=1))
        cum.append(parts[:, :HEAD_DIM] + (parts[:, HEAD_DIM:2 * HEAD_DIM] + parts[:, 2 * HEAD_DIM:]))
    last = [cum[i][0:1] if revs[i] else cum[i][CHUNK - 1:CHUNK] for i in range(n)]
    u_t = [_dot(vs[i].T.astype(BF16), (k[i] * jnp.exp(last[i] - cum[i])).astype(BF16)) for i in range(n)]
    dec = [jnp.exp(x) for x in last]
    if qs is None:
        return u_t, dec, None, None
    scores = []
    for i in range(n):
        mid_row = CHUNK // 2 - 1 if revs[i] else CHUNK // 2
        ref = cum[i][mid_row:mid_row + 1]
        sc = _dot_nt((qs[i] * jnp.exp(cum[i] - ref)).astype(BF16), (k[i] * jnp.exp(ref - cum[i])).astype(BF16))
        scores.append(jnp.where(incl[i], sc, 0.0).astype(BF16))
    o_intra = [_dot(scores[i], vs[i].astype(BF16)) for i in range(n)]
    qe = [(qs[i] * jnp.exp(cum[i])).astype(BF16) for i in range(n)]
    return u_t, dec, o_intra, qe


def _hgrn_out_kernel(q_ref, ff_ref, fb_ref, v_ref, g_ref, lbf_ref, lbb_ref, nw_ref, s0f_ref, s0b_ref,
                     o_ref, o_scr, sf_scr, sb_scr, *, n_chunks, group):
    sf_scr[...] = s0f_ref[...]
    sb_scr[...] = s0b_ref[...]
    lbf, lbb, nw = lbf_ref[...], lbb_ref[...], nw_ref[...]

    def body(g, carry, final):
        sls, _, revs = _group_slices(g, group, n_chunks)
        zfs = [(fb_ref if r else ff_ref)[sl, :] for sl, r in zip(sls, revs)]
        vs = [v_ref[sl, :] for sl in sls]
        qs = [q_ref[sl, :] for sl in sls]
        u_t, dec, o_intra, qe = _gla_group(zfs, vs, qs, [lbb if r else lbf for r in revs], revs)
        for i, (sl, rev) in enumerate(zip(sls, revs)):
            s_scr = sb_scr if rev else sf_scr
            st = s_scr[...]
            o = o_intra[i] + _dot_nt(qe[i], st.astype(BF16))
            s_scr[...] = st * dec[i] + u_t[i]
            if final:
                o_ref[sl, :] = _gated_head_norm(o_scr[sl, :] + o, g_ref[sl, :], nw).astype(o_ref.dtype)
            else:
                o_scr[sl, :] = o
        return carry

    trips = n_chunks // group
    lax.fori_loop(0, trips // 2, functools.partial(body, final=False), 0)
    lax.fori_loop(trips // 2, trips, functools.partial(body, final=True), 0)


def _hgrn_state_kernel(ff_ref, fb_ref, v_ref, lbf_ref, lbb_ref, sf_ref, sb_ref, *, n_chunks, group):
    sf_ref[...] = jnp.zeros_like(sf_ref)
    sb_ref[...] = jnp.zeros_like(sb_ref)
    lbf, lbb = lbf_ref[...], lbb_ref[...]

    def body(g, carry):
        sls, _, revs = _group_slices(g, group, n_chunks)
        zfs = [(fb_ref if r else ff_ref)[sl, :] for sl, r in zip(sls, revs)]
        vs = [v_ref[sl, :] for sl in sls]
        u_t, dec, _, _ = _gla_group(zfs, vs, None, [lbb if r else lbf for r in revs], revs)
        for i, rev in enumerate(revs):
            s_ref = sb_ref if rev else sf_ref
            s_ref[...] = s_ref[...] * dec[i] + u_t[i]
        return carry

    lax.fori_loop(0, n_chunks // group, body, 0)


def _col_spec(t, section):
    return pl.BlockSpec((None, t, HEAD_DIM), lambda b, h: (b, 0, section * N_HEADS + h))


def _head_row_spec(offset=0):
    return pl.BlockSpec((1, HEAD_DIM), lambda b, h: (0, offset + h))


_STATE_SPEC = pl.BlockSpec((None, None, HEAD_DIM, HEAD_DIM), lambda b, h: (b, h, 0, 0))


def _hgrn_states(pc, lbf, lbb):
    b, t, _ = pc.shape
    n_chunks = t // CHUNK
    shp = jax.ShapeDtypeStruct((b, N_HEADS, HEAD_DIM, HEAD_DIM), F32)
    return pl.pallas_call(
        functools.partial(_hgrn_state_kernel, n_chunks=n_chunks, group=_scan_group(n_chunks, HGRN_GROUP, False)),
        grid=(b, N_HEADS),
        in_specs=[_col_spec(t, 1), _col_spec(t, 2), _col_spec(t, 3), _head_row_spec(), _head_row_spec()],
        out_specs=[_STATE_SPEC, _STATE_SPEC],
        out_shape=[shp, shp],
        compiler_params=_cparams("parallel", "parallel"),
        name="hgrn_ctx",
    )(pc, pc, pc, lbf, lbb)


def _hgrn_out(px, lbf, lbb, nw, s0f, s0b):
    b, t, _ = px.shape
    n_chunks = t // CHUNK
    return pl.pallas_call(
        functools.partial(_hgrn_out_kernel, n_chunks=n_chunks, group=_scan_group(n_chunks, HGRN_GROUP, True)),
        grid=(b, N_HEADS),
        in_specs=[_col_spec(t, 0), _col_spec(t, 1), _col_spec(t, 2), _col_spec(t, 3), _col_spec(t, 4),
                  _head_row_spec(), _head_row_spec(),
                  pl.BlockSpec((1, HEAD_DIM), lambda b_, h: (0, 0)),
                  _STATE_SPEC, _STATE_SPEC],
        out_specs=pl.BlockSpec((None, t, HEAD_DIM), lambda b_, h: (b_, 0, h)),
        out_shape=jax.ShapeDtypeStruct((b, t, N_HEADS * HEAD_DIM), BF16),
        scratch_shapes=[pltpu.VMEM((t, HEAD_DIM), F32),
                        pltpu.VMEM((HEAD_DIM, HEAD_DIM), F32),
                        pltpu.VMEM((HEAD_DIM, HEAD_DIM), F32)],
        compiler_params=_cparams("parallel", "parallel"),
        name="hgrn_x",
    )(px, px, px, px, px, lbf, lbb, nw, s0f, s0b)


def _conv_silu(a, w, period):
    rows = a.shape[0]
    pos = lax.broadcasted_iota(jnp.int32, a.shape, 0) % period
    prev = jnp.where(pos == 0, 0.0, pltpu.roll(a, 1, axis=0))
    nxt = jnp.where(pos == period - 1, 0.0, pltpu.roll(a, rows - 1, axis=0))
    y = prev * w[0:1] + a * w[1:2] + nxt * w[2:3]
    return y * _sigmoid(y)


def _l2norm(x):
    return x * lax.rsqrt(jnp.sum(x * x, axis=-1, keepdims=True) + EPS)


def _softplus(x):
    return jnp.maximum(x, 0.0) + jnp.log1p(jnp.exp(-jnp.abs(x)))


def _unit_tri_inverses(tris):
    row = lax.broadcasted_iota(jnp.int32, (CHUNK, 2 * CHUNK), 0)
    lane = lax.broadcasted_iota(jnp.int32, (CHUNK, 2 * CHUNK), 1)
    left = lane < CHUNK
    eye = jnp.where((lane & (CHUNK - 1)) == row, 1.0, 0.0)
    zeros = jnp.zeros((CHUNK, 2 * CHUNK), BF16)

    def lhs(hl):
        return jnp.where(left, hl[0], hl[1])

    def rhs(hl):
        top = jnp.concatenate([hl[0], hl[1]], axis=1)
        return jnp.concatenate([top, jnp.concatenate([hl[0], zeros], axis=1)], axis=0)

    def fold(o):
        return o[:, :2 * CHUNK] + o[:, 2 * CHUNK:]

    ps = [_split2(-a) for a in tris]
    invs = [eye - a for a in tris]
    ps = [_split2(fold(_dot(lhs(p), rhs(p)))) for p in ps]
    for _ in range(CHUNK.bit_length() - 3):
        outs = [fold(_dot(jnp.concatenate([lhs(p), lhs(_split2(inv))], axis=0), rhs(p)))
                for p, inv in zip(ps, invs)]
        invs = [inv + o[CHUNK:] for inv, o in zip(invs, outs)]
        ps = [_split2(o[:CHUNK]) for o in outs]
    invs = [inv + fold(_dot(lhs(_split2(inv)), rhs(p))) for inv, p in zip(invs, ps)]
    return [inv[:, :CHUNK] for inv in invs]


def _gdn_group(qs, ks, vs, a_cs, b_cs, a_rs, alogs, dtbs, revs):
    n = len(ks)
    left = lax.broadcasted_iota(jnp.int32, (CHUNK, 2 * CHUNK), 1) < CHUNK
    dmask, ecum, beta, eend, dec, strict = [], [], [], [], [], []
    for i in range(n):
        incl, st = _chunk_masks(revs[i], wide=True)
        incl_t = _chunk_masks(not revs[i], wide=True)[0]
        scale = -jnp.exp(alogs[i])
        g_c = scale * _softplus(a_cs[i] + dtbs[i])
        g_r = scale * _softplus(a_rs[i] + dtbs[i])
        cum_c = jnp.sum(jnp.where(incl & left, g_r, 0.0), axis=1, keepdims=True)
        cum_r = jnp.sum(jnp.where(incl_t, g_c, 0.0), axis=0, keepdims=True)
        dmask.append(jnp.exp(jnp.where(incl, cum_c - cum_r, -jnp.inf)))
        ecum.append(jnp.exp(cum_c))
        beta.append(_sigmoid(b_cs[i]))
        last = cum_c[0:1] if revs[i] else cum_c[CHUNK - 1:CHUNK]
        eend.append(jnp.exp(last - cum_c))
        dec.append(jnp.exp(last))
        strict.append(st)
    kbf = [k.astype(BF16) for k in ks]
    kb = [ks[i] * beta[i] for i in range(n)]
    kk = [_dot_nt(kb[i].astype(BF16), jnp.concatenate([kbf[i], kbf[i]], axis=0)) for i in range(n)]
    tris = [jnp.where(strict[i], kk[i] * dmask[i], 0.0) for i in range(n)]
    tinv = [t.astype(BF16) for t in _unit_tri_inverses(tris)]
    dmask = [m[:, :CHUNK] for m in dmask]
    u = [_dot(tinv[i], (vs[i] * beta[i]).astype(BF16)).astype(BF16) for i in range(n)]
    w = [_dot(tinv[i], (kb[i] * ecum[i]).astype(BF16)).astype(BF16) for i in range(n)]
    ke_t = [(ks[i] * eend[i]).T.astype(BF16) for i in range(n)]
    mp = [_dot(ke_t[i], w[i]) for i in range(n)]
    cc = [_dot(ke_t[i], u[i]) for i in range(n)]
    if qs is None:
        return mp, cc, dec, None, None
    attn = [(_dot_nt(qs[i].astype(BF16), kbf[i]) * dmask[i]).astype(BF16) for i in range(n)]
    qp = [qs[i] * ecum[i] - _dot(attn[i], w[i]) for i in range(n)]
    oi = [_dot(attn[i], u[i]) for i in range(n)]
    return mp, cc, dec, qp, oi


def _gdn_kernel(*refs, n_chunks, period, prep_rows, group, igroup, heads, with_out):
    if with_out:
        (q_ref, k_ref, v_ref, z_ref, gc_ref, gr_ref, wq_ref, wk_ref, wv_ref, hp_ref, nw_ref,
         s0f_ref, s0b_ref, o_ref, qn, kn, vn, mp_s, cc_s, dc_s, qp_s, o_scr, sf, sb) = refs
    else:
        (k_ref, v_ref, gc_ref, gr_ref, wk_ref, wv_ref, hp_ref,
         sf, sb, kn, vn, mp_s, cc_s, dc_s) = refs
        q_ref = None

    t = n_chunks * CHUNK
    lanes = lambda hs: slice(hs * HEAD_DIM, (hs + 1) * HEAD_DIM)

    def prep(i, carry):
        sl = pl.ds(pl.multiple_of(i * prep_rows, prep_rows), prep_rows)
        for hs in range(heads):
            ln = lanes(hs)
            kn[hs, sl, :] = _l2norm(_conv_silu(k_ref[sl, ln], wk_ref[:, ln], period))
            vn[hs, sl, :] = _conv_silu(v_ref[sl, ln], wv_ref[:, ln], period)
            if with_out:
                qn[hs, sl, :] = _l2norm(_conv_silu(q_ref[sl, ln], wq_ref[:, ln], period)) * (HEAD_DIM ** -0.5)
        return carry

    lax.fori_loop(0, t // prep_rows, prep, 0)

    trips_i = n_chunks // igroup

    def intra(it, carry):
        hs = it // trips_i
        g = it % trips_i
        hp = hp_ref[hs]
        qs, ks, vs, a_cs, b_cs, a_rs, alogs, dtbs, revs, where = [], [], [], [], [], [], [], [], [], []
        for i in range(igroup):
            n = g * igroup + i
            sl = pl.ds(pl.multiple_of(n * CHUNK, CHUNK), CHUNK)
            gc = gc_ref[hs, sl, :]
            gr = gr_ref[hs, n]
            for d, rev in enumerate((False, True)):
                if with_out:
                    qs.append(qn[hs, sl, :])
                ks.append(kn[hs, sl, :])
                vs.append(vn[hs, sl, :])
                a_cs.append(gc[:, d:d + 1])
                b_cs.append(gc[:, 2 + d:3 + d])
                a_rs.append(gr[d:d + 1, :])
                alogs.append(hp[2 * d:2 * d + 1, 0:1])
                dtbs.append(hp[2 * d + 1:2 * d + 2, 0:1])
                revs.append(rev)
                where.append((d, n, sl))
        mp, cc, dec, qp, oi = _gdn_group(qs if with_out else None, ks, vs, a_cs, b_cs, a_rs, alogs, dtbs, revs)
        for i, (d, n, sl) in enumerate(where):
            mp_s[hs, d, n] = mp[i].astype(BF16)
            cc_s[hs, d, n] = cc[i]
            dc_s[hs, d, n] = jnp.broadcast_to(dec[i], (8, HEAD_DIM))
            if with_out:
                qp_s[hs, d, sl, :] = qp[i].astype(BF16)
                if d == 1:
                    o_scr[hs, sl, :] = oi[i - 1] + oi[i]
        return carry

    lax.fori_loop(0, heads * trips_i, intra, 0)

    if with_out:
        sf[...] = s0f_ref[...]
        sb[...] = s0b_ref[...]
        nw = nw_ref[...]
    else:
        sf[...] = jnp.zeros_like(sf)
        sb[...] = jnp.zeros_like(sb)

    def scan(g, carry, final):
        sls, idxs, revs = _group_slices(g, group, n_chunks)
        for sl, idx, rev in zip(sls, idxs, revs):
            d = int(rev)
            s_ref = sb if rev else sf
            for hs in range(heads):
                s = s_ref[hs]
                sbf = s.astype(BF16)
                s_ref[hs] = s * dc_s[hs, d, idx][0:1, :] - _dot(mp_s[hs, d, idx], sbf) + cc_s[hs, d, idx]
                if with_out:
                    o = o_scr[hs, sl, :] + _dot(qp_s[hs, d, sl, :], sbf)
                    if final:
                        o_ref[sl, lanes(hs)] = _gated_head_norm(o, z_ref[sl, lanes(hs)], nw).astype(o_ref.dtype)
                    else:
                        o_scr[hs, sl, :] = o
        return carry

    trips = n_chunks // group
    if with_out:
        lax.fori_loop(0, trips // 2, functools.partial(scan, final=False), 0)
        lax.fori_loop(trips // 2, trips, functools.partial(scan, final=True), 0)
    else:
        lax.fori_loop(0, trips, functools.partial(scan, final=False), 0)


def _gdn_call(p, gates, conv_w, hp, nw, s0f, s0b, period, with_out, sec0):
    b, t, _ = p.shape
    n_chunks = t // CHUNK
    prep_rows = max(period, min(t, 256))
    assert prep_rows % period == 0 and t % prep_rows == 0
    group = _scan_group(n_chunks, GDN_GROUP, with_out)
    g4 = gates[:, :, :4 * N_HEADS].reshape(b, t, 4, N_HEADS)
    g_col = g4.transpose(0, 3, 1, 2)
    g_row = jnp.pad(g4.transpose(0, 3, 2, 1), ((0, 0), (0, 0), (0, 4), (0, 0)))
    g_row = g_row.reshape(b, N_HEADS, 8, n_chunks, CHUNK).transpose(0, 1, 3, 2, 4)
    g_row = jnp.concatenate([g_row, g_row], axis=-1)

    nh = GDN_HEADS
    assert N_HEADS % nh == 0
    wide = nh * HEAD_DIM
    col = lambda sec: pl.BlockSpec((None, t, wide), lambda b_, h: (b_, 0, sec * (N_HEADS // nh) + h))
    gc_spec = pl.BlockSpec((None, nh, t, 4), lambda b_, h: (b_, h, 0, 0))
    gr_spec = pl.BlockSpec((None, nh, n_chunks, 8, 2 * CHUNK), lambda b_, h: (b_, h, 0, 0, 0))
    conv_spec = lambda sec: pl.BlockSpec((3, wide), lambda b_, h: (0, sec * (N_HEADS // nh) + h))
    hp_spec = pl.BlockSpec((nh, 8, HEAD_DIM), lambda b_, h: (h, 0, 0))
    state_spec = pl.BlockSpec((None, nh, HEAD_DIM, HEAD_DIM), lambda b_, h: (b_, h, 0, 0))
    seq = lambda: pltpu.VMEM((nh, t, HEAD_DIM), F32)
    mats = lambda dt: pltpu.VMEM((nh, 2, n_chunks, HEAD_DIM, HEAD_DIM), dt)
    dc = pltpu.VMEM((nh, 2, n_chunks, 8, HEAD_DIM), F32)
    state = lambda: pltpu.VMEM((nh, HEAD_DIM, HEAD_DIM), F32)
    kern = functools.partial(_gdn_kernel, n_chunks=n_chunks, period=period, prep_rows=prep_rows,
                             group=group, igroup=_scan_group(n_chunks, GDN_INTRA_GROUP, False),
                             heads=nh, with_out=with_out)
    if with_out:
        return pl.pallas_call(
            kern,
            grid=(b, N_HEADS // nh),
            in_specs=[col(sec0), col(sec0 + 1), col(sec0 + 2), col(sec0 + 3),
                      gc_spec, gr_spec, conv_spec(0), conv_spec(1), conv_spec(2), hp_spec,
                      pl.BlockSpec((1, HEAD_DIM), lambda b_, h: (0, 0)), state_spec, state_spec],
            out_specs=pl.BlockSpec((None, t, wide), lambda b_, h: (b_, 0, h)),
            out_shape=jax.ShapeDtypeStruct((b, t, N_HEADS * HEAD_DIM), BF16),
            scratch_shapes=[seq(), seq(), seq(), mats(BF16), mats(F32), dc,
                            pltpu.VMEM((nh, 2, t, HEAD_DIM), BF16), seq(), state(), state()],
            compiler_params=_cparams("parallel", "parallel"),
            name="gdn_x",
        )(p, p, p, p, g_col, g_row, conv_w, conv_w, conv_w, hp, nw, s0f, s0b)
    shp = jax.ShapeDtypeStruct((b, N_HEADS, HEAD_DIM, HEAD_DIM), F32)
    return pl.pallas_call(
        kern,
        grid=(b, N_HEADS // nh),
        in_specs=[col(sec0 + 1), col(sec0 + 2), gc_spec, gr_spec, conv_spec(1), conv_spec(2), hp_spec],
        out_specs=[state_spec, state_spec],
        out_shape=[shp, shp],
        scratch_shapes=[seq(), seq(), mats(BF16), mats(F32), dc],
        compiler_params=_cparams("parallel", "parallel"),
        name="gdn_ctx",
    )(p, p, g_col, g_row, conv_w, conv_w, hp)


def _outproj_kernel(mh_ref, mg_ref, x_ref, gt_ref, sc_ref, sh_ref, nw_ref, woh_ref, wog_ref,
                    wrh_ref, wrl_ref, br_ref, x1_ref, h2_ref, idx_ref, gate_ref, cnt_ref):
    y = _dot(mh_ref[...], woh_ref[...]) + _dot(mg_ref[...], wog_ref[...])
    x1 = x_ref[...] + gt_ref[...] * y
    x1_ref[...] = x1
    h = _modulated_norm(x1, nw_ref[...], sc_ref[...], sh_ref[...])
    hh = h.astype(BF16)
    half = hh.shape[1] // 2
    hi_bits = lax.bitcast_convert_type(hh[:, :half].astype(F32), jnp.uint32)
    lo_bits = lax.bitcast_convert_type(hh[:, half:].astype(F32), jnp.uint32)
    h2_ref[...] = hi_bits | (lo_bits >> 16)
    hl = (h - hh.astype(F32)).astype(BF16)
    wrh = wrh_ref[...]
    logits = _dot(hh, wrh) + _dot(hl, wrh) + _dot(hh, wrl_ref[...]) + br_ref[...]
    lane = lax.broadcasted_iota(jnp.int32, logits.shape, 1).astype(F32)
    vals, idxs = [], []
    for _ in range(TOP_K):
        m = jnp.max(logits, axis=-1, keepdims=True)
        i = jnp.min(jnp.where(logits == m, lane, float(LANES)), axis=-1, keepdims=True)
        vals.append(m)
        idxs.append(i)
        logits = jnp.where(lane == i, -jnp.inf, logits)
    es = [jnp.exp(v - vals[0]) for v in vals]
    inv = 1.0 / functools.reduce(lambda a, b_: a + b_, es)

    @pl.when(pl.program_id(0) == 0)
    def _():
        cnt_ref[...] = jnp.zeros_like(cnt_ref)

    onehots = [lane == i for i in idxs]
    picked = functools.reduce(lambda a, b_: a | b_, onehots)
    tm = lane.shape[0]
    before = (lax.broadcasted_iota(jnp.int32, (tm, tm), 1) < lax.broadcasted_iota(jnp.int32, (tm, tm), 0))
    prior = cnt_ref[...] + _dot(before.astype(BF16), picked.astype(BF16))
    ranks = [jnp.sum(jnp.where(oh, prior, 0.0), axis=-1, keepdims=True) for oh in onehots]
    cnt_ref[...] += jnp.sum(picked.astype(F32), axis=0, keepdims=True)

    idx_out = jnp.zeros(lane.shape, F32)
    gate_out = jnp.zeros(lane.shape, F32)
    for k in range(TOP_K):
        idx_out = jnp.where(lane == k, idxs[k], idx_out)
        idx_out = jnp.where(lane == TOP_K + k, ranks[k], idx_out)
        gate_out = jnp.where(lane == k, es[k] * inv, gate_out)
    idx_ref[...] = idx_out.astype(jnp.int32)
    gate_ref[...] = gate_out


def _outproj(mix_h, mix_g, x, gt1, sc2, sh2, nw, w_out2, wr_hi, wr_lo, br):
    b, t, d = x.shape
    w = mix_h.shape[-1]
    tm = min(t, 512)
    tpb = t // tm
    row = lambda width: pl.BlockSpec((None, tm, width), lambda i: (i // tpb, i % tpb, 0))
    per_b = pl.BlockSpec((None, 1, d), lambda i: (i // tpb, 0, 0))
    const = lambda r, c: pl.BlockSpec((r, c), lambda i: (0, 0))
    return pl.pallas_call(
        _outproj_kernel,
        grid=(b * tpb,),
        in_specs=[row(w), row(w), row(d), per_b, per_b, per_b, const(1, d),
                  pl.BlockSpec((None, w, d), lambda i: (0, 0, 0)), pl.BlockSpec((None, w, d), lambda i: (1, 0, 0)),
                  const(d, LANES), const(d, LANES), const(1, LANES)],
        out_specs=[row(d), row(d // 2), row(LANES), row(LANES), const(1, LANES)],
        out_shape=[jax.ShapeDtypeStruct((b, t, d), F32), jax.ShapeDtypeStruct((b, t, d // 2), jnp.uint32),
                   jax.ShapeDtypeStruct((b, t, LANES), jnp.int32), jax.ShapeDtypeStruct((b, t, LANES), F32),
                   jax.ShapeDtypeStruct((1, LANES), F32)],
        compiler_params=_cparams("arbitrary"),
        name="outproj",
    )(mix_h, mix_g, x, gt1, sc2, sh2, nw.reshape(1, d), w_out2, w_out2, wr_hi, wr_lo, br)


def _dispatch_kernel(dest_ref, zrow_ref, h_ref, xs_ref, zbuf, sem, zsem, *, tt, k, tm, n_exp):
    @pl.when(pl.program_id(0) == 0)
    def _():
        zbuf[...] = jnp.zeros_like(zbuf)
        zero_block = lambda e: pltpu.make_async_copy(
            zbuf, xs_ref.at[pl.ds(pl.multiple_of(zrow_ref[e], tm), tm), :], zsem)
        for e in range(n_exp):
            @pl.when(zrow_ref[e] >= 0)
            def _():
                zero_block(e).start()
        for e in range(n_exp):
            @pl.when(zrow_ref[e] >= 0)
            def _():
                zero_block(e).wait()

        def unused_block(b):
            return pltpu.make_async_copy(zbuf, xs_ref.at[pl.ds(pl.multiple_of(b * tm, tm), tm), :], zsem)

        def start_unused(b, carry):
            unused_block(b).start()
            return carry

        def wait_unused(b, carry):
            unused_block(b).wait()
            return carry

        lax.fori_loop(zrow_ref[n_exp], xs_ref.shape[0] // tm, start_unused, 0)
        lax.fori_loop(zrow_ref[n_exp], xs_ref.shape[0] // tm, wait_unused, 0)

    base = pl.program_id(0) * (tt * k)

    def body(tb, carry):
        t0 = pl.multiple_of(tb * 8, 8)
        for u in range(8):
            for kk in range(k):
                row = dest_ref[base + (t0 + u) * k + kk]
                pltpu.make_async_copy(h_ref.at[pl.ds(t0 + u, 1), :], xs_ref.at[pl.ds(row, 1), :], sem).start()
        return carry

    lax.fori_loop(0, tt // 8, body, 0)
    done = xs_ref.at[pl.ds(0, tt * k), :]
    pltpu.make_async_copy(done, done, sem).wait()


def _dispatch(h2, dest, zrow, rows, k, tm):
    n_tok, w = h2.shape
    tt = min(n_tok, 512)
    n_exp = zrow.shape[0] - 1
    return pl.pallas_call(
        functools.partial(_dispatch_kernel, tt=tt, k=k, tm=tm, n_exp=n_exp),
        grid_spec=pltpu.PrefetchScalarGridSpec(
            num_scalar_prefetch=2,
            grid=(n_tok // tt,),
            in_specs=[pl.BlockSpec((tt, w), lambda i, dest_, zrow_: (i, 0))],
            out_specs=pl.BlockSpec(memory_space=pl.ANY),
            scratch_shapes=[pltpu.VMEM((tm, w), h2.dtype), pltpu.SemaphoreType.DMA(()),
                            pltpu.SemaphoreType.DMA(())]),
        out_shape=jax.ShapeDtypeStruct((rows, w), h2.dtype),
        compiler_params=_cparams("arbitrary"),
        name="dispatch",
    )(dest, zrow, h2)


def _cast_tile(src, dst):
    rows = 256

    def body(i, carry):
        sl = pl.ds(pl.multiple_of(i * rows, rows), rows)
        dst[sl, :] = src[sl, :].astype(dst.dtype)
        return carry

    lax.fori_loop(0, src.shape[0] // rows, body, 0)


S_E, S_J, S_R, S_RO, S_JO, S_FIRST, S_VALID, S_NE, S_NJ, S_SLOT = range(10)


def _weight_group_prefetch(sc_ref, copies, on_ready):
    s = pl.program_id(0)

    @pl.when(s == 0)
    def _():
        for c in copies(sc_ref[S_E, 0], sc_ref[S_J, 0], 0):
            c.start()

    @pl.when(sc_ref[S_FIRST, s] == 1)
    def _():
        slot = sc_ref[S_SLOT, s]

        @pl.when(sc_ref[S_NE, s] >= 0)
        def _():
            for c in copies(sc_ref[S_NE, s], sc_ref[S_NJ, s], 1 - slot):
                c.start()

        for c in copies(sc_ref[S_E, s], sc_ref[S_J, s], slot):
            c.wait()
        on_ready(slot)


def _moe_up_kernel(sc_ref, x_ref, bg_ref, bu_ref, wg_hbm, wu_hbm, act_ref, stage, wg_b, wu_b, sems, *, tf):
    s = pl.program_id(0)

    def copies(e, j, slot):
        cols = pl.ds(pl.multiple_of(j * tf, tf), tf)
        return [pltpu.make_async_copy(wg_hbm.at[e, :, cols], stage.at[slot, 0], sems.at[slot, 0]),
                pltpu.make_async_copy(wu_hbm.at[e, :, cols], stage.at[slot, 1], sems.at[slot, 1])]

    def on_ready(slot):
        _cast_tile(stage.at[slot, 0], wg_b)
        _cast_tile(stage.at[slot, 1], wu_b)

    _weight_group_prefetch(sc_ref, copies, on_ready)

    @pl.when(sc_ref[S_VALID, s] == 1)
    def _():
        xp = x_ref[...]
        half = xp.shape[1]
        xa = lax.bitcast_convert_type(xp & jnp.uint32(0xFFFF0000), F32).astype(BF16)
        xb = lax.bitcast_convert_type(xp << 16, F32).astype(BF16)
        gate = _dot(xa, wg_b[:half, :]) + _dot(xb, wg_b[half:, :]) + bg_ref[...]
        up = _dot(xa, wu_b[:half, :]) + _dot(xb, wu_b[half:, :]) + bu_ref[...]
        gate = jnp.minimum(gate, SWIGLU_LIMIT)
        up = jnp.clip(up, -SWIGLU_LIMIT, SWIGLU_LIMIT)
        act_ref[...] = ((up + 1.0) * gate * _sigmoid(SWIGLU_ALPHA * gate)).astype(act_ref.dtype)

    @pl.when(sc_ref[S_VALID, s] == 0)
    def _():
        act_ref[...] = jnp.zeros_like(act_ref)


def _moe_down_kernel(sc_ref, a_ref, bd_ref, wd_hbm, y_ref, stage, wd_b, sems, *, tf):
    s = pl.program_id(0)

    def copies(e, j, slot):
        del j
        return [pltpu.make_async_copy(wd_hbm.at[e], stage.at[slot], sems.at[slot])]

    _weight_group_prefetch(sc_ref, copies, lambda slot: _cast_tile(stage.at[slot], wd_b))

    @pl.when(sc_ref[S_VALID, s] == 1)
    def _():
        cols = pl.ds(pl.multiple_of(sc_ref[S_J, s] * tf, tf), tf)
        y_ref[...] = _dot(a_ref[...], wd_b[:, cols]) + bd_ref[...]

    @pl.when(sc_ref[S_VALID, s] == 0)
    def _():
        y_ref[...] = jnp.zeros_like(y_ref)


def _moe_experts(xs, sched_up, sched_down, wg, wu, wd, bg, bu, bd, tm, tf):
    rows = xs.shape[0]
    n_exp, d, f = wg.shape
    assert xs.shape[1] * 2 == d and d // tf == f // tf
    steps = sched_up.shape[1]
    blk = lambda shape, at: pl.BlockSpec(shape, lambda s, sc: at(sc, s))
    hbm = pl.BlockSpec(memory_space=pl.ANY)
    act = pl.pallas_call(
        functools.partial(_moe_up_kernel, tf=tf),
        grid_spec=pltpu.PrefetchScalarGridSpec(
            num_scalar_prefetch=1,
            grid=(steps,),
            in_specs=[blk((tm, d // 2), lambda sc, s: (sc[S_R, s], 0)),
                      blk((None, 1, tf), lambda sc, s: (sc[S_E, s], 0, sc[S_J, s])),
                      blk((None, 1, tf), lambda sc, s: (sc[S_E, s], 0, sc[S_J, s])),
                      hbm, hbm],
            out_specs=blk((tm, tf), lambda sc, s: (sc[S_RO, s], sc[S_JO, s])),
            scratch_shapes=[pltpu.VMEM((2, 2, d, tf), F32), pltpu.VMEM((d, tf), BF16), pltpu.VMEM((d, tf), BF16),
                            pltpu.SemaphoreType.DMA((2, 2))]),
        out_shape=jax.ShapeDtypeStruct((rows, f), BF16),
        compiler_params=_cparams("arbitrary"),
        name="moe_up",
    )(sched_up, xs, bg.reshape(n_exp, 1, f), bu.reshape(n_exp, 1, f), wg, wu)
    return pl.pallas_call(
        functools.partial(_moe_down_kernel, tf=tf),
        grid_spec=pltpu.PrefetchScalarGridSpec(
            num_scalar_prefetch=1,
            grid=(steps,),
            in_specs=[blk((tm, f), lambda sc, s: (sc[S_R, s], 0)),
                      blk((None, 1, tf), lambda sc, s: (sc[S_E, s], 0, sc[S_J, s])),
                      hbm],
            out_specs=blk((tm, tf), lambda sc, s: (sc[S_RO, s], sc[S_JO, s])),
            scratch_shapes=[pltpu.VMEM((2, f, d), F32), pltpu.VMEM((f, d), BF16), pltpu.SemaphoreType.DMA((2,))]),
        out_shape=jax.ShapeDtypeStruct((rows, d), F32),
        compiler_params=_cparams("arbitrary"),
        name="moe_down",
    )(sched_down, act, bd.reshape(n_exp, 1, d), wd)


def _route(top_idx, rank, counts, tm, n_tiles):
    n_tok, k = top_idx.shape
    n_exp = counts.shape[0]
    m = n_tok * k
    padded = (counts + tm - 1) // tm * tm
    pend = jnp.cumsum(padded)
    onehot = top_idx.reshape(m, 1) == jnp.arange(n_exp, dtype=jnp.int32)[None, :]
    dest = jnp.sum(jnp.where(onehot, (pend - padded)[None, :], 0), axis=1) + rank.reshape(m)
    n_blocks = -(-m // tm) + n_exp

    nb = padded // tm
    blk0 = (pend - padded) // tm
    cum = jnp.cumsum(nb * n_tiles)
    total = cum[-1]
    s = jnp.arange(n_tiles * n_blocks, dtype=jnp.int32)
    valid = s < total
    sc = jnp.minimum(s, total - 1)
    e = jnp.minimum(jnp.sum((cum[None, :] <= sc[:, None]).astype(jnp.int32), axis=1), n_exp - 1)
    of_e = e[:, None] == jnp.arange(n_exp, dtype=jnp.int32)[None, :]
    pick = lambda v: jnp.sum(jnp.where(of_e, v[None, :], 0), axis=1)
    nb_e = pick(nb)
    local = sc - (pick(cum) - nb_e * n_tiles)
    nbe = jnp.maximum(nb_e, 1)
    blk0_e = pick(blk0)
    extra = s - total
    ids = jnp.arange(n_exp, dtype=jnp.int32)
    later = jnp.where((ids[None, :] > ids[:, None]) & (nb[None, :] > 0), ids[None, :], n_exp)
    next_e = jnp.min(later, axis=1)
    next_e = pick(jnp.where(next_e == n_exp, -1, next_e))
    erank = pick(jnp.cumsum((nb > 0).astype(jnp.int32)) - 1)

    def rows_of(j, r, first, ne, nj, group):
        r_out = jnp.where(valid, r, total // n_tiles + extra // n_tiles)
        j_out = jnp.where(valid, j, extra % n_tiles)
        fields = (e, j, r, r_out, j_out, valid & first, valid, ne, nj, group % 2)
        return jnp.stack([a.astype(jnp.int32) for a in fields])

    j_up = local // nbe
    last_tile = j_up == n_tiles - 1
    sched_up = rows_of(j_up, blk0_e + local % nbe, local % nbe == 0,
                       jnp.where(last_tile, next_e, e), jnp.where(last_tile, 0, j_up + 1),
                       erank * n_tiles + j_up)
    sched_down = rows_of(local % n_tiles, blk0_e + local // n_tiles, local == 0, next_e, jnp.zeros_like(e), erank)
    zrow = jnp.concatenate([jnp.where(padded > 0, pend - tm, -1), pend[-1:] // tm]).astype(jnp.int32)
    return dest.astype(jnp.int32), zrow, n_blocks * tm, sched_up, sched_down


def _combine_kernel(dest_ref, x1_ref, gate_ref, gt_ref, nw_ref, ys_ref, o_ref, ybuf, sems, *, tt, k):
    i = pl.program_id(0)
    n = pl.num_programs(0)

    def start_gather(tile, slot):
        base = tile * (tt * k)

        def body(tb, carry):
            for u in range(8):
                for j in range(k):
                    row = dest_ref[base + (tb * 8 + u) * k + j]
                    pltpu.make_async_copy(ys_ref.at[row >> 3, pl.ds(row & 7, 1), :],
                                          ybuf.at[slot, j * (tt // 8) + tb, pl.ds(u, 1), :], sems.at[slot]).start()
            return carry

        lax.fori_loop(0, tt // 8, body, 0)

    @pl.when(i == 0)
    def _():
        start_gather(0, 0)

    for nxt in (0, 1):
        @pl.when((i + 1 < n) & ((i + 1) % 2 == nxt))
        def _():
            start_gather(i + 1, nxt)

    slot = i % 2
    pltpu.make_async_copy(ybuf.at[slot], ybuf.at[slot], sems.at[slot]).wait()
    g = gate_ref[...]
    d = o_ref.shape[-1]
    picked = lambda j: ybuf[slot, pl.ds(j * (tt // 8), tt // 8), :, :].reshape(tt, d)
    moe = picked(0) * g[:, 0:1]
    for j in range(1, k):
        moe = moe + picked(j) * g[:, j:j + 1]
    x = x1_ref[...] + gt_ref[...] * moe
    ms = jnp.mean(x * x, axis=-1, keepdims=True)
    o_ref[...] = x * lax.rsqrt(ms + EPS) * nw_ref[...]


def _combine(x1, ys, dest, gates, gt2, nw, k):
    b, t, d = x1.shape
    tt = min(t, 256)
    tpb = t // tt
    row = lambda width: pl.BlockSpec((None, tt, width), lambda i, dest_: (i // tpb, i % tpb, 0))
    return pl.pallas_call(
        functools.partial(_combine_kernel, tt=tt, k=k),
        grid_spec=pltpu.PrefetchScalarGridSpec(
            num_scalar_prefetch=1,
            grid=(b * tpb,),
            in_specs=[row(d), row(LANES),
                      pl.BlockSpec((None, 1, d), lambda i, dest_: (i // tpb, 0, 0)),
                      pl.BlockSpec((1, d), lambda i, dest_: (0, 0)),
                      pl.BlockSpec(memory_space=pl.ANY)],
            out_specs=row(d),
            scratch_shapes=[pltpu.VMEM((2, k * tt // 8, 8, d), F32), pltpu.SemaphoreType.DMA((2,))]),
        out_shape=jax.ShapeDtypeStruct((b, t, d), F32),
        compiler_params=_cparams("arbitrary"),
        name="combine",
    )(dest, x1, gates, gt2, nw.reshape(1, d), ys.reshape(ys.shape[0] // 8, 8, d))


def kernel(x, c, ctx, c_ctx, w_ada, b_ada, norm_mix_w, w_in, hg_lb_f, hg_lb_b, hg_norm_w, gd_conv_w,
           gd_a_log_f, gd_a_log_b, gd_dt_bias_f, gd_dt_bias_b, gd_norm_w, w_out, norm_ffn_w, w_router,
           b_router, w_gate, b_gate, w_up, b_up, w_down, b_down, norm_out_w):
    bsz, t, d = x.shape
    t_ctx = ctx.shape[1]
    n_exp = w_router.shape[-1]
    l = 0
    hg_w = N_HEADS * HEAD_DIM
    n_main = 9 * hg_w

    lb_f = jnp.cumsum(jax.nn.softmax(hg_lb_f.astype(F32), axis=0), axis=0)[l].reshape(1, hg_w)
    lb_b = jnp.cumsum(jax.nn.softmax(hg_lb_b.astype(F32), axis=0), axis=0)[l].reshape(1, hg_w)

    rows = -(-(bsz + 1) // 8) * 8
    cc = jnp.zeros((rows, d), F32).at[:bsz].set(c).at[bsz].set(c_ctx)
    mod = _ada(cc, w_ada[l], b_ada[l])
    sh1, sc1, gt1, sh2, sc2, gt2 = (mod[:bsz, i * d:(i + 1) * d].reshape(bsz, 1, d) for i in range(6))
    csh1 = jnp.broadcast_to(mod[bsz, 0:d].reshape(1, 1, d), (bsz, 1, d))
    csc1 = jnp.broadcast_to(mod[bsz, d:2 * d].reshape(1, 1, d), (bsz, 1, d))

    w_in_b = w_in[l].astype(BF16)
    w_gates = jnp.pad(w_in_b[:, n_main:], ((0, 0), (0, LANES - (w_in_b.shape[1] - n_main))))
    px, gx = _inproj(x, norm_mix_w[l], sc1, sh1, w_in_b, n_main, w_gates)
    pc, gc = _inproj(ctx, norm_mix_w[l], csc1, csh1, w_in_b, n_main, w_gates)

    hg_nw = hg_norm_w[l].reshape(1, HEAD_DIM)
    hs_f, hs_b = _hgrn_states(pc, lb_f, lb_b)
    mix_h = _hgrn_out(px, lb_f, lb_b, hg_nw, hs_f, hs_b)

    hp = jnp.stack([gd_a_log_f[l], gd_dt_bias_f[l], gd_a_log_b[l], gd_dt_bias_b[l]], axis=1)
    hp = jnp.broadcast_to(jnp.pad(hp, ((0, 0), (0, 4)))[:, :, None], (N_HEADS, 8, HEAD_DIM)).astype(F32)
    gd_nw = gd_norm_w[l].reshape(1, HEAD_DIM)
    gs_f, gs_b = _gdn_call(pc, gc, gd_conv_w[l], hp, gd_nw, None, None, t_ctx, False, 5)
    mix_g = _gdn_call(px, gx, gd_conv_w[l], hp, gd_nw, gs_f, gs_b, GRID_W, True, 5)

    w_out_b = w_out[l].astype(BF16)
    wr = jnp.pad(w_router[l], ((0, 0), (0, LANES - n_exp)))
    wr_hi = wr.astype(BF16)
    wr_lo = (wr - wr_hi.astype(F32)).astype(BF16)
    br = jnp.full((1, LANES), NEG_BIG, F32).at[0, :n_exp].set(b_router[l])
    x1, h2, idx_pad, gate_pad, cnt = _outproj(mix_h, mix_g, x, gt1, sc2, sh2, norm_ffn_w[l],
                                              w_out_b.reshape(2, hg_w, d), wr_hi, wr_lo, br)

    n_tok = bsz * t
    idx_pad = idx_pad.reshape(n_tok, LANES)
    dest, zrow, n_rows, sched_up, sched_down = _route(idx_pad[:, :TOP_K], idx_pad[:, TOP_K:2 * TOP_K],
                                                      cnt[0, :n_exp].astype(jnp.int32), MOE_TM,
                                                      w_gate.shape[-1] // MOE_TF)
    xs = _dispatch(h2.reshape(n_tok, d // 2), dest, zrow, n_rows, TOP_K, MOE_TM)
    ys = _moe_experts(xs, sched_up, sched_down, w_gate[l], w_up[l], w_down[l], b_gate[l], b_up[l], b_down[l],
                      MOE_TM, MOE_TF)

    return _combine(x1, ys, dest, gate_pad, gt2, norm_out_w, TOP_K)
```

```python
import functools

import jax
import jax.numpy as jnp
from jax import lax
from jax.experimental import pallas as pl
from jax.experimental.pallas import tpu as pltpu

F32 = jnp.float32
BF16 = jnp.bfloat16

EPS = 1e-6
CHUNK = 64
HEAD_DIM = 128
N_HEADS = 8
GRID_W = 64
TOP_K = 4
SWIGLU_LIMIT = 7.0
SWIGLU_ALPHA = 1.702
LANES = 128
NEG_BIG = -1e30

HGRN_GROUP = 8
GDN_GROUP = 4
GDN_INTRA_GROUP = 16
GDN_HEADS = 2
GDN_CTX_HEADS = 4
MOE_TM = 512
MOE_TF = 1024
VMEM_LIMIT = 56 * 1024 * 1024


def _cparams(*sem):
    return pltpu.CompilerParams(dimension_semantics=sem, vmem_limit_bytes=VMEM_LIMIT)


def _dot(a, b):
    return jnp.dot(a, b, preferred_element_type=F32)


def _dot_nt(a, b):
    return lax.dot_general(a, b, (((1,), (1,)), ((), ())), preferred_element_type=F32)


def _split2(x):
    hi = x.astype(BF16)
    return hi, (x - hi.astype(F32)).astype(BF16)


def _split3(x):
    hi = x.astype(BF16)
    rest = x - hi.astype(F32)
    mid = rest.astype(BF16)
    return hi, mid, (rest - mid.astype(F32)).astype(BF16)


def _dot_split(a, b):
    ah, al = a
    bh, bl = b
    return _dot(ah, bh) + (_dot(ah, bl) + _dot(al, bh))


def _sigmoid(x):
    return 1.0 / (1.0 + jnp.exp(-x))


def _ada_kernel(c_ref, w_ref, b_ref, o_ref):
    c = c_ref[...]
    s = (c * _sigmoid(c)).astype(BF16)
    o_ref[...] = _dot(s, w_ref[...].astype(BF16)) + b_ref[...]


def _ada(cc, w_ada, b_ada):
    rows, d = cc.shape
    n = w_ada.shape[1]
    tn = 1024
    return pl.pallas_call(
        _ada_kernel,
        grid=(n // tn,),
        in_specs=[pl.BlockSpec((rows, d), lambda j: (0, 0)),
                  pl.BlockSpec((d, tn), lambda j: (0, j)),
                  pl.BlockSpec((1, tn), lambda j: (0, j))],
        out_specs=pl.BlockSpec((rows, tn), lambda j: (0, j)),
        out_shape=jax.ShapeDtypeStruct((rows, n), F32),
        compiler_params=_cparams("arbitrary"),
        name="ada",
    )(cc, w_ada, b_ada.reshape(1, n))


def _modulated_norm(x, nw, sc, sh):
    ms = jnp.mean(x * x, axis=-1, keepdims=True)
    return (x * lax.rsqrt(ms + EPS) * nw) * (1.0 + sc) + sh


def _inproj_kernel(x_ref, nw_ref, sc_ref, sh_ref, w_ref, wg_ref, o_ref, og_ref, h_scr):
    @pl.when(pl.program_id(1) == 0)
    def _():
        h = _modulated_norm(x_ref[...], nw_ref[...], sc_ref[...], sh_ref[...]).astype(BF16)
        h_scr[...] = h
        og_ref[...] = _dot(h, wg_ref[...])

    o_ref[...] = _dot(h_scr[...], w_ref[...])


def _inproj(x, nw, sc, sh, w_all, n, w_gates):
    b, t, d = x.shape
    w_main = w_all
    tm = min(t, 1024)
    tn = 1024
    tpb = t // tm
    assert n % tn == 0
    return pl.pallas_call(
        _inproj_kernel,
        grid=(b * tpb, n // tn),
        in_specs=[pl.BlockSpec((None, tm, d), lambda i, j: (i // tpb, i % tpb, 0)),
                  pl.BlockSpec((1, d), lambda i, j: (0, 0)),
                  pl.BlockSpec((None, 1, d), lambda i, j: (i // tpb, 0, 0)),
                  pl.BlockSpec((None, 1, d), lambda i, j: (i // tpb, 0, 0)),
                  pl.BlockSpec((d, tn), lambda i, j: (0, j)),
                  pl.BlockSpec((d, LANES), lambda i, j: (0, 0))],
        out_specs=[pl.BlockSpec((None, tm, tn), lambda i, j: (i // tpb, i % tpb, j)),
                   pl.BlockSpec((None, tm, LANES), lambda i, j: (i // tpb, i % tpb, 0))],
        out_shape=[jax.ShapeDtypeStruct((b, t, n), F32),
                   jax.ShapeDtypeStruct((b, t, LANES), F32)],
        scratch_shapes=[pltpu.VMEM((tm, d), BF16)],
        compiler_params=_cparams("parallel", "arbitrary"),
        name="inproj",
    )(x, nw.reshape(1, d), sc, sh, w_main, w_gates)


def _chunk_masks(rev, wide=False):
    shape = (CHUNK, 2 * CHUNK if wide else CHUNK)
    r = lax.broadcasted_iota(jnp.int32, shape, 0)
    c = lax.broadcasted_iota(jnp.int32, shape, 1) & (CHUNK - 1)
    incl = (c >= r) if rev else (c <= r)
    strict = (c > r) if rev else (c < r)
    return incl, strict


def _gated_head_norm(o, z, nw):
    ms = jnp.mean(o * o, axis=-1, keepdims=True)
    return (o * lax.rsqrt(ms + EPS) * nw) * (z * _sigmoid(z))


def _group_slices(g, group, n_chunks):
    idxs, revs = [], []
    for i in range(group):
        n = g * group + i
        idxs += [n, n_chunks - 1 - n]
        revs += [False, True]
    return [pl.ds(pl.multiple_of(ix * CHUNK, CHUNK), CHUNK) for ix in idxs], idxs, revs


def _scan_group(n_chunks, want, two_phase):
    g = want
    while g > 1 and (n_chunks % g or (two_phase and (n_chunks // g) % 2)):
        g //= 2
    assert n_chunks % g == 0 and not (two_phase and (n_chunks // g) % 2)
    return g


def _gla_group(zfs, vs, qs, lbs, revs):
    n = len(zfs)
    incl = [_chunk_masks(r)[0] for r in revs]
    logf, k = [], []
    for i in range(n):
        one_m = 1.0 - lbs[i]
        logf.append(jnp.log(lbs[i] + one_m * _sigmoid(zfs[i])))
        k.append(one_m * _sigmoid(-zfs[i]))
    cum = []
    for i in range(n):
        tri = incl[i].astype(BF16)
        hi, mid, lo = _split3(logf[i])
        parts = _dot(tri, jnp.concatenate([hi, mid, lo], axis=1))
        cum.append(parts[:, :HEAD_DIM] + (parts[:, HEAD_DIM:2 * HEAD_DIM] + parts[:, 2 * HEAD_DIM:]))
    last = [cum[i][0:1] if revs[i] else cum[i][CHUNK - 1:CHUNK] for i in range(n)]
    u_t = [_dot(vs[i].T.astype(BF16), (k[i] * jnp.exp(last[i] - cum[i])).astype(BF16)) for i in range(n)]
    dec = [jnp.exp(x) for x in last]
    if qs is None:
        return u_t, dec, None, None
    scores = []
    for i in range(n):
        mid_row = CHUNK // 2 - 1 if revs[i] else CHUNK // 2
        ref = cum[i][mid_row:mid_row + 1]
        sc = _dot_nt((qs[i] * jnp.exp(cum[i] - ref)).astype(BF16), (k[i] * jnp.exp(ref - cum[i])).astype(BF16))
        scores.append(jnp.where(incl[i], sc, 0.0).astype(BF16))
    o_intra = [_dot(scores[i], vs[i].astype(BF16)) for i in range(n)]
    qe = [(qs[i] * jnp.exp(cum[i])).astype(BF16) for i in range(n)]
    return u_t, dec, o_intra, qe


def _hgrn_out_kernel(q_ref, ff_ref, fb_ref, v_ref, g_ref, lbf_ref, lbb_ref, nw_ref, s0f_ref, s0b_ref,
                     o_ref, o_scr, sf_scr, sb_scr, *, n_chunks, group):
    sf_scr[...] = s0f_ref[...]
    sb_scr[...] = s0b_ref[...]
    lbf, lbb, nw = lbf_ref[...], lbb_ref[...], nw_ref[...]

    def body(g, carry, final):
        sls, _, revs = _group_slices(g, group, n_chunks)
        zfs = [(fb_ref if r else ff_ref)[sl, :] for sl, r in zip(sls, revs)]
        vs = [v_ref[sl, :] for sl in sls]
        qs = [q_ref[sl, :] for sl in sls]
        u_t, dec, o_intra, qe = _gla_group(zfs, vs, qs, [lbb if r else lbf for r in revs], revs)
        for i, (sl, rev) in enumerate(zip(sls, revs)):
            s_scr = sb_scr if rev else sf_scr
            st = s_scr[...]
            o = o_intra[i] + _dot_nt(qe[i], st.astype(BF16))
            s_scr[...] = st * dec[i] + u_t[i]
            if final:
                o_ref[sl, :] = _gated_head_norm(o_scr[sl, :] + o, g_ref[sl, :], nw).astype(o_ref.dtype)
            else:
                o_scr[sl, :] = o
        return carry

    trips = n_chunks // group
    lax.fori_loop(0, trips // 2, functools.partial(body, final=False), 0)
    lax.fori_loop(trips // 2, trips, functools.partial(body, final=True), 0)


def _hgrn_state_kernel(ff_ref, fb_ref, v_ref, lbf_ref, lbb_ref, sf_ref, sb_ref, *, n_chunks, group):
    sf_ref[...] = jnp.zeros_like(sf_ref)
    sb_ref[...] = jnp.zeros_like(sb_ref)
    lbf, lbb = lbf_ref[...], lbb_ref[...]

    def body(g, carry):
        sls, _, revs = _group_slices(g, group, n_chunks)
        zfs = [(fb_ref if r else ff_ref)[sl, :] for sl, r in zip(sls, revs)]
        vs = [v_ref[sl, :] for sl in sls]
        u_t, dec, _, _ = _gla_group(zfs, vs, None, [lbb if r else lbf for r in revs], revs)
        for i, rev in enumerate(revs):
            s_ref = sb_ref if rev else sf_ref
            s_ref[...] = s_ref[...] * dec[i] + u_t[i]
        return carry

    lax.fori_loop(0, n_chunks // group, body, 0)


def _col_spec(t, section):
    return pl.BlockSpec((None, t, HEAD_DIM), lambda b, h: (b, 0, section * N_HEADS + h))


def _head_row_spec(offset=0):
    return pl.BlockSpec((1, HEAD_DIM), lambda b, h: (0, offset + h))


_STATE_SPEC = pl.BlockSpec((None, None, HEAD_DIM, HEAD_DIM), lambda b, h: (b, h, 0, 0))


def _hgrn_states(pc, lbf, lbb):
    b, t, _ = pc.shape
    n_chunks = t // CHUNK
    shp = jax.ShapeDtypeStruct((b, N_HEADS, HEAD_DIM, HEAD_DIM), F32)
    return pl.pallas_call(
        functools.partial(_hgrn_state_kernel, n_chunks=n_chunks, group=_scan_group(n_chunks, HGRN_GROUP, False)),
        grid=(b, N_HEADS),
        in_specs=[_col_spec(t, 1), _col_spec(t, 2), _col_spec(t, 3), _head_row_spec(), _head_row_spec()],
        out_specs=[_STATE_SPEC, _STATE_SPEC],
        out_shape=[shp, shp],
        compiler_params=_cparams("parallel", "parallel"),
        name="hgrn_ctx",
    )(pc, pc, pc, lbf, lbb)


def _hgrn_out(px, lbf, lbb, nw, s0f, s0b):
    b, t, _ = px.shape
    n_chunks = t // CHUNK
    return pl.pallas_call(
        functools.partial(_hgrn_out_kernel, n_chunks=n_chunks, group=_scan_group(n_chunks, HGRN_GROUP, True)),
        grid=(b, N_HEADS),
        in_specs=[_col_spec(t, 0), _col_spec(t, 1), _col_spec(t, 2), _col_spec(t, 3), _col_spec(t, 4),
                  _head_row_spec(), _head_row_spec(),
                  pl.BlockSpec((1, HEAD_DIM), lambda b_, h: (0, 0)),
                  _STATE_SPEC, _STATE_SPEC],
        out_specs=pl.BlockSpec((None, t, HEAD_DIM), lambda b_, h: (b_, 0, h)),
        out_shape=jax.ShapeDtypeStruct((b, t, N_HEADS * HEAD_DIM), BF16),
        scratch_shapes=[pltpu.VMEM((t, HEAD_DIM), F32),
                        pltpu.VMEM((HEAD_DIM, HEAD_DIM), F32),
                        pltpu.VMEM((HEAD_DIM, HEAD_DIM), F32)],
        compiler_params=_cparams("parallel", "parallel"),
        name="hgrn_x",
    )(px, px, px, px, px, lbf, lbb, nw, s0f, s0b)


def _conv_silu(a, w, period):
    rows = a.shape[0]
    pos = lax.broadcasted_iota(jnp.int32, a.shape, 0) % period
    prev = jnp.where(pos == 0, 0.0, pltpu.roll(a, 1, axis=0))
    nxt = jnp.where(pos == period - 1, 0.0, pltpu.roll(a, rows - 1, axis=0))
    y = prev * w[0:1] + a * w[1:2] + nxt * w[2:3]
    return y * _sigmoid(y)


def _l2norm(x):
    return x * lax.rsqrt(jnp.sum(x * x, axis=-1, keepdims=True) + EPS)


def _softplus(x):
    return jnp.maximum(x, 0.0) + jnp.log1p(jnp.exp(-jnp.abs(x)))


def _unit_tri_inverses(tris):
    row = lax.broadcasted_iota(jnp.int32, (CHUNK, 2 * CHUNK), 0)
    lane = lax.broadcasted_iota(jnp.int32, (CHUNK, 2 * CHUNK), 1)
    left = lane < CHUNK
    eye = jnp.where((lane & (CHUNK - 1)) == row, 1.0, 0.0)
    zeros = jnp.zeros((CHUNK, 2 * CHUNK), BF16)

    def lhs(hl):
        return jnp.where(left, hl[0], hl[1])

    def rhs(hl):
        top = jnp.concatenate([hl[0], hl[1]], axis=1)
        return jnp.concatenate([top, jnp.concatenate([hl[0], zeros], axis=1)], axis=0)

    def fold(o):
        return o[:, :2 * CHUNK] + o[:, 2 * CHUNK:]

    ps = [_split2(-a) for a in tris]
    invs = [eye - a for a in tris]
    ps = [_split2(fold(_dot(lhs(p), rhs(p)))) for p in ps]
    for _ in range(CHUNK.bit_length() - 3):
        outs = [fold(_dot(jnp.concatenate([lhs(p), lhs(_split2(inv))], axis=0), rhs(p)))
                for p, inv in zip(ps, invs)]
        invs = [inv + o[CHUNK:] for inv, o in zip(invs, outs)]
        ps = [_split2(o[:CHUNK]) for o in outs]
    invs = [inv + fold(_dot(lhs(_split2(inv)), rhs(p))) for inv, p in zip(invs, ps)]
    return [inv[:, :CHUNK] for inv in invs]


def _gdn_group(qs, ks, vs, a_cs, b_cs, a_rs, alogs, dtbs, revs):
    n = len(ks)
    left = lax.broadcasted_iota(jnp.int32, (CHUNK, 2 * CHUNK), 1) < CHUNK
    dmask, ecum, beta, eend, dec, strict = [], [], [], [], [], []
    for i in range(n):
        incl, st = _chunk_masks(revs[i], wide=True)
        incl_t = _chunk_masks(not revs[i], wide=True)[0]
        scale = -jnp.exp(alogs[i])
        g_c = scale * _softplus(a_cs[i] + dtbs[i])
        g_r = scale * _softplus(a_rs[i] + dtbs[i])
        cum_c = jnp.sum(jnp.where(incl & left, g_r, 0.0), axis=1, keepdims=True)
        cum_r = jnp.sum(jnp.where(incl_t, g_c, 0.0), axis=0, keepdims=True)
        dmask.append(jnp.exp(jnp.where(incl, cum_c - cum_r, -jnp.inf)))
        ecum.append(jnp.exp(cum_c))
        beta.append(_sigmoid(b_cs[i]))
        last = cum_c[0:1] if revs[i] else cum_c[CHUNK - 1:CHUNK]
        eend.append(jnp.exp(last - cum_c))
        dec.append(jnp.exp(last))
        strict.append(st)
    kbf = [k.astype(BF16) for k in ks]
    kb = [ks[i] * beta[i] for i in range(n)]
    kk = [_dot_nt(kb[i].astype(BF16), jnp.concatenate([kbf[i], kbf[i]], axis=0)) for i in range(n)]
    tris = [jnp.where(strict[i], kk[i] * dmask[i], 0.0) for i in range(n)]
    tinv = [t.astype(BF16) for t in _unit_tri_inverses(tris)]
    dmask = [m[:, :CHUNK] for m in dmask]
    u = [_dot(tinv[i], (vs[i] * beta[i]).astype(BF16)).astype(BF16) for i in range(n)]
    w = [_dot(tinv[i], (kb[i] * ecum[i]).astype(BF16)).astype(BF16) for i in range(n)]
    ke_t = [(ks[i] * eend[i]).T.astype(BF16) for i in range(n)]
    mp = [_dot(ke_t[i], w[i]) for i in range(n)]
    cc = [_dot(ke_t[i], u[i]) for i in range(n)]
    if qs is None:
        return mp, cc, dec, None, None
    attn = [(_dot_nt(qs[i].astype(BF16), kbf[i]) * dmask[i]).astype(BF16) for i in range(n)]
    qp = [qs[i] * ecum[i] - _dot(attn[i], w[i]) for i in range(n)]
    oi = [_dot(attn[i], u[i]) for i in range(n)]
    return mp, cc, dec, qp, oi


def _gdn_kernel(*refs, n_chunks, period, prep_rows, group, igroup, heads, with_out):
    if with_out:
        (q_ref, k_ref, v_ref, z_ref, gc_ref, gr_ref, wq_ref, wk_ref, wv_ref, hp_ref, nw_ref,
         s0f_ref, s0b_ref, o_ref, qn, kn, vn, mp_s, cc_s, dc_s, qp_s, o_scr, sf, sb) = refs
    else:
        (k_ref, v_ref, gc_ref, gr_ref, wk_ref, wv_ref, hp_ref,
         sf, sb, kn, vn, mp_s, cc_s, dc_s) = refs
        q_ref = None

    t = n_chunks * CHUNK
    lanes = lambda hs: slice(hs * HEAD_DIM, (hs + 1) * HEAD_DIM)

    def prep(i, carry):
        sl = pl.ds(pl.multiple_of(i * prep_rows, prep_rows), prep_rows)
        for hs in range(heads):
            ln = lanes(hs)
            kn[hs, sl, :] = _l2norm(_conv_silu(k_ref[sl, ln], wk_ref[:, ln], period))
            vn[hs, sl, :] = _conv_silu(v_ref[sl, ln], wv_ref[:, ln], period)
            if with_out:
                qn[hs, sl, :] = _l2norm(_conv_silu(q_ref[sl, ln], wq_ref[:, ln], period)) * (HEAD_DIM ** -0.5)
        return carry

    lax.fori_loop(0, t // prep_rows, prep, 0)

    trips_i = n_chunks // igroup

    def intra(it, carry):
        hs = it // trips_i
        g = it % trips_i
        hp = hp_ref[hs]
        qs, ks, vs, a_cs, b_cs, a_rs, alogs, dtbs, revs, where = [], [], [], [], [], [], [], [], [], []
        for i in range(igroup):
            n = g * igroup + i
            sl = pl.ds(pl.multiple_of(n * CHUNK, CHUNK), CHUNK)
            gc = gc_ref[hs, sl, :]
            gr = gr_ref[hs, n]
            for d, rev in enumerate((False, True)):
                if with_out:
                    qs.append(qn[hs, sl, :])
                ks.append(kn[hs, sl, :])
                vs.append(vn[hs, sl, :])
                a_cs.append(gc[:, d:d + 1])
                b_cs.append(gc[:, 2 + d:3 + d])
                a_rs.append(gr[d:d + 1, :])
                alogs.append(hp[2 * d:2 * d + 1, 0:1])
                dtbs.append(hp[2 * d + 1:2 * d + 2, 0:1])
                revs.append(rev)
                where.append((d, n, sl))
        mp, cc, dec, qp, oi = _gdn_group(qs if with_out else None, ks, vs, a_cs, b_cs, a_rs, alogs, dtbs, revs)
        for i, (d, n, sl) in enumerate(where):
            mp_s[hs, d, n] = mp[i].astype(BF16)
            cc_s[hs, d, n] = cc[i]
            dc_s[hs, d, n] = jnp.broadcast_to(dec[i], (8, HEAD_DIM))
            if with_out:
                qp_s[hs, d, sl, :] = qp[i].astype(BF16)
                if d == 1:
                    o_scr[hs, sl, :] = oi[i - 1] + oi[i]
        return carry

    lax.fori_loop(0, heads * trips_i, intra, 0)

    if with_out:
        sf[...] = s0f_ref[...]
        sb[...] = s0b_ref[...]
        nw = nw_ref[...]
    else:
        sf[...] = jnp.zeros_like(sf)
        sb[...] = jnp.zeros_like(sb)

    def scan(g, carry, final):
        sls, idxs, revs = _group_slices(g, group, n_chunks)
        for sl, idx, rev in zip(sls, idxs, revs):
            d = int(rev)
            s_ref = sb if rev else sf
            for hs in range(heads):
                s = s_ref[hs]
                sbf = s.astype(BF16)
                s_ref[hs] = s * dc_s[hs, d, idx][0:1, :] - _dot(mp_s[hs, d, idx], sbf) + cc_s[hs, d, idx]
                if with_out:
                    o = o_scr[hs, sl, :] + _dot(qp_s[hs, d, sl, :], sbf)
                    if final:
                        o_ref[sl, lanes(hs)] = _gated_head_norm(o, z_ref[sl, lanes(hs)], nw).astype(o_ref.dtype)
                    else:
                        o_scr[hs, sl, :] = o
        return carry

    trips = n_chunks // group
    if with_out:
        lax.fori_loop(0, trips // 2, functools.partial(scan, final=False), 0)
        lax.fori_loop(trips // 2, trips, functools.partial(scan, final=True), 0)
    else:
        lax.fori_loop(0, trips, functools.partial(scan, final=False), 0)


def _gdn_call(p, gates, conv_w, hp, nw, s0f, s0b, period, with_out, sec0):
    b, t, _ = p.shape
    n_chunks = t // CHUNK
    prep_rows = max(period, min(t, 256))
    assert prep_rows % period == 0 and t % prep_rows == 0
    group = _scan_group(n_chunks, GDN_GROUP, with_out)
    g4 = gates[:, :, :4 * N_HEADS].reshape(b, t, 4, N_HEADS)
    g_col = g4.transpose(0, 3, 1, 2)
    g_row = jnp.pad(g4.transpose(0, 3, 2, 1), ((0, 0), (0, 0), (0, 4), (0, 0)))
    g_row = g_row.reshape(b, N_HEADS, 8, n_chunks, CHUNK).transpose(0, 1, 3, 2, 4)
    g_row = jnp.concatenate([g_row, g_row], axis=-1)

    nh = GDN_HEADS if with_out else GDN_CTX_HEADS
    assert N_HEADS % nh == 0
    wide = nh * HEAD_DIM
    col = lambda sec: pl.BlockSpec((None, t, wide), lambda b_, h: (b_, 0, sec * (N_HEADS // nh) + h))
    gc_spec = pl.BlockSpec((None, nh, t, 4), lambda b_, h: (b_, h, 0, 0))
    gr_spec = pl.BlockSpec((None, nh, n_chunks, 8, 2 * CHUNK), lambda b_, h: (b_, h, 0, 0, 0))
    conv_spec = lambda sec: pl.BlockSpec((3, wide), lambda b_, h: (0, sec * (N_HEADS // nh) + h))
    hp_spec = pl.BlockSpec((nh, 8, HEAD_DIM), lambda b_, h: (h, 0, 0))
    state_spec = pl.BlockSpec((None, nh, HEAD_DIM, HEAD_DIM), lambda b_, h: (b_, h, 0, 0))
    seq = lambda: pltpu.VMEM((nh, t, HEAD_DIM), F32)
    mats = lambda dt: pltpu.VMEM((nh, 2, n_chunks, HEAD_DIM, HEAD_DIM), dt)
    dc = pltpu.VMEM((nh, 2, n_chunks, 8, HEAD_DIM), F32)
    state = lambda: pltpu.VMEM((nh, HEAD_DIM, HEAD_DIM), F32)
    kern = functools.partial(_gdn_kernel, n_chunks=n_chunks, period=period, prep_rows=prep_rows,
                             group=group, igroup=_scan_group(n_chunks, GDN_INTRA_GROUP, False),
                             heads=nh, with_out=with_out)
    if with_out:
        return pl.pallas_call(
            kern,
            grid=(b, N_HEADS // nh),
            in_specs=[col(sec0), col(sec0 + 1), col(sec0 + 2), col(sec0 + 3),
                      gc_spec, gr_spec, conv_spec(0), conv_spec(1), conv_spec(2), hp_spec,
                      pl.BlockSpec((1, HEAD_DIM), lambda b_, h: (0, 0)), state_spec, state_spec],
            out_specs=pl.BlockSpec((None, t, wide), lambda b_, h: (b_, 0, h)),
            out_shape=jax.ShapeDtypeStruct((b, t, N_HEADS * HEAD_DIM), BF16),
            scratch_shapes=[seq(), seq(), seq(), mats(BF16), mats(F32), dc,
                            pltpu.VMEM((nh, 2, t, HEAD_DIM), BF16), seq(), state(), state()],
            compiler_params=_cparams("parallel", "parallel"),
            name="gdn_x",
        )(p, p, p, p, g_col, g_row, conv_w, conv_w, conv_w, hp, nw, s0f, s0b)
    shp = jax.ShapeDtypeStruct((b, N_HEADS, HEAD_DIM, HEAD_DIM), F32)
    return pl.pallas_call(
        kern,
        grid=(b, N_HEADS // nh),
        in_specs=[col(sec0 + 1), col(sec0 + 2), gc_spec, gr_spec, conv_spec(1), conv_spec(2), hp_spec],
        out_specs=[state_spec, state_spec],
        out_shape=[shp, shp],
        scratch_shapes=[seq(), seq(), mats(BF16), mats(F32), dc],
        compiler_params=_cparams("parallel", "parallel"),
        name="gdn_ctx",
    )(p, p, g_col, g_row, conv_w, conv_w, hp)


def _outproj_kernel(mh_ref, mg_ref, x_ref, gt_ref, sc_ref, sh_ref, nw_ref, woh_ref, wog_ref,
                    wrh_ref, wrl_ref, br_ref, x1_ref, h2_ref, idx_ref, gate_ref, cnt_ref):
    y = _dot(mh_ref[...], woh_ref[...]) + _dot(mg_ref[...], wog_ref[...])
    x1 = x_ref[...] + gt_ref[...] * y
    x1_ref[...] = x1
    h = _modulated_norm(x1, nw_ref[...], sc_ref[...], sh_ref[...])
    hh = h.astype(BF16)
    half = hh.shape[1] // 2
    hi_bits = lax.bitcast_convert_type(hh[:, :half].astype(F32), jnp.uint32)
    lo_bits = lax.bitcast_convert_type(hh[:, half:].astype(F32), jnp.uint32)
    h2_ref[...] = hi_bits | (lo_bits >> 16)
    hl = (h - hh.astype(F32)).astype(BF16)
    wrh = wrh_ref[...]
    logits = _dot(hh, wrh) + _dot(hl, wrh) + _dot(hh, wrl_ref[...]) + br_ref[...]
    lane = lax.broadcasted_iota(jnp.int32, logits.shape, 1).astype(F32)
    vals, idxs = [], []
    for _ in range(TOP_K):
        m = jnp.max(logits, axis=-1, keepdims=True)
        i = jnp.min(jnp.where(logits == m, lane, float(LANES)), axis=-1, keepdims=True)
        vals.append(m)
        idxs.append(i)
        logits = jnp.where(lane == i, -jnp.inf, logits)
    es = [jnp.exp(v - vals[0]) for v in vals]
    inv = 1.0 / functools.reduce(lambda a, b_: a + b_, es)

    @pl.when(pl.program_id(0) == 0)
    def _():
        cnt_ref[...] = jnp.zeros_like(cnt_ref)

    onehots = [lane == i for i in idxs]
    picked = functools.reduce(lambda a, b_: a | b_, onehots)
    tm = lane.shape[0]
    before = (lax.broadcasted_iota(jnp.int32, (tm, tm), 1) < lax.broadcasted_iota(jnp.int32, (tm, tm), 0))
    prior = cnt_ref[...] + _dot(before.astype(BF16), picked.astype(BF16))
    ranks = [jnp.sum(jnp.where(oh, prior, 0.0), axis=-1, keepdims=True) for oh in onehots]
    cnt_ref[...] += jnp.sum(picked.astype(F32), axis=0, keepdims=True)

    idx_out = jnp.zeros(lane.shape, F32)
    gate_out = jnp.zeros(lane.shape, F32)
    for k in range(TOP_K):
        idx_out = jnp.where(lane == k, idxs[k], idx_out)
        idx_out = jnp.where(lane == TOP_K + k, ranks[k], idx_out)
        gate_out = jnp.where(lane == k, es[k] * inv, gate_out)
    idx_ref[...] = idx_out.astype(jnp.int32)
    gate_ref[...] = gate_out


def _outproj(mix_h, mix_g, x, gt1, sc2, sh2, nw, w_out2, wr_hi, wr_lo, br):
    b, t, d = x.shape
    w = mix_h.shape[-1]
    tm = min(t, 512)
    tpb = t // tm
    row = lambda width: pl.BlockSpec((None, tm, width), lambda i: (i // tpb, i % tpb, 0))
    per_b = pl.BlockSpec((None, 1, d), lambda i: (i // tpb, 0, 0))
    const = lambda r, c: pl.BlockSpec((r, c), lambda i: (0, 0))
    return pl.pallas_call(
        _outproj_kernel,
        grid=(b * tpb,),
        in_specs=[row(w), row(w), row(d), per_b, per_b, per_b, const(1, d),
                  pl.BlockSpec((None, w, d), lambda i: (0, 0, 0)), pl.BlockSpec((None, w, d), lambda i: (1, 0, 0)),
                  const(d, LANES), const(d, LANES), const(1, LANES)],
        out_specs=[row(d), row(d // 2), row(LANES), row(LANES), const(1, LANES)],
        out_shape=[jax.ShapeDtypeStruct((b, t, d), F32), jax.ShapeDtypeStruct((b, t, d // 2), jnp.uint32),
                   jax.ShapeDtypeStruct((b, t, LANES), jnp.int32), jax.ShapeDtypeStruct((b, t, LANES), F32),
                   jax.ShapeDtypeStruct((1, LANES), F32)],
        compiler_params=_cparams("arbitrary"),
        name="outproj",
    )(mix_h, mix_g, x, gt1, sc2, sh2, nw.reshape(1, d), w_out2, w_out2, wr_hi, wr_lo, br)


def _dispatch_kernel(dest_ref, zrow_ref, h_ref, xs_ref, zbuf, sem, zsem, *, tt, k, tm, n_exp):
    @pl.when(pl.program_id(0) == 0)
    def _():
        zbuf[...] = jnp.zeros_like(zbuf)
        zero_block = lambda e: pltpu.make_async_copy(
            zbuf, xs_ref.at[pl.ds(pl.multiple_of(zrow_ref[e], tm), tm), :], zsem)
        for e in range(n_exp):
            @pl.when(zrow_ref[e] >= 0)
            def _():
                zero_block(e).start()
        for e in range(n_exp):
            @pl.when(zrow_ref[e] >= 0)
            def _():
                zero_block(e).wait()

        def unused_block(b):
            return pltpu.make_async_copy(zbuf, xs_ref.at[pl.ds(pl.multiple_of(b * tm, tm), tm), :], zsem)

        def start_unused(b, carry):
            unused_block(b).start()
            return carry

        def wait_unused(b, carry):
            unused_block(b).wait()
            return carry

        lax.fori_loop(zrow_ref[n_exp], xs_ref.shape[0] // tm, start_unused, 0)
        lax.fori_loop(zrow_ref[n_exp], xs_ref.shape[0] // tm, wait_unused, 0)

    base = pl.program_id(0) * (tt * k)

    def body(tb, carry):
        t0 = pl.multiple_of(tb * 8, 8)
        for u in range(8):
            for kk in range(k):
                row = dest_ref[base + (t0 + u) * k + kk]
                pltpu.make_async_copy(h_ref.at[pl.ds(t0 + u, 1), :], xs_ref.at[pl.ds(row, 1), :], sem).start()
        return carry

    lax.fori_loop(0, tt // 8, body, 0)
    done = xs_ref.at[pl.ds(0, tt * k), :]
    pltpu.make_async_copy(done, done, sem).wait()


def _dispatch(h2, dest, zrow, rows, k, tm):
    n_tok, w = h2.shape
    tt = min(n_tok, 512)
    n_exp = zrow.shape[0] - 1
    return pl.pallas_call(
        functools.partial(_dispatch_kernel, tt=tt, k=k, tm=tm, n_exp=n_exp),
        grid_spec=pltpu.PrefetchScalarGridSpec(
            num_scalar_prefetch=2,
            grid=(n_tok // tt,),
            in_specs=[pl.BlockSpec((tt, w), lambda i, dest_, zrow_: (i, 0))],
            out_specs=pl.BlockSpec(memory_space=pl.ANY),
            scratch_shapes=[pltpu.VMEM((tm, w), h2.dtype), pltpu.SemaphoreType.DMA(()),
                            pltpu.SemaphoreType.DMA(())]),
        out_shape=jax.ShapeDtypeStruct((rows, w), h2.dtype),
        compiler_params=_cparams("arbitrary"),
        name="dispatch",
    )(dest, zrow, h2)


def _cast_tile(src, dst):
    rows = 256

    def body(i, carry):
        sl = pl.ds(pl.multiple_of(i * rows, rows), rows)
        dst[sl, :] = src[sl, :].astype(dst.dtype)
        return carry

    lax.fori_loop(0, src.shape[0] // rows, body, 0)


S_E, S_J, S_R, S_RO, S_JO, S_FIRST, S_VALID, S_NE, S_NJ, S_SLOT = range(10)


def _weight_group_prefetch(sc_ref, copies, on_ready):
    s = pl.program_id(0)

    @pl.when(s == 0)
    def _():
        for c in copies(sc_ref[S_E, 0], sc_ref[S_J, 0], 0):
            c.start()

    @pl.when(sc_ref[S_FIRST, s] == 1)
    def _():
        slot = sc_ref[S_SLOT, s]

        @pl.when(sc_ref[S_NE, s] >= 0)
        def _():
            for c in copies(sc_ref[S_NE, s], sc_ref[S_NJ, s], 1 - slot):
                c.start()

        for c in copies(sc_ref[S_E, s], sc_ref[S_J, s], slot):
            c.wait()
        on_ready(slot)


def _moe_up_kernel(sc_ref, x_ref, bg_ref, bu_ref, wg_hbm, wu_hbm, act_ref, stage, wg_b, wu_b, sems, *, tf):
    s = pl.program_id(0)

    def copies(e, j, slot):
        cols = pl.ds(pl.multiple_of(j * tf, tf), tf)
        return [pltpu.make_async_copy(wg_hbm.at[e, :, cols], stage.at[slot, 0], sems.at[slot, 0]),
                pltpu.make_async_copy(wu_hbm.at[e, :, cols], stage.at[slot, 1], sems.at[slot, 1])]

    def on_ready(slot):
        _cast_tile(stage.at[slot, 0], wg_b)
        _cast_tile(stage.at[slot, 1], wu_b)

    _weight_group_prefetch(sc_ref, copies, on_ready)

    @pl.when(sc_ref[S_VALID, s] == 1)
    def _():
        xp = x_ref[...]
        half = xp.shape[1]
        xa = lax.bitcast_convert_type(xp & jnp.uint32(0xFFFF0000), F32).astype(BF16)
        xb = lax.bitcast_convert_type(xp << 16, F32).astype(BF16)
        gate = _dot(xa, wg_b[:half, :]) + _dot(xb, wg_b[half:, :]) + bg_ref[...]
        up = _dot(xa, wu_b[:half, :]) + _dot(xb, wu_b[half:, :]) + bu_ref[...]
        gate = jnp.minimum(gate, SWIGLU_LIMIT)
        up = jnp.clip(up, -SWIGLU_LIMIT, SWIGLU_LIMIT)
        act_ref[...] = ((up + 1.0) * gate * _sigmoid(SWIGLU_ALPHA * gate)).astype(act_ref.dtype)

    @pl.when(sc_ref[S_VALID, s] == 0)
    def _():
        act_ref[...] = jnp.zeros_like(act_ref)


def _moe_down_kernel(sc_ref, a_ref, bd_ref, wd_hbm, y_ref, stage, wd_b, sems, *, tf):
    s = pl.program_id(0)

    def copies(e, j, slot):
        del j
        return [pltpu.make_async_copy(wd_hbm.at[e], stage.at[slot], sems.at[slot])]

    _weight_group_prefetch(sc_ref, copies, lambda slot: _cast_tile(stage.at[slot], wd_b))

    @pl.when(sc_ref[S_VALID, s] == 1)
    def _():
        cols = pl.ds(pl.multiple_of(sc_ref[S_J, s] * tf, tf), tf)
        y_ref[...] = _dot(a_ref[...], wd_b[:, cols]) + bd_ref[...]

    @pl.when(sc_ref[S_VALID, s] == 0)
    def _():
        y_ref[...] = jnp.zeros_like(y_ref)


def _moe_experts(xs, sched_up, sched_down, wg, wu, wd, bg, bu, bd, tm, tf):
    rows = xs.shape[0]
    n_exp, d, f = wg.shape
    assert xs.shape[1] * 2 == d and d // tf == f // tf
    steps = sched_up.shape[1]
    blk = lambda shape, at: pl.BlockSpec(shape, lambda s, sc: at(sc, s))
    hbm = pl.BlockSpec(memory_space=pl.ANY)
    act = pl.pallas_call(
        functools.partial(_moe_up_kernel, tf=tf),
        grid_spec=pltpu.PrefetchScalarGridSpec(
            num_scalar_prefetch=1,
            grid=(steps,),
            in_specs=[blk((tm, d // 2), lambda sc, s: (sc[S_R, s], 0)),
                      blk((None, 1, tf), lambda sc, s: (sc[S_E, s], 0, sc[S_J, s])),
                      blk((None, 1, tf), lambda sc, s: (sc[S_E, s], 0, sc[S_J, s])),
                      hbm, hbm],
            out_specs=blk((tm, tf), lambda sc, s: (sc[S_RO, s], sc[S_JO, s])),
            scratch_shapes=[pltpu.VMEM((2, 2, d, tf), F32), pltpu.VMEM((d, tf), BF16), pltpu.VMEM((d, tf), BF16),
                            pltpu.SemaphoreType.DMA((2, 2))]),
        out_shape=jax.ShapeDtypeStruct((rows, f), BF16),
        compiler_params=_cparams("arbitrary"),
        name="moe_up",
    )(sched_up, xs, bg.reshape(n_exp, 1, f), bu.reshape(n_exp, 1, f), wg, wu)
    return pl.pallas_call(
        functools.partial(_moe_down_kernel, tf=tf),
        grid_spec=pltpu.PrefetchScalarGridSpec(
            num_scalar_prefetch=1,
            grid=(steps,),
            in_specs=[blk((tm, f), lambda sc, s: (sc[S_R, s], 0)),
                      blk((None, 1, tf), lambda sc, s: (sc[S_E, s], 0, sc[S_J, s])),
                      hbm],
            out_specs=blk((tm, tf), lambda sc, s: (sc[S_RO, s], sc[S_JO, s])),
            scratch_shapes=[pltpu.VMEM((2, f, d), F32), pltpu.VMEM((f, d), BF16), pltpu.SemaphoreType.DMA((2,))]),
        out_shape=jax.ShapeDtypeStruct((rows, d), F32),
        compiler_params=_cparams("arbitrary"),
        name="moe_down",
    )(sched_down, act, bd.reshape(n_exp, 1, d), wd)


def _route(top_idx, rank, counts, tm, n_tiles):
    n_tok, k = top_idx.shape
    n_exp = counts.shape[0]
    m = n_tok * k
    padded = (counts + tm - 1) // tm * tm
    pend = jnp.cumsum(padded)
    onehot = top_idx.reshape(m, 1) == jnp.arange(n_exp, dtype=jnp.int32)[None, :]
    dest = jnp.sum(jnp.where(onehot, (pend - padded)[None, :], 0), axis=1) + rank.reshape(m)
    n_blocks = -(-m // tm) + n_exp

    nb = padded // tm
    blk0 = (pend - padded) // tm
    cum = jnp.cumsum(nb * n_tiles)
    total = cum[-1]
    s = jnp.arange(n_tiles * n_blocks, dtype=jnp.int32)
    valid = s < total
    sc = jnp.minimum(s, total - 1)
    e = jnp.minimum(jnp.sum((cum[None, :] <= sc[:, None]).astype(jnp.int32), axis=1), n_exp - 1)
    of_e = e[:, None] == jnp.arange(n_exp, dtype=jnp.int32)[None, :]
    pick = lambda v: jnp.sum(jnp.where(of_e, v[None, :], 0), axis=1)
    nb_e = pick(nb)
    local = sc - (pick(cum) - nb_e * n_tiles)
    nbe = jnp.maximum(nb_e, 1)
    blk0_e = pick(blk0)
    extra = s - total
    ids = jnp.arange(n_exp, dtype=jnp.int32)
    later = jnp.where((ids[None, :] > ids[:, None]) & (nb[None, :] > 0), ids[None, :], n_exp)
    next_e = jnp.min(later, axis=1)
    next_e = pick(jnp.where(next_e == n_exp, -1, next_e))
    erank = pick(jnp.cumsum((nb > 0).astype(jnp.int32)) - 1)

    def rows_of(j, r, first, ne, nj, group):
        r_out = jnp.where(valid, r, total // n_tiles + extra // n_tiles)
        j_out = jnp.where(valid, j, extra % n_tiles)
        fields = (e, j, r, r_out, j_out, valid & first, valid, ne, nj, group % 2)
        return jnp.stack([a.astype(jnp.int32) for a in fields])

    j_up = local // nbe
    last_tile = j_up == n_tiles - 1
    sched_up = rows_of(j_up, blk0_e + local % nbe, local % nbe == 0,
                       jnp.where(last_tile, next_e, e), jnp.where(last_tile, 0, j_up + 1),
                       erank * n_tiles + j_up)
    sched_down = rows_of(local % n_tiles, blk0_e + local // n_tiles, local == 0, next_e, jnp.zeros_like(e), erank)
    zrow = jnp.concatenate([jnp.where(padded > 0, pend - tm, -1), pend[-1:] // tm]).astype(jnp.int32)
    return dest.astype(jnp.int32), zrow, n_blocks * tm, sched_up, sched_down


def _combine_kernel(dest_ref, x1_ref, gate_ref, gt_ref, nw_ref, ys_ref, o_ref, ybuf, sems, *, tt, k):
    i = pl.program_id(0)
    n = pl.num_programs(0)

    def start_gather(tile, slot):
        base = tile * (tt * k)

        def body(tb, carry):
            for u in range(8):
                for j in range(k):
                    row = dest_ref[base + (tb * 8 + u) * k + j]
                    pltpu.make_async_copy(ys_ref.at[row >> 3, pl.ds(row & 7, 1), :],
                                          ybuf.at[slot, j * (tt // 8) + tb, pl.ds(u, 1), :], sems.at[slot]).start()
            return carry

        lax.fori_loop(0, tt // 8, body, 0)

    @pl.when(i == 0)
    def _():
        start_gather(0, 0)

    for nxt in (0, 1):
        @pl.when((i + 1 < n) & ((i + 1) % 2 == nxt))
        def _():
            start_gather(i + 1, nxt)

    slot = i % 2
    pltpu.make_async_copy(ybuf.at[slot], ybuf.at[slot], sems.at[slot]).wait()
    g = gate_ref[...]
    d = o_ref.shape[-1]
    picked = lambda j: ybuf[slot, pl.ds(j * (tt // 8), tt // 8), :, :].reshape(tt, d)
    moe = picked(0) * g[:, 0:1]
    for j in range(1, k):
        moe = moe + picked(j) * g[:, j:j + 1]
    x = x1_ref[...] + gt_ref[...] * moe
    ms = jnp.mean(x * x, axis=-1, keepdims=True)
    o_ref[...] = x * lax.rsqrt(ms + EPS) * nw_ref[...]


def _combine(x1, ys, dest, gates, gt2, nw, k):
    b, t, d = x1.shape
    tt = min(t, 256)
    tpb = t // tt
    row = lambda width: pl.BlockSpec((None, tt, width), lambda i, dest_: (i // tpb, i % tpb, 0))
    return pl.pallas_call(
        functools.partial(_combine_kernel, tt=tt, k=k),
        grid_spec=pltpu.PrefetchScalarGridSpec(
            num_scalar_prefetch=1,
            grid=(b * tpb,),
            in_specs=[row(d), row(LANES),
                      pl.BlockSpec((None, 1, d), lambda i, dest_: (i // tpb, 0, 0)),
                      pl.BlockSpec((1, d), lambda i, dest_: (0, 0)),
                      pl.BlockSpec(memory_space=pl.ANY)],
            out_specs=row(d),
            scratch_shapes=[pltpu.VMEM((2, k * tt // 8, 8, d), F32), pltpu.SemaphoreType.DMA((2,))]),
        out_shape=jax.ShapeDtypeStruct((b, t, d), F32),
        compiler_params=_cparams("arbitrary"),
        name="combine",
    )(dest, x1, gates, gt2, nw.reshape(1, d), ys.reshape(ys.shape[0] // 8, 8, d))


def kernel(x, c, ctx, c_ctx, w_ada, b_ada, norm_mix_w, w_in, hg_lb_f, hg_lb_b, hg_norm_w, gd_conv_w,
           gd_a_log_f, gd_a_log_b, gd_dt_bias_f, gd_dt_bias_b, gd_norm_w, w_out, norm_ffn_w, w_router,
           b_router, w_gate, b_gate, w_up, b_up, w_down, b_down, norm_out_w):
    bsz, t, d = x.shape
    t_ctx = ctx.shape[1]
    n_exp = w_router.shape[-1]
    assert w_ada.shape[0] == 1, "single-layer block: the context stream only feeds the scan states"
    l = 0
    hg_w = N_HEADS * HEAD_DIM
    n_main = 9 * hg_w

    lb_f = jnp.cumsum(jax.nn.softmax(hg_lb_f.astype(F32), axis=0), axis=0)[l].reshape(1, hg_w)
    lb_b = jnp.cumsum(jax.nn.softmax(hg_lb_b.astype(F32), axis=0), axis=0)[l].reshape(1, hg_w)

    rows = -(-(bsz + 1) // 8) * 8
    cc = jnp.zeros((rows, d), F32).at[:bsz].set(c).at[bsz].set(c_ctx)
    mod = _ada(cc, w_ada[l], b_ada[l])
    sh1, sc1, gt1, sh2, sc2, gt2 = (mod[:bsz, i * d:(i + 1) * d].reshape(bsz, 1, d) for i in range(6))
    csh1 = mod[bsz, 0:d].reshape(1, 1, d)
    csc1 = mod[bsz, d:2 * d].reshape(1, 1, d)

    w_in_b = w_in[l].astype(BF16)
    w_gates = jnp.pad(w_in_b[:, n_main:], ((0, 0), (0, LANES - (w_in_b.shape[1] - n_main))))
    px, gx = _inproj(x, norm_mix_w[l], sc1, sh1, w_in_b, n_main, w_gates)
    pc, gc = _inproj(ctx.reshape(1, bsz * t_ctx, d), norm_mix_w[l], csc1, csh1, w_in_b, n_main, w_gates)
    pc = pc.reshape(bsz, t_ctx, n_main)
    gc = gc.reshape(bsz, t_ctx, LANES)

    hg_nw = hg_norm_w[l].reshape(1, HEAD_DIM)
    hs_f, hs_b = _hgrn_states(pc, lb_f, lb_b)
    mix_h = _hgrn_out(px, lb_f, lb_b, hg_nw, hs_f, hs_b)

    hp = jnp.stack([gd_a_log_f[l], gd_dt_bias_f[l], gd_a_log_b[l], gd_dt_bias_b[l]], axis=1)
    hp = jnp.broadcast_to(jnp.pad(hp, ((0, 0), (0, 4)))[:, :, None], (N_HEADS, 8, HEAD_DIM)).astype(F32)
    gd_nw = gd_norm_w[l].reshape(1, HEAD_DIM)
    gs_f, gs_b = _gdn_call(pc, gc, gd_conv_w[l], hp, gd_nw, None, None, t_ctx, False, 5)
    mix_g = _gdn_call(px, gx, gd_conv_w[l], hp, gd_nw, gs_f, gs_b, GRID_W, True, 5)

    w_out_b = w_out[l].astype(BF16)
    wr = jnp.pad(w_router[l], ((0, 0), (0, LANES - n_exp)))
    wr_hi = wr.astype(BF16)
    wr_lo = (wr - wr_hi.astype(F32)).astype(BF16)
    br = jnp.full((1, LANES), NEG_BIG, F32).at[0, :n_exp].set(b_router[l])
    x1, h2, idx_pad, gate_pad, cnt = _outproj(mix_h, mix_g, x, gt1, sc2, sh2, norm_ffn_w[l],
                                              w_out_b.reshape(2, hg_w, d), wr_hi, wr_lo, br)

    n_tok = bsz * t
    idx_pad = idx_pad.reshape(n_tok, LANES)
    dest, zrow, n_rows, sched_up, sched_down = _route(idx_pad[:, :TOP_K], idx_pad[:, TOP_K:2 * TOP_K],
                                                      cnt[0, :n_exp].astype(jnp.int32), MOE_TM,
                                                      w_gate.shape[-1] // MOE_TF)
    xs = _dispatch(h2.reshape(n_tok, d // 2), dest, zrow, n_rows, TOP_K, MOE_TM)
    ys = _moe_experts(xs, sched_up, sched_down, w_gate[l], w_up[l], w_down[l], b_gate[l], b_up[l], b_down[l],
                      MOE_TM, MOE_TF)

    return _combine(x1, ys, dest, gate_pad, gt2, norm_out_w, TOP_K)
```

```python
import functools

import jax
import jax.numpy as jnp
from jax import lax
from jax.experimental import pallas as pl
from jax.experimental.pallas import tpu as pltpu

F32 = jnp.float32
BF16 = jnp.bfloat16

EPS = 1e-6
CHUNK = 64
HEAD_DIM = 128
N_HEADS = 8
GRID_W = 64
TOP_K = 4
SWIGLU_LIMIT = 7.0
SWIGLU_ALPHA = 1.702
LANES = 128
NEG_BIG = -1e30

HGRN_GROUP = 8
GDN_GROUP = 4
GDN_INTRA_GROUP = 16
GDN_HEADS = 2
GDN_CTX_HEADS = 4
MOE_TM = 512
MOE_TF = 1024
VMEM_LIMIT = 56 * 1024 * 1024


def _cparams(*sem):
    return pltpu.CompilerParams(dimension_semantics=sem, vmem_limit_bytes=VMEM_LIMIT)


def _dot(a, b):
    return jnp.dot(a, b, preferred_element_type=F32)


def _dot_nt(a, b):
    return lax.dot_general(a, b, (((1,), (1,)), ((), ())), preferred_element_type=F32)


def _split2(x):
    hi = x.astype(BF16)
    return hi, (x - hi.astype(F32)).astype(BF16)


def _split3(x):
    hi = x.astype(BF16)
    rest = x - hi.astype(F32)
    mid = rest.astype(BF16)
    return hi, mid, (rest - mid.astype(F32)).astype(BF16)


def _sigmoid(x):
    return 1.0 / (1.0 + jnp.exp(-x))


def _ada_kernel(c_ref, w_ref, b_ref, o_ref):
    c = c_ref[...]
    s = (c * _sigmoid(c)).astype(BF16)
    o_ref[...] = _dot(s, w_ref[...].astype(BF16)) + b_ref[...]


def _ada(cc, w_ada, b_ada):
    rows, d = cc.shape
    n = w_ada.shape[1]
    tn = 1024
    return pl.pallas_call(
        _ada_kernel,
        grid=(n // tn,),
        in_specs=[pl.BlockSpec((rows, d), lambda j: (0, 0)),
                  pl.BlockSpec((d, tn), lambda j: (0, j)),
                  pl.BlockSpec((1, tn), lambda j: (0, j))],
        out_specs=pl.BlockSpec((rows, tn), lambda j: (0, j)),
        out_shape=jax.ShapeDtypeStruct((rows, n), F32),
        compiler_params=_cparams("arbitrary"),
        name="ada",
    )(cc, w_ada, b_ada.reshape(1, n))


def _modulated_norm(x, nw, sc, sh):
    ms = jnp.mean(x * x, axis=-1, keepdims=True)
    return (x * lax.rsqrt(ms + EPS) * nw) * (1.0 + sc) + sh


def _inproj_kernel(x_ref, nw_ref, sc_ref, sh_ref, w_ref, wg_ref, o_ref, og_ref, h_scr):
    @pl.when(pl.program_id(1) == 0)
    def _():
        h = _modulated_norm(x_ref[...], nw_ref[...], sc_ref[...], sh_ref[...]).astype(BF16)
        h_scr[...] = h
        og_ref[...] = _dot(h, wg_ref[...])

    o_ref[...] = _dot(h_scr[...], w_ref[...])


def _inproj(x, nw, sc, sh, w_all, n, w_gates):
    b, t, d = x.shape
    w_main = w_all
    tm = min(t, 1024)
    tn = 1536 if n % 1536 == 0 else 1024
    tpb = t // tm
    assert n % tn == 0
    return pl.pallas_call(
        _inproj_kernel,
        grid=(b * tpb, n // tn),
        in_specs=[pl.BlockSpec((None, tm, d), lambda i, j: (i // tpb, i % tpb, 0)),
                  pl.BlockSpec((1, d), lambda i, j: (0, 0)),
                  pl.BlockSpec((None, 1, d), lambda i, j: (i // tpb, 0, 0)),
                  pl.BlockSpec((None, 1, d), lambda i, j: (i // tpb, 0, 0)),
                  pl.BlockSpec((d, tn), lambda i, j: (0, j)),
                  pl.BlockSpec((d, LANES), lambda i, j: (0, 0))],
        out_specs=[pl.BlockSpec((None, tm, tn), lambda i, j: (i // tpb, i % tpb, j)),
                   pl.BlockSpec((None, tm, LANES), lambda i, j: (i // tpb, i % tpb, 0))],
        out_shape=[jax.ShapeDtypeStruct((b, t, n), F32),
                   jax.ShapeDtypeStruct((b, t, LANES), F32)],
        scratch_shapes=[pltpu.VMEM((tm, d), BF16)],
        compiler_params=_cparams("parallel", "arbitrary"),
        name="inproj",
    )(x, nw.reshape(1, d), sc, sh, w_main, w_gates)


def _chunk_masks(rev, wide=False):
    shape = (CHUNK, 2 * CHUNK if wide else CHUNK)
    r = lax.broadcasted_iota(jnp.int32, shape, 0)
    c = lax.broadcasted_iota(jnp.int32, shape, 1) & (CHUNK - 1)
    incl = (c >= r) if rev else (c <= r)
    strict = (c > r) if rev else (c < r)
    return incl, strict


def _gated_head_norm(o, z, nw):
    ms = jnp.mean(o * o, axis=-1, keepdims=True)
    return (o * lax.rsqrt(ms + EPS) * nw) * (z * _sigmoid(z))


def _group_slices(g, group, n_chunks):
    idxs, revs = [], []
    for i in range(group):
        n = g * group + i
        idxs += [n, n_chunks - 1 - n]
        revs += [False, True]
    return [pl.ds(pl.multiple_of(ix * CHUNK, CHUNK), CHUNK) for ix in idxs], idxs, revs


def _scan_group(n_chunks, want, two_phase):
    g = want
    while g > 1 and (n_chunks % g or (two_phase and (n_chunks // g) % 2)):
        g //= 2
    assert n_chunks % g == 0 and not (two_phase and (n_chunks // g) % 2)
    return g


def _gla_group(zfs, vs, qs, lbs, revs):
    n = len(zfs)
    incl = [_chunk_masks(r)[0] for r in revs]
    logf, k = [], []
    for i in range(n):
        one_m = 1.0 - lbs[i]
        logf.append(jnp.log(lbs[i] + one_m * _sigmoid(zfs[i])))
        k.append(one_m * _sigmoid(-zfs[i]))
    cum = []
    for i in range(n):
        tri = incl[i].astype(BF16)
        hi, mid, lo = _split3(logf[i])
        parts = _dot(tri, jnp.concatenate([hi, mid, lo], axis=1))
        cum.append(parts[:, :HEAD_DIM] + (parts[:, HEAD_DIM:2 * HEAD_DIM] + parts[:, 2 * HEAD_DIM:]))
    last = [cum[i][0:1] if revs[i] else cum[i][CHUNK - 1:CHUNK] for i in range(n)]
    u_t = [_dot(vs[i].T.astype(BF16), (k[i] * jnp.exp(last[i] - cum[i])).astype(BF16)) for i in range(n)]
    dec = [jnp.exp(x) for x in last]
    if qs is None:
        return u_t, dec, None, None
    scores = []
    for i in range(n):
        mid_row = CHUNK // 2 - 1 if revs[i] else CHUNK // 2
        ref = cum[i][mid_row:mid_row + 1]
        sc = _dot_nt((qs[i] * jnp.exp(cum[i] - ref)).astype(BF16), (k[i] * jnp.exp(ref - cum[i])).astype(BF16))
        scores.append(jnp.where(incl[i], sc, 0.0).astype(BF16))
    o_intra = [_dot(scores[i], vs[i].astype(BF16)) for i in range(n)]
    qe = [(qs[i] * jnp.exp(cum[i])).astype(BF16) for i in range(n)]
    return u_t, dec, o_intra, qe


def _hgrn_out_kernel(q_ref, ff_ref, fb_ref, v_ref, g_ref, lbf_ref, lbb_ref, nw_ref, s0f_ref, s0b_ref,
                     o_ref, o_scr, sf_scr, sb_scr, *, n_chunks, group):
    sf_scr[...] = s0f_ref[...]
    sb_scr[...] = s0b_ref[...]
    lbf, lbb, nw = lbf_ref[...], lbb_ref[...], nw_ref[...]

    def body(g, carry, final):
        sls, _, revs = _group_slices(g, group, n_chunks)
        zfs = [(fb_ref if r else ff_ref)[sl, :] for sl, r in zip(sls, revs)]
        vs = [v_ref[sl, :] for sl in sls]
        qs = [q_ref[sl, :] for sl in sls]
        u_t, dec, o_intra, qe = _gla_group(zfs, vs, qs, [lbb if r else lbf for r in revs], revs)
        for i, (sl, rev) in enumerate(zip(sls, revs)):
            s_scr = sb_scr if rev else sf_scr
            st = s_scr[...]
            o = o_intra[i] + _dot_nt(qe[i], st.astype(BF16))
            s_scr[...] = st * dec[i] + u_t[i]
            if final:
                o_ref[sl, :] = _gated_head_norm(o_scr[sl, :] + o, g_ref[sl, :], nw).astype(o_ref.dtype)
            else:
                o_scr[sl, :] = o
        return carry

    trips = n_chunks // group
    lax.fori_loop(0, trips // 2, functools.partial(body, final=False), 0)
    lax.fori_loop(trips // 2, trips, functools.partial(body, final=True), 0)


def _hgrn_state_kernel(ff_ref, fb_ref, v_ref, lbf_ref, lbb_ref, sf_ref, sb_ref, *, n_chunks, group):
    sf_ref[...] = jnp.zeros_like(sf_ref)
    sb_ref[...] = jnp.zeros_like(sb_ref)
    lbf, lbb = lbf_ref[...], lbb_ref[...]

    def body(g, carry):
        sls, _, revs = _group_slices(g, group, n_chunks)
        zfs = [(fb_ref if r else ff_ref)[sl, :] for sl, r in zip(sls, revs)]
        vs = [v_ref[sl, :] for sl in sls]
        u_t, dec, _, _ = _gla_group(zfs, vs, None, [lbb if r else lbf for r in revs], revs)
        for i, rev in enumerate(revs):
            s_ref = sb_ref if rev else sf_ref
            s_ref[...] = s_ref[...] * dec[i] + u_t[i]
        return carry

    lax.fori_loop(0, n_chunks // group, body, 0)


def _col_spec(t, section):
    return pl.BlockSpec((None, t, HEAD_DIM), lambda b, h: (b, 0, section * N_HEADS + h))


def _head_row_spec(offset=0):
    return pl.BlockSpec((1, HEAD_DIM), lambda b, h: (0, offset + h))


_STATE_SPEC = pl.BlockSpec((None, None, HEAD_DIM, HEAD_DIM), lambda b, h: (b, h, 0, 0))


def _hgrn_states(pc, lbf, lbb):
    b, t, _ = pc.shape
    n_chunks = t // CHUNK
    shp = jax.ShapeDtypeStruct((b, N_HEADS, HEAD_DIM, HEAD_DIM), F32)
    return pl.pallas_call(
        functools.partial(_hgrn_state_kernel, n_chunks=n_chunks, group=_scan_group(n_chunks, HGRN_GROUP, False)),
        grid=(b, N_HEADS),
        in_specs=[_col_spec(t, 1), _col_spec(t, 2), _col_spec(t, 3), _head_row_spec(), _head_row_spec()],
        out_specs=[_STATE_SPEC, _STATE_SPEC],
        out_shape=[shp, shp],
        compiler_params=_cparams("parallel", "parallel"),
        name="hgrn_ctx",
    )(pc, pc, pc, lbf, lbb)


def _hgrn_out(px, lbf, lbb, nw, s0f, s0b):
    b, t, _ = px.shape
    n_chunks = t // CHUNK
    return pl.pallas_call(
        functools.partial(_hgrn_out_kernel, n_chunks=n_chunks, group=_scan_group(n_chunks, HGRN_GROUP, True)),
        grid=(b, N_HEADS),
        in_specs=[_col_spec(t, 0), _col_spec(t, 1), _col_spec(t, 2), _col_spec(t, 3), _col_spec(t, 4),
                  _head_row_spec(), _head_row_spec(),
                  pl.BlockSpec((1, HEAD_DIM), lambda b_, h: (0, 0)),
                  _STATE_SPEC, _STATE_SPEC],
        out_specs=pl.BlockSpec((None, t, HEAD_DIM), lambda b_, h: (b_, 0, h)),
        out_shape=jax.ShapeDtypeStruct((b, t, N_HEADS * HEAD_DIM), BF16),
        scratch_shapes=[pltpu.VMEM((t, HEAD_DIM), F32),
                        pltpu.VMEM((HEAD_DIM, HEAD_DIM), F32),
                        pltpu.VMEM((HEAD_DIM, HEAD_DIM), F32)],
        compiler_params=_cparams("parallel", "parallel"),
        name="hgrn_x",
    )(px, px, px, px, px, lbf, lbb, nw, s0f, s0b)


def _conv_silu(a, w, period):
    rows = a.shape[0]
    pos = lax.broadcasted_iota(jnp.int32, a.shape, 0) % period
    prev = jnp.where(pos == 0, 0.0, pltpu.roll(a, 1, axis=0))
    nxt = jnp.where(pos == period - 1, 0.0, pltpu.roll(a, rows - 1, axis=0))
    y = prev * w[0:1] + a * w[1:2] + nxt * w[2:3]
    return y * _sigmoid(y)


def _l2norm(x):
    return x * lax.rsqrt(jnp.sum(x * x, axis=-1, keepdims=True) + EPS)


def _softplus(x):
    return jnp.maximum(x, 0.0) + jnp.log1p(jnp.exp(-jnp.abs(x)))


def _unit_tri_inverses(tris):
    row = lax.broadcasted_iota(jnp.int32, (CHUNK, 2 * CHUNK), 0)
    lane = lax.broadcasted_iota(jnp.int32, (CHUNK, 2 * CHUNK), 1)
    left = lane < CHUNK
    eye = jnp.where((lane & (CHUNK - 1)) == row, 1.0, 0.0)
    zeros = jnp.zeros((CHUNK, 2 * CHUNK), BF16)

    def lhs(hl):
        return jnp.where(left, hl[0], hl[1])

    def rhs(hl):
        top = jnp.concatenate([hl[0], hl[1]], axis=1)
        return jnp.concatenate([top, jnp.concatenate([hl[0], zeros], axis=1)], axis=0)

    def fold(o):
        return o[:, :2 * CHUNK] + o[:, 2 * CHUNK:]

    ps = [_split2(-a) for a in tris]
    invs = [eye - a for a in tris]
    ps = [_split2(fold(_dot(lhs(p), rhs(p)))) for p in ps]
    for _ in range(CHUNK.bit_length() - 3):
        outs = [fold(_dot(jnp.concatenate([lhs(p), lhs(_split2(inv))], axis=0), rhs(p)))
                for p, inv in zip(ps, invs)]
        invs = [inv + o[CHUNK:] for inv, o in zip(invs, outs)]
        ps = [_split2(o[:CHUNK]) for o in outs]
    invs = [inv + fold(_dot(lhs(_split2(inv)), rhs(p))) for inv, p in zip(invs, ps)]
    return [inv[:, :CHUNK] for inv in invs]


def _gdn_group(qs, ks, vs, a_cs, b_cs, a_rs, alogs, dtbs, revs):
    n = len(ks)
    left = lax.broadcasted_iota(jnp.int32, (CHUNK, 2 * CHUNK), 1) < CHUNK
    dmask, ecum, beta, eend, dec, strict = [], [], [], [], [], []
    for i in range(n):
        incl, st = _chunk_masks(revs[i], wide=True)
        incl_t = _chunk_masks(not revs[i], wide=True)[0]
        scale = -jnp.exp(alogs[i])
        g_c = scale * _softplus(a_cs[i] + dtbs[i])
        g_r = scale * _softplus(a_rs[i] + dtbs[i])
        cum_c = jnp.sum(jnp.where(incl & left, g_r, 0.0), axis=1, keepdims=True)
        cum_r = jnp.sum(jnp.where(incl_t, g_c, 0.0), axis=0, keepdims=True)
        dmask.append(jnp.exp(jnp.where(incl, cum_c - cum_r, -jnp.inf)))
        ecum.append(jnp.exp(cum_c))
        beta.append(_sigmoid(b_cs[i]))
        last = cum_c[0:1] if revs[i] else cum_c[CHUNK - 1:CHUNK]
        eend.append(jnp.exp(last - cum_c))
        dec.append(jnp.exp(last))
        strict.append(st)
    kbf = [k.astype(BF16) for k in ks]
    kb = [ks[i] * beta[i] for i in range(n)]
    kk = [_dot_nt(kb[i].astype(BF16), jnp.concatenate([kbf[i], kbf[i]], axis=0)) for i in range(n)]
    tris = [jnp.where(strict[i], kk[i] * dmask[i], 0.0) for i in range(n)]
    tinv = [t.astype(BF16) for t in _unit_tri_inverses(tris)]
    dmask = [m[:, :CHUNK] for m in dmask]
    u = [_dot(tinv[i], (vs[i] * beta[i]).astype(BF16)).astype(BF16) for i in range(n)]
    w = [_dot(tinv[i], (kb[i] * ecum[i]).astype(BF16)).astype(BF16) for i in range(n)]
    ke_t = [(ks[i] * eend[i]).T.astype(BF16) for i in range(n)]
    mp = [_dot(ke_t[i], w[i]) for i in range(n)]
    cc = [_dot(ke_t[i], u[i]) for i in range(n)]
    if qs is None:
        return mp, cc, dec, None, None
    attn = [(_dot_nt(qs[i].astype(BF16), kbf[i]) * dmask[i]).astype(BF16) for i in range(n)]
    qp = [qs[i] * ecum[i] - _dot(attn[i], w[i]) for i in range(n)]
    oi = [_dot(attn[i], u[i]) for i in range(n)]
    return mp, cc, dec, qp, oi


def _gdn_kernel(*refs, n_chunks, period, prep_rows, group, igroup, heads, with_out):
    if with_out:
        (q_ref, k_ref, v_ref, z_ref, gc_ref, gr_ref, wq_ref, wk_ref, wv_ref, hp_ref, nw_ref,
         s0f_ref, s0b_ref, o_ref, qn, kn, vn, mp_s, cc_s, dc_s, qp_s, o_scr, sf, sb) = refs
    else:
        (k_ref, v_ref, gc_ref, gr_ref, wk_ref, wv_ref, hp_ref,
         sf, sb, kn, vn, mp_s, cc_s, dc_s) = refs
        q_ref = None

    t = n_chunks * CHUNK
    lanes = lambda hs: slice(hs * HEAD_DIM, (hs + 1) * HEAD_DIM)

    def prep(i, carry):
        sl = pl.ds(pl.multiple_of(i * prep_rows, prep_rows), prep_rows)
        for hs in range(heads):
            ln = lanes(hs)
            kn[hs, sl, :] = _l2norm(_conv_silu(k_ref[sl, ln], wk_ref[:, ln], period))
            vn[hs, sl, :] = _conv_silu(v_ref[sl, ln], wv_ref[:, ln], period)
            if with_out:
                qn[hs, sl, :] = _l2norm(_conv_silu(q_ref[sl, ln], wq_ref[:, ln], period)) * (HEAD_DIM ** -0.5)
        return carry

    lax.fori_loop(0, t // prep_rows, prep, 0)

    trips_i = n_chunks // igroup

    def intra(it, carry):
        hs = it // trips_i
        g = it % trips_i
        hp = hp_ref[hs]
        qs, ks, vs, a_cs, b_cs, a_rs, alogs, dtbs, revs, where = [], [], [], [], [], [], [], [], [], []
        for i in range(igroup):
            n = g * igroup + i
            sl = pl.ds(pl.multiple_of(n * CHUNK, CHUNK), CHUNK)
            gc = gc_ref[hs, sl, :]
            gr = gr_ref[hs, n]
            for d, rev in enumerate((False, True)):
                if with_out:
                    qs.append(qn[hs, sl, :])
                ks.append(kn[hs, sl, :])
                vs.append(vn[hs, sl, :])
                a_cs.append(gc[:, d:d + 1])
                b_cs.append(gc[:, 2 + d:3 + d])
                a_rs.append(gr[d:d + 1, :])
                alogs.append(hp[2 * d:2 * d + 1, 0:1])
                dtbs.append(hp[2 * d + 1:2 * d + 2, 0:1])
                revs.append(rev)
                where.append((d, n, sl))
        mp, cc, dec, qp, oi = _gdn_group(qs if with_out else None, ks, vs, a_cs, b_cs, a_rs, alogs, dtbs, revs)
        for i, (d, n, sl) in enumerate(where):
            mp_s[hs, d, n] = mp[i].astype(BF16)
            cc_s[hs, d, n] = cc[i]
            dc_s[hs, d, n] = jnp.broadcast_to(dec[i], (8, HEAD_DIM))
            if with_out:
                qp_s[hs, d, sl, :] = qp[i].astype(BF16)
                if d == 1:
                    o_scr[hs, sl, :] = oi[i - 1] + oi[i]
        return carry

    lax.fori_loop(0, heads * trips_i, intra, 0)

    if with_out:
        sf[...] = s0f_ref[...]
        sb[...] = s0b_ref[...]
        nw = nw_ref[...]
    else:
        sf[...] = jnp.zeros_like(sf)
        sb[...] = jnp.zeros_like(sb)

    def scan(g, carry, final):
        sls, idxs, revs = _group_slices(g, group, n_chunks)
        for sl, idx, rev in zip(sls, idxs, revs):
            d = int(rev)
            s_ref = sb if rev else sf
            for hs in range(heads):
                s = s_ref[hs]
                sbf = s.astype(BF16)
                s_ref[hs] = s * dc_s[hs, d, idx][0:1, :] - _dot(mp_s[hs, d, idx], sbf) + cc_s[hs, d, idx]
                if with_out:
                    o = o_scr[hs, sl, :] + _dot(qp_s[hs, d, sl, :], sbf)
                    if final:
                        o_ref[sl, lanes(hs)] = _gated_head_norm(o, z_ref[sl, lanes(hs)], nw).astype(o_ref.dtype)
                    else:
                        o_scr[hs, sl, :] = o
        return carry

    trips = n_chunks // group
    if with_out:
        lax.fori_loop(0, trips // 2, functools.partial(scan, final=False), 0)
        lax.fori_loop(trips // 2, trips, functools.partial(scan, final=True), 0)
    else:
        lax.fori_loop(0, trips, functools.partial(scan, final=False), 0)


def _gdn_call(p, gates, conv_w, hp, nw, s0f, s0b, period, with_out, sec0):
    b, t, _ = p.shape
    n_chunks = t // CHUNK
    prep_rows = max(period, min(t, 256))
    assert prep_rows % period == 0 and t % prep_rows == 0
    group = _scan_group(n_chunks, GDN_GROUP, with_out)
    g4 = gates[:, :, :4 * N_HEADS].reshape(b, t, 4, N_HEADS)
    g_col = g4.transpose(0, 3, 1, 2)
    g_row = jnp.pad(g4.transpose(0, 3, 2, 1), ((0, 0), (0, 0), (0, 4), (0, 0)))
    g_row = g_row.reshape(b, N_HEADS, 8, n_chunks, CHUNK).transpose(0, 1, 3, 2, 4)
    g_row = jnp.concatenate([g_row, g_row], axis=-1)

    nh = GDN_HEADS if with_out else GDN_CTX_HEADS
    assert N_HEADS % nh == 0
    wide = nh * HEAD_DIM
    col = lambda sec: pl.BlockSpec((None, t, wide), lambda b_, h: (b_, 0, sec * (N_HEADS // nh) + h))
    gc_spec = pl.BlockSpec((None, nh, t, 4), lambda b_, h: (b_, h, 0, 0))
    gr_spec = pl.BlockSpec((None, nh, n_chunks, 8, 2 * CHUNK), lambda b_, h: (b_, h, 0, 0, 0))
    conv_spec = lambda sec: pl.BlockSpec((3, wide), lambda b_, h: (0, sec * (N_HEADS // nh) + h))
    hp_spec = pl.BlockSpec((nh, 8, HEAD_DIM), lambda b_, h: (h, 0, 0))
    state_spec = pl.BlockSpec((None, nh, HEAD_DIM, HEAD_DIM), lambda b_, h: (b_, h, 0, 0))
    seq = lambda: pltpu.VMEM((nh, t, HEAD_DIM), F32)
    mats = lambda dt: pltpu.VMEM((nh, 2, n_chunks, HEAD_DIM, HEAD_DIM), dt)
    dc = pltpu.VMEM((nh, 2, n_chunks, 8, HEAD_DIM), F32)
    state = lambda: pltpu.VMEM((nh, HEAD_DIM, HEAD_DIM), F32)
    kern = functools.partial(_gdn_kernel, n_chunks=n_chunks, period=period, prep_rows=prep_rows,
                             group=group, igroup=_scan_group(n_chunks, GDN_INTRA_GROUP, False),
                             heads=nh, with_out=with_out)
    if with_out:
        return pl.pallas_call(
            kern,
            grid=(b, N_HEADS // nh),
            in_specs=[col(sec0), col(sec0 + 1), col(sec0 + 2), col(sec0 + 3),
                      gc_spec, gr_spec, conv_spec(0), conv_spec(1), conv_spec(2), hp_spec,
                      pl.BlockSpec((1, HEAD_DIM), lambda b_, h: (0, 0)), state_spec, state_spec],
            out_specs=pl.BlockSpec((None, t, wide), lambda b_, h: (b_, 0, h)),
            out_shape=jax.ShapeDtypeStruct((b, t, N_HEADS * HEAD_DIM), BF16),
            scratch_shapes=[seq(), seq(), seq(), mats(BF16), mats(F32), dc,
                            pltpu.VMEM((nh, 2, t, HEAD_DIM), BF16), seq(), state(), state()],
            compiler_params=_cparams("parallel", "parallel"),
            name="gdn_x",
        )(p, p, p, p, g_col, g_row, conv_w, conv_w, conv_w, hp, nw, s0f, s0b)
    shp = jax.ShapeDtypeStruct((b, N_HEADS, HEAD_DIM, HEAD_DIM), F32)
    return pl.pallas_call(
        kern,
        grid=(b, N_HEADS // nh),
        in_specs=[col(sec0 + 1), col(sec0 + 2), gc_spec, gr_spec, conv_spec(1), conv_spec(2), hp_spec],
        out_specs=[state_spec, state_spec],
        out_shape=[shp, shp],
        scratch_shapes=[seq(), seq(), mats(BF16), mats(F32), dc],
        compiler_params=_cparams("parallel", "parallel"),
        name="gdn_ctx",
    )(p, p, g_col, g_row, conv_w, conv_w, hp)


def _outproj_kernel(mh_ref, mg_ref, x_ref, gt_ref, sc_ref, sh_ref, nw_ref, woh_ref, wog_ref,
                    wrh_ref, wrl_ref, br_ref, x1_ref, h2_ref, idx_ref, gate_ref, cnt_ref):
    y = _dot(mh_ref[...], woh_ref[...]) + _dot(mg_ref[...], wog_ref[...])
    x1 = x_ref[...] + gt_ref[...] * y
    x1_ref[...] = x1
    h = _modulated_norm(x1, nw_ref[...], sc_ref[...], sh_ref[...])
    hh = h.astype(BF16)
    half = hh.shape[1] // 2
    hi_bits = lax.bitcast_convert_type(hh[:, :half].astype(F32), jnp.uint32)
    lo_bits = lax.bitcast_convert_type(hh[:, half:].astype(F32), jnp.uint32)
    h2_ref[...] = hi_bits | (lo_bits >> 16)
    hl = (h - hh.astype(F32)).astype(BF16)
    wrh = wrh_ref[...]
    logits = _dot(hh, wrh) + _dot(hl, wrh) + _dot(hh, wrl_ref[...]) + br_ref[...]
    lane = lax.broadcasted_iota(jnp.int32, logits.shape, 1).astype(F32)
    vals, idxs = [], []
    for _ in range(TOP_K):
        m = jnp.max(logits, axis=-1, keepdims=True)
        i = jnp.min(jnp.where(logits == m, lane, float(LANES)), axis=-1, keepdims=True)
        vals.append(m)
        idxs.append(i)
        logits = jnp.where(lane == i, -jnp.inf, logits)
    es = [jnp.exp(v - vals[0]) for v in vals]
    inv = 1.0 / functools.reduce(lambda a, b_: a + b_, es)

    @pl.when(pl.program_id(0) == 0)
    def _():
        cnt_ref[...] = jnp.zeros_like(cnt_ref)

    onehots = [lane == i for i in idxs]
    picked = functools.reduce(lambda a, b_: a | b_, onehots)
    tm = lane.shape[0]
    before = (lax.broadcasted_iota(jnp.int32, (tm, tm), 1) < lax.broadcasted_iota(jnp.int32, (tm, tm), 0))
    prior = cnt_ref[...] + _dot(before.astype(BF16), picked.astype(BF16))
    ranks = [jnp.sum(jnp.where(oh, prior, 0.0), axis=-1, keepdims=True) for oh in onehots]
    cnt_ref[...] += jnp.sum(picked.astype(F32), axis=0, keepdims=True)

    idx_out = jnp.zeros(lane.shape, F32)
    gate_out = jnp.zeros(lane.shape, F32)
    for k in range(TOP_K):
        idx_out = jnp.where(lane == k, idxs[k], idx_out)
        idx_out = jnp.where(lane == TOP_K + k, ranks[k], idx_out)
        gate_out = jnp.where(lane == k, es[k] * inv, gate_out)
    idx_ref[...] = idx_out.astype(jnp.int32)
    gate_ref[...] = gate_out


def _outproj(mix_h, mix_g, x, gt1, sc2, sh2, nw, w_out2, wr_hi, wr_lo, br):
    b, t, d = x.shape
    w = mix_h.shape[-1]
    tm = min(t, 512)
    tpb = t // tm
    row = lambda width: pl.BlockSpec((None, tm, width), lambda i: (i // tpb, i % tpb, 0))
    per_b = pl.BlockSpec((None, 1, d), lambda i: (i // tpb, 0, 0))
    const = lambda r, c: pl.BlockSpec((r, c), lambda i: (0, 0))
    return pl.pallas_call(
        _outproj_kernel,
        grid=(b * tpb,),
        in_specs=[row(w), row(w), row(d), per_b, per_b, per_b, const(1, d),
                  pl.BlockSpec((None, w, d), lambda i: (0, 0, 0)), pl.BlockSpec((None, w, d), lambda i: (1, 0, 0)),
                  const(d, LANES), const(d, LANES), const(1, LANES)],
        out_specs=[row(d), row(d // 2), row(LANES), row(LANES), const(1, LANES)],
        out_shape=[jax.ShapeDtypeStruct((b, t, d), F32), jax.ShapeDtypeStruct((b, t, d // 2), jnp.uint32),
                   jax.ShapeDtypeStruct((b, t, LANES), jnp.int32), jax.ShapeDtypeStruct((b, t, LANES), F32),
                   jax.ShapeDtypeStruct((1, LANES), F32)],
        compiler_params=_cparams("arbitrary"),
        name="outproj",
    )(mix_h, mix_g, x, gt1, sc2, sh2, nw.reshape(1, d), w_out2, w_out2, wr_hi, wr_lo, br)


def _dispatch_kernel(dest_ref, zrow_ref, h_ref, xs_ref, zbuf, sem, zsem, *, tt, k, tm, n_exp):
    @pl.when(pl.program_id(0) == 0)
    def _():
        zbuf[...] = jnp.zeros_like(zbuf)
        zero_block = lambda e: pltpu.make_async_copy(
            zbuf, xs_ref.at[pl.ds(pl.multiple_of(zrow_ref[e], tm), tm), :], zsem)
        for e in range(n_exp):
            @pl.when(zrow_ref[e] >= 0)
            def _():
                zero_block(e).start()
        for e in range(n_exp):
            @pl.when(zrow_ref[e] >= 0)
            def _():
                zero_block(e).wait()

        def unused_block(b):
            return pltpu.make_async_copy(zbuf, xs_ref.at[pl.ds(pl.multiple_of(b * tm, tm), tm), :], zsem)

        def start_unused(b, carry):
            unused_block(b).start()
            return carry

        def wait_unused(b, carry):
            unused_block(b).wait()
            return carry

        lax.fori_loop(zrow_ref[n_exp], xs_ref.shape[0] // tm, start_unused, 0)
        lax.fori_loop(zrow_ref[n_exp], xs_ref.shape[0] // tm, wait_unused, 0)

    base = pl.program_id(0) * (tt * k)

    def body(tb, carry):
        t0 = pl.multiple_of(tb * 8, 8)
        for u in range(8):
            for kk in range(k):
                row = dest_ref[base + (t0 + u) * k + kk]
                pltpu.make_async_copy(h_ref.at[pl.ds(t0 + u, 1), :], xs_ref.at[pl.ds(row, 1), :], sem).start()
        return carry

    lax.fori_loop(0, tt // 8, body, 0)
    done = xs_ref.at[pl.ds(0, tt * k), :]
    pltpu.make_async_copy(done, done, sem).wait()


def _dispatch(h2, dest, zrow, rows, k, tm):
    n_tok, w = h2.shape
    tt = min(n_tok, 512)
    n_exp = zrow.shape[0] - 1
    return pl.pallas_call(
        functools.partial(_dispatch_kernel, tt=tt, k=k, tm=tm, n_exp=n_exp),
        grid_spec=pltpu.PrefetchScalarGridSpec(
            num_scalar_prefetch=2,
            grid=(n_tok // tt,),
            in_specs=[pl.BlockSpec((tt, w), lambda i, dest_, zrow_: (i, 0))],
            out_specs=pl.BlockSpec(memory_space=pl.ANY),
            scratch_shapes=[pltpu.VMEM((tm, w), h2.dtype), pltpu.SemaphoreType.DMA(()),
                            pltpu.SemaphoreType.DMA(())]),
        out_shape=jax.ShapeDtypeStruct((rows, w), h2.dtype),
        compiler_params=_cparams("arbitrary"),
        name="dispatch",
    )(dest, zrow, h2)


def _cast_tile(src, dst):
    rows = 256

    def body(i, carry):
        sl = pl.ds(pl.multiple_of(i * rows, rows), rows)
        dst[sl, :] = src[sl, :].astype(dst.dtype)
        return carry

    lax.fori_loop(0, src.shape[0] // rows, body, 0)


S_E, S_J, S_R, S_RO, S_JO, S_FIRST, S_VALID, S_NE, S_NJ, S_SLOT = range(10)


def _weight_group_prefetch(sc_ref, copies, on_ready):
    s = pl.program_id(0)

    @pl.when(s == 0)
    def _():
        for c in copies(sc_ref[S_E, 0], sc_ref[S_J, 0], 0):
            c.start()

    @pl.when(sc_ref[S_FIRST, s] == 1)
    def _():
        slot = sc_ref[S_SLOT, s]

        @pl.when(sc_ref[S_NE, s] >= 0)
        def _():
            for c in copies(sc_ref[S_NE, s], sc_ref[S_NJ, s], 1 - slot):
                c.start()

        for c in copies(sc_ref[S_E, s], sc_ref[S_J, s], slot):
            c.wait()
        on_ready(slot)


def _moe_up_kernel(sc_ref, x_ref, bg_ref, bu_ref, wg_hbm, wu_hbm, act_ref, stage, wg_b, wu_b, sems, *, tf):
    s = pl.program_id(0)

    def copies(e, j, slot):
        cols = pl.ds(pl.multiple_of(j * tf, tf), tf)
        return [pltpu.make_async_copy(wg_hbm.at[e, :, cols], stage.at[slot, 0], sems.at[slot, 0]),
                pltpu.make_async_copy(wu_hbm.at[e, :, cols], stage.at[slot, 1], sems.at[slot, 1])]

    def on_ready(slot):
        _cast_tile(stage.at[slot, 0], wg_b)
        _cast_tile(stage.at[slot, 1], wu_b)

    _weight_group_prefetch(sc_ref, copies, on_ready)

    @pl.when(sc_ref[S_VALID, s] == 1)
    def _():
        xp = x_ref[...]
        half = xp.shape[1]
        xa = lax.bitcast_convert_type(xp & jnp.uint32(0xFFFF0000), F32).astype(BF16)
        xb = lax.bitcast_convert_type(xp << 16, F32).astype(BF16)
        gate = _dot(xa, wg_b[:half, :]) + _dot(xb, wg_b[half:, :]) + bg_ref[...]
        up = _dot(xa, wu_b[:half, :]) + _dot(xb, wu_b[half:, :]) + bu_ref[...]
        gate = jnp.minimum(gate, SWIGLU_LIMIT)
        up = jnp.clip(up, -SWIGLU_LIMIT, SWIGLU_LIMIT)
        act_ref[...] = ((up + 1.0) * gate * _sigmoid(SWIGLU_ALPHA * gate)).astype(act_ref.dtype)

    @pl.when(sc_ref[S_VALID, s] == 0)
    def _():
        act_ref[...] = jnp.zeros_like(act_ref)


def _moe_down_kernel(sc_ref, a_ref, bd_ref, wd_hbm, y_ref, stage, wd_b, sems, *, tf):
    s = pl.program_id(0)

    def copies(e, j, slot):
        del j
        return [pltpu.make_async_copy(wd_hbm.at[e], stage.at[slot], sems.at[slot])]

    _weight_group_prefetch(sc_ref, copies, lambda slot: _cast_tile(stage.at[slot], wd_b))

    @pl.when(sc_ref[S_VALID, s] == 1)
    def _():
        cols = pl.ds(pl.multiple_of(sc_ref[S_J, s] * tf, tf), tf)
        y_ref[...] = _dot(a_ref[...], wd_b[:, cols]) + bd_ref[...]

    @pl.when(sc_ref[S_VALID, s] == 0)
    def _():
        y_ref[...] = jnp.zeros_like(y_ref)


def _moe_experts(xs, sched_up, sched_down, wg, wu, wd, bg, bu, bd, tm, tf):
    rows = xs.shape[0]
    n_exp, d, f = wg.shape
    assert xs.shape[1] * 2 == d and d // tf == f // tf
    steps = sched_up.shape[1]
    blk = lambda shape, at: pl.BlockSpec(shape, lambda s, sc: at(sc, s))
    hbm = pl.BlockSpec(memory_space=pl.ANY)
    act = pl.pallas_call(
        functools.partial(_moe_up_kernel, tf=tf),
        grid_spec=pltpu.PrefetchScalarGridSpec(
            num_scalar_prefetch=1,
            grid=(steps,),
            in_specs=[blk((tm, d // 2), lambda sc, s: (sc[S_R, s], 0)),
                      blk((None, 1, tf), lambda sc, s: (sc[S_E, s], 0, sc[S_J, s])),
                      blk((None, 1, tf), lambda sc, s: (sc[S_E, s], 0, sc[S_J, s])),
                      hbm, hbm],
            out_specs=blk((tm, tf), lambda sc, s: (sc[S_RO, s], sc[S_JO, s])),
            scratch_shapes=[pltpu.VMEM((2, 2, d, tf), F32), pltpu.VMEM((d, tf), BF16), pltpu.VMEM((d, tf), BF16),
                            pltpu.SemaphoreType.DMA((2, 2))]),
        out_shape=jax.ShapeDtypeStruct((rows, f), BF16),
        compiler_params=_cparams("arbitrary"),
        name="moe_up",
    )(sched_up, xs, bg.reshape(n_exp, 1, f), bu.reshape(n_exp, 1, f), wg, wu)
    return pl.pallas_call(
        functools.partial(_moe_down_kernel, tf=tf),
        grid_spec=pltpu.PrefetchScalarGridSpec(
            num_scalar_prefetch=1,
            grid=(steps,),
            in_specs=[blk((tm, f), lambda sc, s: (sc[S_R, s], 0)),
                      blk((None, 1, tf), lambda sc, s: (sc[S_E, s], 0, sc[S_J, s])),
                      hbm],
            out_specs=blk((tm, tf), lambda sc, s: (sc[S_RO, s], sc[S_JO, s])),
            scratch_shapes=[pltpu.VMEM((2, f, d), F32), pltpu.VMEM((f, d), BF16), pltpu.SemaphoreType.DMA((2,))]),
        out_shape=jax.ShapeDtypeStruct((rows, d), F32),
        compiler_params=_cparams("arbitrary"),
        name="moe_down",
    )(sched_down, act, bd.reshape(n_exp, 1, d), wd)


def _route(top_idx, rank, counts, tm, n_tiles):
    n_tok, k = top_idx.shape
    n_exp = counts.shape[0]
    m = n_tok * k
    padded = (counts + tm - 1) // tm * tm
    pend = jnp.cumsum(padded)
    onehot = top_idx.reshape(m, 1) == jnp.arange(n_exp, dtype=jnp.int32)[None, :]
    dest = jnp.sum(jnp.where(onehot, (pend - padded)[None, :], 0), axis=1) + rank.reshape(m)
    n_blocks = -(-m // tm) + n_exp

    nb = padded // tm
    blk0 = (pend - padded) // tm
    cum = jnp.cumsum(nb * n_tiles)
    total = cum[-1]
    s = jnp.arange(n_tiles * n_blocks, dtype=jnp.int32)
    valid = s < total
    sc = jnp.minimum(s, total - 1)
    e = jnp.minimum(jnp.sum((cum[None, :] <= sc[:, None]).astype(jnp.int32), axis=1), n_exp - 1)
    of_e = e[:, None] == jnp.arange(n_exp, dtype=jnp.int32)[None, :]
    pick = lambda v: jnp.sum(jnp.where(of_e, v[None, :], 0), axis=1)
    nb_e = pick(nb)
    local = sc - (pick(cum) - nb_e * n_tiles)
    nbe = jnp.maximum(nb_e, 1)
    blk0_e = pick(blk0)
    extra = s - total
    ids = jnp.arange(n_exp, dtype=jnp.int32)
    later = jnp.where((ids[None, :] > ids[:, None]) & (nb[None, :] > 0), ids[None, :], n_exp)
    next_e = jnp.min(later, axis=1)
    next_e = pick(jnp.where(next_e == n_exp, -1, next_e))
    erank = pick(jnp.cumsum((nb > 0).astype(jnp.int32)) - 1)

    def rows_of(j, r, first, ne, nj, group):
        r_out = jnp.where(valid, r, total // n_tiles + extra // n_tiles)
        j_out = jnp.where(valid, j, extra % n_tiles)
        fields = (e, j, r, r_out, j_out, valid & first, valid, ne, nj, group % 2)
        return jnp.stack([a.astype(jnp.int32) for a in fields])

    j_up = local // nbe
    last_tile = j_up == n_tiles - 1
    sched_up = rows_of(j_up, blk0_e + local % nbe, local % nbe == 0,
                       jnp.where(last_tile, next_e, e), jnp.where(last_tile, 0, j_up + 1),
                       erank * n_tiles + j_up)
    sched_down = rows_of(local % n_tiles, blk0_e + local // n_tiles, local == 0, next_e, jnp.zeros_like(e), erank)
    zrow = jnp.concatenate([jnp.where(padded > 0, pend - tm, -1), pend[-1:] // tm]).astype(jnp.int32)
    return dest.astype(jnp.int32), zrow, n_blocks * tm, sched_up, sched_down


def _combine_kernel(dest_ref, x1_ref, gate_ref, gt_ref, nw_ref, ys_ref, o_ref, ybuf, sems, *, tt, k):
    i = pl.program_id(0)
    n = pl.num_programs(0)

    def start_gather(tile, slot):
        base = tile * (tt * k)

        def body(tb, carry):
            for u in range(8):
                for j in range(k):
                    row = dest_ref[base + (tb * 8 + u) * k + j]
                    pltpu.make_async_copy(ys_ref.at[row >> 3, pl.ds(row & 7, 1), :],
                                          ybuf.at[slot, j * (tt // 8) + tb, pl.ds(u, 1), :], sems.at[slot]).start()
            return carry

        lax.fori_loop(0, tt // 8, body, 0)

    @pl.when(i == 0)
    def _():
        start_gather(0, 0)

    for nxt in (0, 1):
        @pl.when((i + 1 < n) & ((i + 1) % 2 == nxt))
        def _():
            start_gather(i + 1, nxt)

    slot = i % 2
    pltpu.make_async_copy(ybuf.at[slot], ybuf.at[slot], sems.at[slot]).wait()
    g = gate_ref[...]
    d = o_ref.shape[-1]
    picked = lambda j: ybuf[slot, pl.ds(j * (tt // 8), tt // 8), :, :].reshape(tt, d)
    moe = picked(0) * g[:, 0:1]
    for j in range(1, k):
        moe = moe + picked(j) * g[:, j:j + 1]
    x = x1_ref[...] + gt_ref[...] * moe
    ms = jnp.mean(x * x, axis=-1, keepdims=True)
    o_ref[...] = x * lax.rsqrt(ms + EPS) * nw_ref[...]


def _combine(x1, ys, dest, gates, gt2, nw, k):
    b, t, d = x1.shape
    tt = min(t, 256)
    tpb = t // tt
    row = lambda width: pl.BlockSpec((None, tt, width), lambda i, dest_: (i // tpb, i % tpb, 0))
    return pl.pallas_call(
        functools.partial(_combine_kernel, tt=tt, k=k),
        grid_spec=pltpu.PrefetchScalarGridSpec(
            num_scalar_prefetch=1,
            grid=(b * tpb,),
            in_specs=[row(d), row(LANES),
                      pl.BlockSpec((None, 1, d), lambda i, dest_: (i // tpb, 0, 0)),
                      pl.BlockSpec((1, d), lambda i, dest_: (0, 0)),
                      pl.BlockSpec(memory_space=pl.ANY)],
            out_specs=row(d),
            scratch_shapes=[pltpu.VMEM((2, k * tt // 8, 8, d), F32), pltpu.SemaphoreType.DMA((2,))]),
        out_shape=jax.ShapeDtypeStruct((b, t, d), F32),
        compiler_params=_cparams("arbitrary"),
        name="combine",
    )(dest, x1, gates, gt2, nw.reshape(1, d), ys.reshape(ys.shape[0] // 8, 8, d))


def kernel(x, c, ctx, c_ctx, w_ada, b_ada, norm_mix_w, w_in, hg_lb_f, hg_lb_b, hg_norm_w, gd_conv_w,
           gd_a_log_f, gd_a_log_b, gd_dt_bias_f, gd_dt_bias_b, gd_norm_w, w_out, norm_ffn_w, w_router,
           b_router, w_gate, b_gate, w_up, b_up, w_down, b_down, norm_out_w):
    bsz, t, d = x.shape
    t_ctx = ctx.shape[1]
    n_exp = w_router.shape[-1]
    assert w_ada.shape[0] == 1, "single-layer block: the context stream only feeds the scan states"
    l = 0
    hg_w = N_HEADS * HEAD_DIM
    n_main = 9 * hg_w

    lb_f = jnp.cumsum(jax.nn.softmax(hg_lb_f.astype(F32), axis=0), axis=0)[l].reshape(1, hg_w)
    lb_b = jnp.cumsum(jax.nn.softmax(hg_lb_b.astype(F32), axis=0), axis=0)[l].reshape(1, hg_w)

    rows = -(-(bsz + 1) // 8) * 8
    cc = jnp.zeros((rows, d), F32).at[:bsz].set(c).at[bsz].set(c_ctx)
    mod = _ada(cc, w_ada[l], b_ada[l])
    sh1, sc1, gt1, sh2, sc2, gt2 = (mod[:bsz, i * d:(i + 1) * d].reshape(bsz, 1, d) for i in range(6))
    csh1 = mod[bsz, 0:d].reshape(1, 1, d)
    csc1 = mod[bsz, d:2 * d].reshape(1, 1, d)

    w_in_b = w_in[l].astype(BF16)
    w_gates = jnp.pad(w_in_b[:, n_main:], ((0, 0), (0, LANES - (w_in_b.shape[1] - n_main))))
    px, gx = _inproj(x, norm_mix_w[l], sc1, sh1, w_in_b, n_main, w_gates)
    pc, gc = _inproj(ctx.reshape(1, bsz * t_ctx, d), norm_mix_w[l], csc1, csh1, w_in_b, n_main, w_gates)
    pc = pc.reshape(bsz, t_ctx, n_main)
    gc = gc.reshape(bsz, t_ctx, LANES)

    hg_nw = hg_norm_w[l].reshape(1, HEAD_DIM)
    hs_f, hs_b = _hgrn_states(pc, lb_f, lb_b)
    mix_h = _hgrn_out(px, lb_f, lb_b, hg_nw, hs_f, hs_b)

    hp = jnp.stack([gd_a_log_f[l], gd_dt_bias_f[l], gd_a_log_b[l], gd_dt_bias_b[l]], axis=1)
    hp = jnp.broadcast_to(jnp.pad(hp, ((0, 0), (0, 4)))[:, :, None], (N_HEADS, 8, HEAD_DIM)).astype(F32)
    gd_nw = gd_norm_w[l].reshape(1, HEAD_DIM)
    gs_f, gs_b = _gdn_call(pc, gc, gd_conv_w[l], hp, gd_nw, None, None, t_ctx, False, 5)
    mix_g = _gdn_call(px, gx, gd_conv_w[l], hp, gd_nw, gs_f, gs_b, GRID_W, True, 5)

    w_out_b = w_out[l].astype(BF16)
    wr = jnp.pad(w_router[l], ((0, 0), (0, LANES - n_exp)))
    wr_hi = wr.astype(BF16)
    wr_lo = (wr - wr_hi.astype(F32)).astype(BF16)
    br = jnp.full((1, LANES), NEG_BIG, F32).at[0, :n_exp].set(b_router[l])
    x1, h2, idx_pad, gate_pad, cnt = _outproj(mix_h, mix_g, x, gt1, sc2, sh2, norm_ffn_w[l],
                                              w_out_b.reshape(2, hg_w, d), wr_hi, wr_lo, br)

    n_tok = bsz * t
    idx_pad = idx_pad.reshape(n_tok, LANES)
    dest, zrow, n_rows, sched_up, sched_down = _route(idx_pad[:, :TOP_K], idx_pad[:, TOP_K:2 * TOP_K],
                                                      cnt[0, :n_exp].astype(jnp.int32), MOE_TM,
                                                      w_gate.shape[-1] // MOE_TF)
    xs = _dispatch(h2.reshape(n_tok, d // 2), dest, zrow, n_rows, TOP_K, MOE_TM)
    ys = _moe_experts(xs, sched_up, sched_down, w_gate[l], w_up[l], w_down[l], b_gate[l], b_up[l], b_down[l],
                      MOE_TM, MOE_TF)

    return _combine(x1, ys, dest, gate_pad, gt2, norm_out_w, TOP_K)
```

```python
import functools

import jax
import jax.numpy as jnp
from jax import lax
from jax.experimental import pallas as pl
from jax.experimental.pallas import tpu as pltpu

F32 = jnp.float32
BF16 = jnp.bfloat16

EPS = 1e-6
CHUNK = 64
HEAD_DIM = 128
N_HEADS = 8
GRID_W = 64
TOP_K = 4
SWIGLU_LIMIT = 7.0
SWIGLU_ALPHA = 1.702
LANES = 128
NEG_BIG = -1e30

HGRN_GROUP = 8
GDN_GROUP = 4
GDN_INTRA_GROUP = 16
GDN_HEADS = 2
GDN_CTX_HEADS = 4
MOE_TM = 512
MOE_TF = 1024
VMEM_LIMIT = 56 * 1024 * 1024


def _cparams(*sem):
    return pltpu.CompilerParams(dimension_semantics=sem, vmem_limit_bytes=VMEM_LIMIT)


def _dot(a, b):
    return jnp.dot(a, b, preferred_element_type=F32)


def _dot_nt(a, b):
    return lax.dot_general(a, b, (((1,), (1,)), ((), ())), preferred_element_type=F32)


def _split2(x):
    hi = x.astype(BF16)
    return hi, (x - hi.astype(F32)).astype(BF16)


def _split3(x):
    hi = x.astype(BF16)
    rest = x - hi.astype(F32)
    mid = rest.astype(BF16)
    return hi, mid, (rest - mid.astype(F32)).astype(BF16)


def _sigmoid(x):
    return 1.0 / (1.0 + jnp.exp(-x))


def _ada_kernel(c_ref, w_ref, b_ref, o_ref):
    c = c_ref[...]
    s = (c * _sigmoid(c)).astype(BF16)
    o_ref[...] = _dot(s, w_ref[...].astype(BF16)) + b_ref[...]


def _ada(cc, w_ada, b_ada):
    rows, d = cc.shape
    n = w_ada.shape[1]
    tn = 1024
    return pl.pallas_call(
        _ada_kernel,
        grid=(n // tn,),
        in_specs=[pl.BlockSpec((rows, d), lambda j: (0, 0)),
                  pl.BlockSpec((d, tn), lambda j: (0, j)),
                  pl.BlockSpec((1, tn), lambda j: (0, j))],
        out_specs=pl.BlockSpec((rows, tn), lambda j: (0, j)),
        out_shape=jax.ShapeDtypeStruct((rows, n), F32),
        compiler_params=_cparams("arbitrary"),
        name="ada",
    )(cc, w_ada, b_ada.reshape(1, n))


def _modulated_norm(x, nw, sc, sh):
    ms = jnp.mean(x * x, axis=-1, keepdims=True)
    return (x * lax.rsqrt(ms + EPS) * nw) * (1.0 + sc) + sh


def _inproj_kernel(x_ref, nw_ref, sc_ref, sh_ref, w_ref, wg_ref, o_ref, og_ref, h_scr):
    @pl.when(pl.program_id(1) == 0)
    def _():
        h = _modulated_norm(x_ref[...], nw_ref[...], sc_ref[...], sh_ref[...]).astype(BF16)
        h_scr[...] = h
        og_ref[...] = _dot(h, wg_ref[...])

    o_ref[...] = _dot(h_scr[...], w_ref[...])


def _inproj(x, nw, sc, sh, w_all, n, w_gates):
    b, t, d = x.shape
    w_main = w_all
    tm = min(t, 1024)
    tn = 1536 if n % 1536 == 0 else 1024
    tpb = t // tm
    assert n % tn == 0
    return pl.pallas_call(
        _inproj_kernel,
        grid=(b * tpb, n // tn),
        in_specs=[pl.BlockSpec((None, tm, d), lambda i, j: (i // tpb, i % tpb, 0)),
                  pl.BlockSpec((1, d), lambda i, j: (0, 0)),
                  pl.BlockSpec((None, 1, d), lambda i, j: (i // tpb, 0, 0)),
                  pl.BlockSpec((None, 1, d), lambda i, j: (i // tpb, 0, 0)),
                  pl.BlockSpec((d, tn), lambda i, j: (0, j)),
                  pl.BlockSpec((d, LANES), lambda i, j: (0, 0))],
        out_specs=[pl.BlockSpec((None, tm, tn), lambda i, j: (i // tpb, i % tpb, j)),
                   pl.BlockSpec((None, tm, LANES), lambda i, j: (i // tpb, i % tpb, 0))],
        out_shape=[jax.ShapeDtypeStruct((b, t, n), F32),
                   jax.ShapeDtypeStruct((b, t, LANES), F32)],
        scratch_shapes=[pltpu.VMEM((tm, d), BF16)],
        compiler_params=_cparams("parallel", "arbitrary"),
        name="inproj",
    )(x, nw.reshape(1, d), sc, sh, w_main, w_gates)


def _chunk_masks(rev, wide=False):
    shape = (CHUNK, 2 * CHUNK if wide else CHUNK)
    r = lax.broadcasted_iota(jnp.int32, shape, 0)
    c = lax.broadcasted_iota(jnp.int32, shape, 1) & (CHUNK - 1)
    incl = (c >= r) if rev else (c <= r)
    strict = (c > r) if rev else (c < r)
    return incl, strict


def _gated_head_norm(o, z, nw):
    ms = jnp.mean(o * o, axis=-1, keepdims=True)
    return (o * lax.rsqrt(ms + EPS) * nw) * (z * _sigmoid(z))


def _group_slices(g, group, n_chunks):
    idxs, revs = [], []
    for i in range(group):
        n = g * group + i
        idxs += [n, n_chunks - 1 - n]
        revs += [False, True]
    return [pl.ds(pl.multiple_of(ix * CHUNK, CHUNK), CHUNK) for ix in idxs], idxs, revs


def _scan_group(n_chunks, want, two_phase):
    g = want
    while g > 1 and (n_chunks % g or (two_phase and (n_chunks // g) % 2)):
        g //= 2
    assert n_chunks % g == 0 and not (two_phase and (n_chunks // g) % 2)
    return g


def _gla_group(zfs, vs, qs, lbs, revs):
    n = len(zfs)
    incl = [_chunk_masks(r)[0] for r in revs]
    logf, k = [], []
    for i in range(n):
        one_m = 1.0 - lbs[i]
        logf.append(jnp.log(lbs[i] + one_m * _sigmoid(zfs[i])))
        k.append(one_m * _sigmoid(-zfs[i]))
    cum = []
    for i in range(n):
        tri = incl[i].astype(BF16)
        hi, mid, lo = _split3(logf[i])
        parts = _dot(tri, jnp.concatenate([hi, mid, lo], axis=1))
        cum.append(parts[:, :HEAD_DIM] + (parts[:, HEAD_DIM:2 * HEAD_DIM] + parts[:, 2 * HEAD_DIM:]))
    last = [cum[i][0:1] if revs[i] else cum[i][CHUNK - 1:CHUNK] for i in range(n)]
    u_t = [_dot(vs[i].T.astype(BF16), (k[i] * jnp.exp(last[i] - cum[i])).astype(BF16)) for i in range(n)]
    dec = [jnp.exp(x) for x in last]
    if qs is None:
        return u_t, dec, None, None
    scores = []
    for i in range(n):
        mid_row = CHUNK // 2 - 1 if revs[i] else CHUNK // 2
        ref = cum[i][mid_row:mid_row + 1]
        sc = _dot_nt((qs[i] * jnp.exp(cum[i] - ref)).astype(BF16), (k[i] * jnp.exp(ref - cum[i])).astype(BF16))
        scores.append(jnp.where(incl[i], sc, 0.0).astype(BF16))
    o_intra = [_dot(scores[i], vs[i].astype(BF16)) for i in range(n)]
    qe = [(qs[i] * jnp.exp(cum[i])).astype(BF16) for i in range(n)]
    return u_t, dec, o_intra, qe


def _hgrn_out_kernel(q_ref, ff_ref, fb_ref, v_ref, g_ref, lbf_ref, lbb_ref, nw_ref, s0f_ref, s0b_ref,
                     o_ref, o_scr, sf_scr, sb_scr, *, n_chunks, group):
    sf_scr[...] = s0f_ref[...]
    sb_scr[...] = s0b_ref[...]
    lbf, lbb, nw = lbf_ref[...], lbb_ref[...], nw_ref[...]

    def body(g, carry, final):
        sls, _, revs = _group_slices(g, group, n_chunks)
        zfs = [(fb_ref if r else ff_ref)[sl, :] for sl, r in zip(sls, revs)]
        vs = [v_ref[sl, :] for sl in sls]
        qs = [q_ref[sl, :] for sl in sls]
        u_t, dec, o_intra, qe = _gla_group(zfs, vs, qs, [lbb if r else lbf for r in revs], revs)
        for i, (sl, rev) in enumerate(zip(sls, revs)):
            s_scr = sb_scr if rev else sf_scr
            st = s_scr[...]
            o = o_intra[i] + _dot_nt(qe[i], st.astype(BF16))
            s_scr[...] = st * dec[i] + u_t[i]
            if final:
                o_ref[sl, :] = _gated_head_norm(o_scr[sl, :] + o, g_ref[sl, :], nw).astype(o_ref.dtype)
            else:
                o_scr[sl, :] = o
        return carry

    trips = n_chunks // group
    lax.fori_loop(0, trips // 2, functools.partial(body, final=False), 0)
    lax.fori_loop(trips // 2, trips, functools.partial(body, final=True), 0)


def _hgrn_state_kernel(ff_ref, fb_ref, v_ref, lbf_ref, lbb_ref, sf_ref, sb_ref, *, n_chunks, group):
    sf_ref[...] = jnp.zeros_like(sf_ref)
    sb_ref[...] = jnp.zeros_like(sb_ref)
    lbf, lbb = lbf_ref[...], lbb_ref[...]

    def body(g, carry):
        sls, _, revs = _group_slices(g, group, n_chunks)
        zfs = [(fb_ref if r else ff_ref)[sl, :] for sl, r in zip(sls, revs)]
        vs = [v_ref[sl, :] for sl in sls]
        u_t, dec, _, _ = _gla_group(zfs, vs, None, [lbb if r else lbf for r in revs], revs)
        for i, rev in enumerate(revs):
            s_ref = sb_ref if rev else sf_ref
            s_ref[...] = s_ref[...] * dec[i] + u_t[i]
        return carry

    lax.fori_loop(0, n_chunks // group, body, 0)


def _col_spec(t, section):
    return pl.BlockSpec((None, t, HEAD_DIM), lambda b, h: (b, 0, section * N_HEADS + h))


def _head_row_spec(offset=0):
    return pl.BlockSpec((1, HEAD_DIM), lambda b, h: (0, offset + h))


_STATE_SPEC = pl.BlockSpec((None, None, HEAD_DIM, HEAD_DIM), lambda b, h: (b, h, 0, 0))


def _hgrn_states(pc, lbf, lbb):
    b, t, _ = pc.shape
    n_chunks = t // CHUNK
    shp = jax.ShapeDtypeStruct((b, N_HEADS, HEAD_DIM, HEAD_DIM), F32)
    return pl.pallas_call(
        functools.partial(_hgrn_state_kernel, n_chunks=n_chunks, group=_scan_group(n_chunks, HGRN_GROUP, False)),
        grid=(b, N_HEADS),
        in_specs=[_col_spec(t, 1), _col_spec(t, 2), _col_spec(t, 3), _head_row_spec(), _head_row_spec()],
        out_specs=[_STATE_SPEC, _STATE_SPEC],
        out_shape=[shp, shp],
        compiler_params=_cparams("parallel", "parallel"),
        name="hgrn_ctx",
    )(pc, pc, pc, lbf, lbb)


def _hgrn_out(px, lbf, lbb, nw, s0f, s0b):
    b, t, _ = px.shape
    n_chunks = t // CHUNK
    return pl.pallas_call(
        functools.partial(_hgrn_out_kernel, n_chunks=n_chunks, group=_scan_group(n_chunks, HGRN_GROUP, True)),
        grid=(b, N_HEADS),
        in_specs=[_col_spec(t, 0), _col_spec(t, 1), _col_spec(t, 2), _col_spec(t, 3), _col_spec(t, 4),
                  _head_row_spec(), _head_row_spec(),
                  pl.BlockSpec((1, HEAD_DIM), lambda b_, h: (0, 0)),
                  _STATE_SPEC, _STATE_SPEC],
        out_specs=pl.BlockSpec((None, t, HEAD_DIM), lambda b_, h: (b_, 0, h)),
        out_shape=jax.ShapeDtypeStruct((b, t, N_HEADS * HEAD_DIM), BF16),
        scratch_shapes=[pltpu.VMEM((t, HEAD_DIM), F32),
                        pltpu.VMEM((HEAD_DIM, HEAD_DIM), F32),
                        pltpu.VMEM((HEAD_DIM, HEAD_DIM), F32)],
        compiler_params=_cparams("parallel", "parallel"),
        name="hgrn_x",
    )(px, px, px, px, px, lbf, lbb, nw, s0f, s0b)


def _conv_silu(a, w, period):
    rows = a.shape[0]
    pos = lax.broadcasted_iota(jnp.int32, a.shape, 0) % period
    prev = jnp.where(pos == 0, 0.0, pltpu.roll(a, 1, axis=0))
    nxt = jnp.where(pos == period - 1, 0.0, pltpu.roll(a, rows - 1, axis=0))
    y = prev * w[0:1] + a * w[1:2] + nxt * w[2:3]
    return y * _sigmoid(y)


def _l2norm(x):
    return x * lax.rsqrt(jnp.sum(x * x, axis=-1, keepdims=True) + EPS)


def _softplus(x):
    return jnp.maximum(x, 0.0) + jnp.log1p(jnp.exp(-jnp.abs(x)))


def _unit_tri_inverses(tris):
    row = lax.broadcasted_iota(jnp.int32, (CHUNK, 2 * CHUNK), 0)
    lane = lax.broadcasted_iota(jnp.int32, (CHUNK, 2 * CHUNK), 1)
    left = lane < CHUNK
    eye = jnp.where((lane & (CHUNK - 1)) == row, 1.0, 0.0)
    zeros = jnp.zeros((CHUNK, 2 * CHUNK), BF16)

    def lhs(hl):
        return jnp.where(left, hl[0], hl[1])

    def rhs(hl):
        top = jnp.concatenate([hl[0], hl[1]], axis=1)
        return jnp.concatenate([top, jnp.concatenate([hl[0], zeros], axis=1)], axis=0)

    def fold(o):
        return o[:, :2 * CHUNK] + o[:, 2 * CHUNK:]

    ps = [_split2(-a) for a in tris]
    invs = [eye - a for a in tris]
    ps = [_split2(fold(_dot(lhs(p), rhs(p)))) for p in ps]
    for _ in range(CHUNK.bit_length() - 3):
        outs = [fold(_dot(jnp.concatenate([lhs(p), lhs(_split2(inv))], axis=0), rhs(p)))
                for p, inv in zip(ps, invs)]
        invs = [inv + o[CHUNK:] for inv, o in zip(invs, outs)]
        ps = [_split2(o[:CHUNK]) for o in outs]
    invs = [inv + fold(_dot(lhs(_split2(inv)), rhs(p))) for inv, p in zip(invs, ps)]
    return [inv[:, :CHUNK] for inv in invs]


def _gdn_group(qs, ks, vs, a_cs, b_cs, a_rs, alogs, dtbs, revs):
    n = len(ks)
    left = lax.broadcasted_iota(jnp.int32, (CHUNK, 2 * CHUNK), 1) < CHUNK
    dmask, ecum, beta, eend, dec, strict = [], [], [], [], [], []
    for i in range(n):
        incl, st = _chunk_masks(revs[i], wide=True)
        incl_t = _chunk_masks(not revs[i], wide=True)[0]
        scale = -jnp.exp(alogs[i])
        g_c = scale * _softplus(a_cs[i] + dtbs[i])
        g_r = scale * _softplus(a_rs[i] + dtbs[i])
        cum_c = jnp.sum(jnp.where(incl & left, g_r, 0.0), axis=1, keepdims=True)
        cum_r = jnp.sum(jnp.where(incl_t, g_c, 0.0), axis=0, keepdims=True)
        dmask.append(jnp.exp(jnp.where(incl, cum_c - cum_r, -jnp.inf)))
        ecum.append(jnp.exp(cum_c))
        beta.append(_sigmoid(b_cs[i]))
        last = cum_c[0:1] if revs[i] else cum_c[CHUNK - 1:CHUNK]
        eend.append(jnp.exp(last - cum_c))
        dec.append(jnp.exp(last))
        strict.append(st)
    kbf = [k.astype(BF16) for k in ks]
    kb = [ks[i] * beta[i] for i in range(n)]
    kk = [_dot_nt(kb[i].astype(BF16), jnp.concatenate([kbf[i], kbf[i]], axis=0)) for i in range(n)]
    tris = [jnp.where(strict[i], kk[i] * dmask[i], 0.0) for i in range(n)]
    tinv = [t.astype(BF16) for t in _unit_tri_inverses(tris)]
    dmask = [m[:, :CHUNK] for m in dmask]
    u = [_dot(tinv[i], (vs[i] * beta[i]).astype(BF16)).astype(BF16) for i in range(n)]
    w = [_dot(tinv[i], (kb[i] * ecum[i]).astype(BF16)).astype(BF16) for i in range(n)]
    ke_t = [(ks[i] * eend[i]).T.astype(BF16) for i in range(n)]
    mp = [_dot(ke_t[i], w[i]) for i in range(n)]
    cc = [_dot(ke_t[i], u[i]) for i in range(n)]
    if qs is None:
        return mp, cc, dec, None, None
    attn = [(_dot_nt(qs[i].astype(BF16), kbf[i]) * dmask[i]).astype(BF16) for i in range(n)]
    qp = [qs[i] * ecum[i] - _dot(attn[i], w[i]) for i in range(n)]
    oi = [_dot(attn[i], u[i]) for i in range(n)]
    return mp, cc, dec, qp, oi


def _gdn_kernel(*refs, n_chunks, period, prep_rows, group, igroup, heads, with_out):
    if with_out:
        (q_ref, k_ref, v_ref, z_ref, gc_ref, gr_ref, wq_ref, wk_ref, wv_ref, hp_ref, nw_ref,
         s0f_ref, s0b_ref, o_ref, qn, kn, vn, mp_s, cc_s, dc_s, qp_s, o_scr, sf, sb) = refs
    else:
        (k_ref, v_ref, gc_ref, gr_ref, wk_ref, wv_ref, hp_ref,
         sf, sb, kn, vn, mp_s, cc_s, dc_s) = refs
        q_ref = None

    t = n_chunks * CHUNK
    lanes = lambda hs: slice(hs * HEAD_DIM, (hs + 1) * HEAD_DIM)

    def prep(i, carry):
        sl = pl.ds(pl.multiple_of(i * prep_rows, prep_rows), prep_rows)
        for hs in range(heads):
            ln = lanes(hs)
            kn[hs, sl, :] = _l2norm(_conv_silu(k_ref[sl, ln], wk_ref[:, ln], period))
            vn[hs, sl, :] = _conv_silu(v_ref[sl, ln], wv_ref[:, ln], period)
            if with_out:
                qn[hs, sl, :] = _l2norm(_conv_silu(q_ref[sl, ln], wq_ref[:, ln], period)) * (HEAD_DIM ** -0.5)
        return carry

    lax.fori_loop(0, t // prep_rows, prep, 0)

    trips_i = n_chunks // igroup

    def intra(it, carry):
        hs = it // trips_i
        g = it % trips_i
        hp = hp_ref[hs]
        qs, ks, vs, a_cs, b_cs, a_rs, alogs, dtbs, revs, where = [], [], [], [], [], [], [], [], [], []
        for i in range(igroup):
            n = g * igroup + i
            sl = pl.ds(pl.multiple_of(n * CHUNK, CHUNK), CHUNK)
            gc = gc_ref[hs, sl, :]
            gr = gr_ref[hs, n]
            for d, rev in enumerate((False, True)):
                if with_out:
                    qs.append(qn[hs, sl, :])
                ks.append(kn[hs, sl, :])
                vs.append(vn[hs, sl, :])
                a_cs.append(gc[:, d:d + 1])
                b_cs.append(gc[:, 2 + d:3 + d])
                a_rs.append(gr[d:d + 1, :])
                alogs.append(hp[2 * d:2 * d + 1, 0:1])
                dtbs.append(hp[2 * d + 1:2 * d + 2, 0:1])
                revs.append(rev)
                where.append((d, n, sl))
        mp, cc, dec, qp, oi = _gdn_group(qs if with_out else None, ks, vs, a_cs, b_cs, a_rs, alogs, dtbs, revs)
        for i, (d, n, sl) in enumerate(where):
            mp_s[hs, d, n] = mp[i].astype(BF16)
            cc_s[hs, d, n] = cc[i]
            dc_s[hs, d, n] = jnp.broadcast_to(dec[i], (8, HEAD_DIM))
            if with_out:
                qp_s[hs, d, sl, :] = qp[i].astype(BF16)
                if d == 1:
                    o_scr[hs, sl, :] = oi[i - 1] + oi[i]
        return carry

    lax.fori_loop(0, heads * trips_i, intra, 0)

    if with_out:
        sf[...] = s0f_ref[...]
        sb[...] = s0b_ref[...]
        nw = nw_ref[...]
    else:
        sf[...] = jnp.zeros_like(sf)
        sb[...] = jnp.zeros_like(sb)

    def scan(g, carry, final):
        sls, idxs, revs = _group_slices(g, group, n_chunks)
        for sl, idx, rev in zip(sls, idxs, revs):
            d = int(rev)
            s_ref = sb if rev else sf
            for hs in range(heads):
                s = s_ref[hs]
                sbf = s.astype(BF16)
                s_ref[hs] = s * dc_s[hs, d, idx][0:1, :] - _dot(mp_s[hs, d, idx], sbf) + cc_s[hs, d, idx]
                if with_out:
                    o = o_scr[hs, sl, :] + _dot(qp_s[hs, d, sl, :], sbf)
                    if final:
                        o_ref[sl, lanes(hs)] = _gated_head_norm(o, z_ref[sl, lanes(hs)], nw).astype(o_ref.dtype)
                    else:
                        o_scr[hs, sl, :] = o
        return carry

    trips = n_chunks // group
    if with_out:
        lax.fori_loop(0, trips // 2, functools.partial(scan, final=False), 0)
        lax.fori_loop(trips // 2, trips, functools.partial(scan, final=True), 0)
    else:
        lax.fori_loop(0, trips, functools.partial(scan, final=False), 0)


def _gdn_call(p, gates, conv_w, hp, nw, s0f, s0b, period, with_out, sec0):
    b, t, _ = p.shape
    n_chunks = t // CHUNK
    prep_rows = max(period, min(t, 256))
    assert prep_rows % period == 0 and t % prep_rows == 0
    group = _scan_group(n_chunks, GDN_GROUP, with_out)
    g4 = gates[:, :, :4 * N_HEADS].reshape(b, t, 4, N_HEADS)
    g_col = g4.transpose(0, 3, 1, 2)
    g_row = jnp.pad(g4.transpose(0, 3, 2, 1), ((0, 0), (0, 0), (0, 4), (0, 0)))
    g_row = g_row.reshape(b, N_HEADS, 8, n_chunks, CHUNK).transpose(0, 1, 3, 2, 4)
    g_row = jnp.concatenate([g_row, g_row], axis=-1)

    nh = GDN_HEADS if with_out else GDN_CTX_HEADS
    assert N_HEADS % nh == 0
    wide = nh * HEAD_DIM
    col = lambda sec: pl.BlockSpec((None, t, wide), lambda b_, h: (b_, 0, sec * (N_HEADS // nh) + h))
    gc_spec = pl.BlockSpec((None, nh, t, 4), lambda b_, h: (b_, h, 0, 0))
    gr_spec = pl.BlockSpec((None, nh, n_chunks, 8, 2 * CHUNK), lambda b_, h: (b_, h, 0, 0, 0))
    conv_spec = lambda sec: pl.BlockSpec((3, wide), lambda b_, h: (0, sec * (N_HEADS // nh) + h))
    hp_spec = pl.BlockSpec((nh, 8, HEAD_DIM), lambda b_, h: (h, 0, 0))
    state_spec = pl.BlockSpec((None, nh, HEAD_DIM, HEAD_DIM), lambda b_, h: (b_, h, 0, 0))
    seq = lambda: pltpu.VMEM((nh, t, HEAD_DIM), F32)
    mats = lambda dt: pltpu.VMEM((nh, 2, n_chunks, HEAD_DIM, HEAD_DIM), dt)
    dc = pltpu.VMEM((nh, 2, n_chunks, 8, HEAD_DIM), F32)
    state = lambda: pltpu.VMEM((nh, HEAD_DIM, HEAD_DIM), F32)
    kern = functools.partial(_gdn_kernel, n_chunks=n_chunks, period=period, prep_rows=prep_rows,
                             group=group, igroup=_scan_group(n_chunks, GDN_INTRA_GROUP, False),
                             heads=nh, with_out=with_out)
    if with_out:
        return pl.pallas_call(
            kern,
            grid=(b, N_HEADS // nh),
            in_specs=[col(sec0), col(sec0 + 1), col(sec0 + 2), col(sec0 + 3),
                      gc_spec, gr_spec, conv_spec(0), conv_spec(1), conv_spec(2), hp_spec,
                      pl.BlockSpec((1, HEAD_DIM), lambda b_, h: (0, 0)), state_spec, state_spec],
            out_specs=pl.BlockSpec((None, t, wide), lambda b_, h: (b_, 0, h)),
            out_shape=jax.ShapeDtypeStruct((b, t, N_HEADS * HEAD_DIM), BF16),
            scratch_shapes=[seq(), seq(), seq(), mats(BF16), mats(F32), dc,
                            pltpu.VMEM((nh, 2, t, HEAD_DIM), BF16), seq(), state(), state()],
            compiler_params=_cparams("parallel", "parallel"),
            name="gdn_x",
        )(p, p, p, p, g_col, g_row, conv_w, conv_w, conv_w, hp, nw, s0f, s0b)
    shp = jax.ShapeDtypeStruct((b, N_HEADS, HEAD_DIM, HEAD_DIM), F32)
    return pl.pallas_call(
        kern,
        grid=(b, N_HEADS // nh),
        in_specs=[col(sec0 + 1), col(sec0 + 2), gc_spec, gr_spec, conv_spec(1), conv_spec(2), hp_spec],
        out_specs=[state_spec, state_spec],
        out_shape=[shp, shp],
        scratch_shapes=[seq(), seq(), mats(BF16), mats(F32), dc],
        compiler_params=_cparams("parallel", "parallel"),
        name="gdn_ctx",
    )(p, p, g_col, g_row, conv_w, conv_w, hp)


def _outproj_kernel(mh_ref, mg_ref, x_ref, gt_ref, sc_ref, sh_ref, nw_ref, woh_ref, wog_ref,
                    wrh_ref, wrl_ref, br_ref, x1_ref, h2_ref, idx_ref, gate_ref, cnt_ref):
    y = _dot(mh_ref[...], woh_ref[...]) + _dot(mg_ref[...], wog_ref[...])
    x1 = x_ref[...] + gt_ref[...] * y
    x1_ref[...] = x1
    h = _modulated_norm(x1, nw_ref[...], sc_ref[...], sh_ref[...])
    hh = h.astype(BF16)
    half = hh.shape[1] // 2
    hi_bits = lax.bitcast_convert_type(hh[:, :half].astype(F32), jnp.uint32)
    lo_bits = lax.bitcast_convert_type(hh[:, half:].astype(F32), jnp.uint32)
    h2_ref[...] = hi_bits | (lo_bits >> 16)
    hl = (h - hh.astype(F32)).astype(BF16)
    wrh = wrh_ref[...]
    logits = _dot(hh, wrh) + _dot(hl, wrh) + _dot(hh, wrl_ref[...]) + br_ref[...]
    lane = lax.broadcasted_iota(jnp.int32, logits.shape, 1).astype(F32)
    vals, idxs = [], []
    for _ in range(TOP_K):
        m = jnp.max(logits, axis=-1, keepdims=True)
        i = jnp.min(jnp.where(logits == m, lane, float(LANES)), axis=-1, keepdims=True)
        vals.append(m)
        idxs.append(i)
        logits = jnp.where(lane == i, -jnp.inf, logits)
    es = [jnp.exp(v - vals[0]) for v in vals]
    inv = 1.0 / functools.reduce(lambda a, b_: a + b_, es)

    @pl.when(pl.program_id(0) == 0)
    def _():
        cnt_ref[...] = jnp.zeros_like(cnt_ref)

    onehots = [lane == i for i in idxs]
    picked = functools.reduce(lambda a, b_: a | b_, onehots)
    tm = lane.shape[0]
    before = (lax.broadcasted_iota(jnp.int32, (tm, tm), 1) < lax.broadcasted_iota(jnp.int32, (tm, tm), 0))
    prior = cnt_ref[...] + _dot(before.astype(BF16), picked.astype(BF16))
    ranks = [jnp.sum(jnp.where(oh, prior, 0.0), axis=-1, keepdims=True) for oh in onehots]
    cnt_ref[...] += jnp.sum(picked.astype(F32), axis=0, keepdims=True)

    idx_out = jnp.zeros(lane.shape, F32)
    gate_out = jnp.zeros(lane.shape, F32)
    for k in range(TOP_K):
        idx_out = jnp.where(lane == k, idxs[k], idx_out)
        idx_out = jnp.where(lane == TOP_K + k, ranks[k], idx_out)
        gate_out = jnp.where(lane == k, es[k] * inv, gate_out)
    idx_ref[...] = idx_out.astype(jnp.int32)
    gate_ref[...] = gate_out


def _outproj(mix_h, mix_g, x, gt1, sc2, sh2, nw, w_out2, wr_hi, wr_lo, br):
    b, t, d = x.shape
    w = mix_h.shape[-1]
    tm = min(t, 512)
    tpb = t // tm
    row = lambda width: pl.BlockSpec((None, tm, width), lambda i: (i // tpb, i % tpb, 0))
    per_b = pl.BlockSpec((None, 1, d), lambda i: (i // tpb, 0, 0))
    const = lambda r, c: pl.BlockSpec((r, c), lambda i: (0, 0))
    return pl.pallas_call(
        _outproj_kernel,
        grid=(b * tpb,),
        in_specs=[row(w), row(w), row(d), per_b, per_b, per_b, const(1, d),
                  pl.BlockSpec((None, w, d), lambda i: (0, 0, 0)), pl.BlockSpec((None, w, d), lambda i: (1, 0, 0)),
                  const(d, LANES), const(d, LANES), const(1, LANES)],
        out_specs=[row(d), row(d // 2), row(LANES), row(LANES), const(1, LANES)],
        out_shape=[jax.ShapeDtypeStruct((b, t, d), F32), jax.ShapeDtypeStruct((b, t, d // 2), jnp.uint32),
                   jax.ShapeDtypeStruct((b, t, LANES), jnp.int32), jax.ShapeDtypeStruct((b, t, LANES), F32),
                   jax.ShapeDtypeStruct((1, LANES), F32)],
        compiler_params=_cparams("arbitrary"),
        name="outproj",
    )(mix_h, mix_g, x, gt1, sc2, sh2, nw.reshape(1, d), w_out2, w_out2, wr_hi, wr_lo, br)


def _dispatch_kernel(dest_ref, zrow_ref, h_ref, xs_ref, zbuf, sem, zsem, *, tt, k, tm, n_exp):
    @pl.when(pl.program_id(0) == 0)
    def _():
        zbuf[...] = jnp.zeros_like(zbuf)
        zero_block = lambda e: pltpu.make_async_copy(
            zbuf, xs_ref.at[pl.ds(pl.multiple_of(zrow_ref[e], tm), tm), :], zsem)
        for e in range(n_exp):
            @pl.when(zrow_ref[e] >= 0)
            def _():
                zero_block(e).start()
        for e in range(n_exp):
            @pl.when(zrow_ref[e] >= 0)
            def _():
                zero_block(e).wait()

        def unused_block(b):
            return pltpu.make_async_copy(zbuf, xs_ref.at[pl.ds(pl.multiple_of(b * tm, tm), tm), :], zsem)

        def start_unused(b, carry):
            unused_block(b).start()
            return carry

        def wait_unused(b, carry):
            unused_block(b).wait()
            return carry

        lax.fori_loop(zrow_ref[n_exp], xs_ref.shape[0] // tm, start_unused, 0)
        lax.fori_loop(zrow_ref[n_exp], xs_ref.shape[0] // tm, wait_unused, 0)

    base = pl.program_id(0) * (tt * k)

    def body(tb, carry):
        t0 = pl.multiple_of(tb * 8, 8)
        for u in range(8):
            for kk in range(k):
                row = dest_ref[base + (t0 + u) * k + kk]
                pltpu.make_async_copy(h_ref.at[pl.ds(t0 + u, 1), :], xs_ref.at[pl.ds(row, 1), :],
                                      sem).start(priority=(u * k + kk) % 2)
        return carry

    lax.fori_loop(0, tt // 8, body, 0)
    done = xs_ref.at[pl.ds(0, tt * k), :]
    pltpu.make_async_copy(done, done, sem).wait()


def _dispatch(h2, dest, zrow, rows, k, tm):
    n_tok, w = h2.shape
    tt = min(n_tok, 512)
    n_exp = zrow.shape[0] - 1
    return pl.pallas_call(
        functools.partial(_dispatch_kernel, tt=tt, k=k, tm=tm, n_exp=n_exp),
        grid_spec=pltpu.PrefetchScalarGridSpec(
            num_scalar_prefetch=2,
            grid=(n_tok // tt,),
            in_specs=[pl.BlockSpec((tt, w), lambda i, dest_, zrow_: (i, 0))],
            out_specs=pl.BlockSpec(memory_space=pl.ANY),
            scratch_shapes=[pltpu.VMEM((tm, w), h2.dtype), pltpu.SemaphoreType.DMA(()),
                            pltpu.SemaphoreType.DMA(())]),
        out_shape=jax.ShapeDtypeStruct((rows, w), h2.dtype),
        compiler_params=_cparams("arbitrary"),
        name="dispatch",
    )(dest, zrow, h2)


def _cast_tile(src, dst):
    rows = 256

    def body(i, carry):
        sl = pl.ds(pl.multiple_of(i * rows, rows), rows)
        dst[sl, :] = src[sl, :].astype(dst.dtype)
        return carry

    lax.fori_loop(0, src.shape[0] // rows, body, 0)


S_E, S_J, S_R, S_RO, S_JO, S_FIRST, S_VALID, S_NE, S_NJ, S_SLOT = range(10)


def _weight_group_prefetch(sc_ref, copies, on_ready):
    s = pl.program_id(0)

    @pl.when(s == 0)
    def _():
        for c in copies(sc_ref[S_E, 0], sc_ref[S_J, 0], 0):
            c.start()

    @pl.when(sc_ref[S_FIRST, s] == 1)
    def _():
        slot = sc_ref[S_SLOT, s]

        @pl.when(sc_ref[S_NE, s] >= 0)
        def _():
            for c in copies(sc_ref[S_NE, s], sc_ref[S_NJ, s], 1 - slot):
                c.start()

        for c in copies(sc_ref[S_E, s], sc_ref[S_J, s], slot):
            c.wait()
        on_ready(slot)


def _moe_up_kernel(sc_ref, x_ref, bg_ref, bu_ref, wg_hbm, wu_hbm, act_ref, stage, wg_b, wu_b, sems, *, tf):
    s = pl.program_id(0)

    def copies(e, j, slot):
        cols = pl.ds(pl.multiple_of(j * tf, tf), tf)
        return [pltpu.make_async_copy(wg_hbm.at[e, :, cols], stage.at[slot, 0], sems.at[slot, 0]),
                pltpu.make_async_copy(wu_hbm.at[e, :, cols], stage.at[slot, 1], sems.at[slot, 1])]

    def on_ready(slot):
        _cast_tile(stage.at[slot, 0], wg_b)
        _cast_tile(stage.at[slot, 1], wu_b)

    _weight_group_prefetch(sc_ref, copies, on_ready)

    @pl.when(sc_ref[S_VALID, s] == 1)
    def _():
        xp = x_ref[...]
        half = xp.shape[1]
        xa = lax.bitcast_convert_type(xp & jnp.uint32(0xFFFF0000), F32).astype(BF16)
        xb = lax.bitcast_convert_type(xp << 16, F32).astype(BF16)
        gate = _dot(xa, wg_b[:half, :]) + _dot(xb, wg_b[half:, :]) + bg_ref[...]
        up = _dot(xa, wu_b[:half, :]) + _dot(xb, wu_b[half:, :]) + bu_ref[...]
        gate = jnp.minimum(gate, SWIGLU_LIMIT)
        up = jnp.clip(up, -SWIGLU_LIMIT, SWIGLU_LIMIT)
        act_ref[...] = ((up + 1.0) * gate * _sigmoid(SWIGLU_ALPHA * gate)).astype(act_ref.dtype)

    @pl.when(sc_ref[S_VALID, s] == 0)
    def _():
        act_ref[...] = jnp.zeros_like(act_ref)


def _moe_down_kernel(sc_ref, a_ref, bd_ref, wd_hbm, y_ref, stage, wd_b, sems, *, tf):
    s = pl.program_id(0)

    def copies(e, j, slot):
        del j
        return [pltpu.make_async_copy(wd_hbm.at[e], stage.at[slot], sems.at[slot])]

    _weight_group_prefetch(sc_ref, copies, lambda slot: _cast_tile(stage.at[slot], wd_b))

    @pl.when(sc_ref[S_VALID, s] == 1)
    def _():
        cols = pl.ds(pl.multiple_of(sc_ref[S_J, s] * tf, tf), tf)
        y_ref[...] = _dot(a_ref[...], wd_b[:, cols]) + bd_ref[...]

    @pl.when(sc_ref[S_VALID, s] == 0)
    def _():
        y_ref[...] = jnp.zeros_like(y_ref)


def _moe_experts(xs, sched_up, sched_down, wg, wu, wd, bg, bu, bd, tm, tf):
    rows = xs.shape[0]
    n_exp, d, f = wg.shape
    assert xs.shape[1] * 2 == d and d // tf == f // tf
    steps = sched_up.shape[1]
    blk = lambda shape, at: pl.BlockSpec(shape, lambda s, sc: at(sc, s))
    hbm = pl.BlockSpec(memory_space=pl.ANY)
    act = pl.pallas_call(
        functools.partial(_moe_up_kernel, tf=tf),
        grid_spec=pltpu.PrefetchScalarGridSpec(
            num_scalar_prefetch=1,
            grid=(steps,),
            in_specs=[blk((tm, d // 2), lambda sc, s: (sc[S_R, s], 0)),
                      blk((None, 1, tf), lambda sc, s: (sc[S_E, s], 0, sc[S_J, s])),
                      blk((None, 1, tf), lambda sc, s: (sc[S_E, s], 0, sc[S_J, s])),
                      hbm, hbm],
            out_specs=blk((tm, tf), lambda sc, s: (sc[S_RO, s], sc[S_JO, s])),
            scratch_shapes=[pltpu.VMEM((2, 2, d, tf), F32), pltpu.VMEM((d, tf), BF16), pltpu.VMEM((d, tf), BF16),
                            pltpu.SemaphoreType.DMA((2, 2))]),
        out_shape=jax.ShapeDtypeStruct((rows, f), BF16),
        compiler_params=_cparams("arbitrary"),
        name="moe_up",
    )(sched_up, xs, bg.reshape(n_exp, 1, f), bu.reshape(n_exp, 1, f), wg, wu)
    return pl.pallas_call(
        functools.partial(_moe_down_kernel, tf=tf),
        grid_spec=pltpu.PrefetchScalarGridSpec(
            num_scalar_prefetch=1,
            grid=(steps,),
            in_specs=[blk((tm, f), lambda sc, s: (sc[S_R, s], 0)),
                      blk((None, 1, tf), lambda sc, s: (sc[S_E, s], 0, sc[S_J, s])),
                      hbm],
            out_specs=blk((tm, tf), lambda sc, s: (sc[S_RO, s], sc[S_JO, s])),
            scratch_shapes=[pltpu.VMEM((2, f, d), F32), pltpu.VMEM((f, d), BF16), pltpu.SemaphoreType.DMA((2,))]),
        out_shape=jax.ShapeDtypeStruct((rows, d), F32),
        compiler_params=_cparams("arbitrary"),
        name="moe_down",
    )(sched_down, act, bd.reshape(n_exp, 1, d), wd)


def _route(top_idx, rank, counts, tm, n_tiles):
    n_tok, k = top_idx.shape
    n_exp = counts.shape[0]
    m = n_tok * k
    padded = (counts + tm - 1) // tm * tm
    pend = jnp.cumsum(padded)
    onehot = top_idx.reshape(m, 1) == jnp.arange(n_exp, dtype=jnp.int32)[None, :]
    dest = jnp.sum(jnp.where(onehot, (pend - padded)[None, :], 0), axis=1) + rank.reshape(m)
    n_blocks = -(-m // tm) + n_exp

    nb = padded // tm
    blk0 = (pend - padded) // tm
    cum = jnp.cumsum(nb * n_tiles)
    total = cum[-1]
    s = jnp.arange(n_tiles * n_blocks, dtype=jnp.int32)
    valid = s < total
    sc = jnp.minimum(s, total - 1)
    e = jnp.minimum(jnp.sum((cum[None, :] <= sc[:, None]).astype(jnp.int32), axis=1), n_exp - 1)
    of_e = e[:, None] == jnp.arange(n_exp, dtype=jnp.int32)[None, :]
    pick = lambda v: jnp.sum(jnp.where(of_e, v[None, :], 0), axis=1)
    nb_e = pick(nb)
    local = sc - (pick(cum) - nb_e * n_tiles)
    nbe = jnp.maximum(nb_e, 1)
    blk0_e = pick(blk0)
    extra = s - total
    ids = jnp.arange(n_exp, dtype=jnp.int32)
    later = jnp.where((ids[None, :] > ids[:, None]) & (nb[None, :] > 0), ids[None, :], n_exp)
    next_e = jnp.min(later, axis=1)
    next_e = pick(jnp.where(next_e == n_exp, -1, next_e))
    erank = pick(jnp.cumsum((nb > 0).astype(jnp.int32)) - 1)

    def rows_of(j, r, first, ne, nj, group):
        r_out = jnp.where(valid, r, total // n_tiles + extra // n_tiles)
        j_out = jnp.where(valid, j, extra % n_tiles)
        fields = (e, j, r, r_out, j_out, valid & first, valid, ne, nj, group % 2)
        return jnp.stack([a.astype(jnp.int32) for a in fields])

    j_up = local // nbe
    last_tile = j_up == n_tiles - 1
    sched_up = rows_of(j_up, blk0_e + local % nbe, local % nbe == 0,
                       jnp.where(last_tile, next_e, e), jnp.where(last_tile, 0, j_up + 1),
                       erank * n_tiles + j_up)
    sched_down = rows_of(local % n_tiles, blk0_e + local // n_tiles, local == 0, next_e, jnp.zeros_like(e), erank)
    zrow = jnp.concatenate([jnp.where(padded > 0, pend - tm, -1), pend[-1:] // tm]).astype(jnp.int32)
    return dest.astype(jnp.int32), zrow, n_blocks * tm, sched_up, sched_down


def _combine_kernel(dest_ref, x1_ref, gate_ref, gt_ref, nw_ref, ys_ref, o_ref, ybuf, sems, *, tt, k):
    i = pl.program_id(0)
    n = pl.num_programs(0)

    def start_gather(tile, slot):
        base = tile * (tt * k)

        def body(tb, carry):
            for u in range(8):
                for j in range(k):
                    row = dest_ref[base + (tb * 8 + u) * k + j]
                    pltpu.make_async_copy(ys_ref.at[row >> 3, pl.ds(row & 7, 1), :],
                                          ybuf.at[slot, j * (tt // 8) + tb, pl.ds(u, 1), :],
                                          sems.at[slot]).start(priority=(u * k + j) % 2)
            return carry

        lax.fori_loop(0, tt // 8, body, 0)

    @pl.when(i == 0)
    def _():
        start_gather(0, 0)

    for nxt in (0, 1):
        @pl.when((i + 1 < n) & ((i + 1) % 2 == nxt))
        def _():
            start_gather(i + 1, nxt)

    slot = i % 2
    pltpu.make_async_copy(ybuf.at[slot], ybuf.at[slot], sems.at[slot]).wait()
    g = gate_ref[...]
    d = o_ref.shape[-1]
    picked = lambda j: ybuf[slot, pl.ds(j * (tt // 8), tt // 8), :, :].reshape(tt, d)
    moe = picked(0) * g[:, 0:1]
    for j in range(1, k):
        moe = moe + picked(j) * g[:, j:j + 1]
    x = x1_ref[...] + gt_ref[...] * moe
    ms = jnp.mean(x * x, axis=-1, keepdims=True)
    o_ref[...] = x * lax.rsqrt(ms + EPS) * nw_ref[...]


def _combine(x1, ys, dest, gates, gt2, nw, k):
    b, t, d = x1.shape
    tt = min(t, 256)
    tpb = t // tt
    row = lambda width: pl.BlockSpec((None, tt, width), lambda i, dest_: (i // tpb, i % tpb, 0))
    return pl.pallas_call(
        functools.partial(_combine_kernel, tt=tt, k=k),
        grid_spec=pltpu.PrefetchScalarGridSpec(
            num_scalar_prefetch=1,
            grid=(b * tpb,),
            in_specs=[row(d), row(LANES),
                      pl.BlockSpec((None, 1, d), lambda i, dest_: (i // tpb, 0, 0)),
                      pl.BlockSpec((1, d), lambda i, dest_: (0, 0)),
                      pl.BlockSpec(memory_space=pl.ANY)],
            out_specs=row(d),
            scratch_shapes=[pltpu.VMEM((2, k * tt // 8, 8, d), F32), pltpu.SemaphoreType.DMA((2,))]),
        out_shape=jax.ShapeDtypeStruct((b, t, d), F32),
        compiler_params=_cparams("arbitrary"),
        name="combine",
    )(dest, x1, gates, gt2, nw.reshape(1, d), ys.reshape(ys.shape[0] // 8, 8, d))


def kernel(x, c, ctx, c_ctx, w_ada, b_ada, norm_mix_w, w_in, hg_lb_f, hg_lb_b, hg_norm_w, gd_conv_w,
           gd_a_log_f, gd_a_log_b, gd_dt_bias_f, gd_dt_bias_b, gd_norm_w, w_out, norm_ffn_w, w_router,
           b_router, w_gate, b_gate, w_up, b_up, w_down, b_down, norm_out_w):
    bsz, t, d = x.shape
    t_ctx = ctx.shape[1]
    n_exp = w_router.shape[-1]
    assert w_ada.shape[0] == 1, "single-layer block: the context stream only feeds the scan states"
    l = 0
    hg_w = N_HEADS * HEAD_DIM
    n_main = 9 * hg_w

    lb_f = jnp.cumsum(jax.nn.softmax(hg_lb_f.astype(F32), axis=0), axis=0)[l].reshape(1, hg_w)
    lb_b = jnp.cumsum(jax.nn.softmax(hg_lb_b.astype(F32), axis=0), axis=0)[l].reshape(1, hg_w)

    rows = -(-(bsz + 1) // 8) * 8
    cc = jnp.zeros((rows, d), F32).at[:bsz].set(c).at[bsz].set(c_ctx)
    mod = _ada(cc, w_ada[l], b_ada[l])
    sh1, sc1, gt1, sh2, sc2, gt2 = (mod[:bsz, i * d:(i + 1) * d].reshape(bsz, 1, d) for i in range(6))
    csh1 = mod[bsz, 0:d].reshape(1, 1, d)
    csc1 = mod[bsz, d:2 * d].reshape(1, 1, d)

    w_in_b = w_in[l].astype(BF16)
    w_gates = jnp.pad(w_in_b[:, n_main:], ((0, 0), (0, LANES - (w_in_b.shape[1] - n_main))))
    px, gx = _inproj(x, norm_mix_w[l], sc1, sh1, w_in_b, n_main, w_gates)
    pc, gc = _inproj(ctx.reshape(1, bsz * t_ctx, d), norm_mix_w[l], csc1, csh1, w_in_b, n_main, w_gates)
    pc = pc.reshape(bsz, t_ctx, n_main)
    gc = gc.reshape(bsz, t_ctx, LANES)

    hg_nw = hg_norm_w[l].reshape(1, HEAD_DIM)
    hs_f, hs_b = _hgrn_states(pc, lb_f, lb_b)
    mix_h = _hgrn_out(px, lb_f, lb_b, hg_nw, hs_f, hs_b)

    hp = jnp.stack([gd_a_log_f[l], gd_dt_bias_f[l], gd_a_log_b[l], gd_dt_bias_b[l]], axis=1)
    hp = jnp.broadcast_to(jnp.pad(hp, ((0, 0), (0, 4)))[:, :, None], (N_HEADS, 8, HEAD_DIM)).astype(F32)
    gd_nw = gd_norm_w[l].reshape(1, HEAD_DIM)
    gs_f, gs_b = _gdn_call(pc, gc, gd_conv_w[l], hp, gd_nw, None, None, t_ctx, False, 5)
    mix_g = _gdn_call(px, gx, gd_conv_w[l], hp, gd_nw, gs_f, gs_b, GRID_W, True, 5)

    w_out_b = w_out[l].astype(BF16)
    wr = jnp.pad(w_router[l], ((0, 0), (0, LANES - n_exp)))
    wr_hi = wr.astype(BF16)
    wr_lo = (wr - wr_hi.astype(F32)).astype(BF16)
    br = jnp.full((1, LANES), NEG_BIG, F32).at[0, :n_exp].set(b_router[l])
    x1, h2, idx_pad, gate_pad, cnt = _outproj(mix_h, mix_g, x, gt1, sc2, sh2, norm_ffn_w[l],
                                              w_out_b.reshape(2, hg_w, d), wr_hi, wr_lo, br)

    n_tok = bsz * t
    idx_pad = idx_pad.reshape(n_tok, LANES)
    dest, zrow, n_rows, sched_up, sched_down = _route(idx_pad[:, :TOP_K], idx_pad[:, TOP_K:2 * TOP_K],
                                                      cnt[0, :n_exp].astype(jnp.int32), MOE_TM,
                                                      w_gate.shape[-1] // MOE_TF)
    xs = _dispatch(h2.reshape(n_tok, d // 2), dest, zrow, n_rows, TOP_K, MOE_TM)
    ys = _moe_experts(xs, sched_up, sched_down, w_gate[l], w_up[l], w_down[l], b_gate[l], b_up[l], b_down[l],
                      MOE_TM, MOE_TF)

    return _combine(x1, ys, dest, gate_pad, gt2, norm_out_w, TOP_K)
```
